```python
import math
import jax
import jax.numpy as jnp
from jax import lax
import numpy as np

D_MODEL = 1024
BATCH = 32
SEQ = 2048
DEPTH = 2

CTX_LEN = 256
GRID_W = 64
NORM_EPS = 1e-6
N_BRANCH = 3
N_MOD = 6
BRANCH_DIM = D_MODEL // 2

SGU_GROUP_DIM = 128
SGU_GROUPS = BRANCH_DIM // SGU_GROUP_DIM
SGU_DIM = SGU_GROUPS * SGU_GROUP_DIM
SGU_CHUNK = 128

NA_HEAD_DIM = 64
NA_HEADS = BRANCH_DIM // NA_HEAD_DIM
NA_DIM = NA_HEADS * NA_HEAD_DIM
NA_WIN_ROWS = 8
NA_WIN_COLS = 16
ROPE_THETA = 10000.0
NEG_INF = -1e30

HY_DIM = BRANCH_DIM
HY_ORDER = 2
HY_EMB = 33
HY_FILTER_HIDDEN = 64
HY_SHORT_CONV = 3
HY_FAST_DECAY_PCT = 0.3
HY_SLOW_DECAY_PCT = 1.5
HY_DECAY_TARGET = 1e-2

D_FF = 128 * ((8 * D_MODEL + 3 * 128 - 1) // (3 * 128))
FFN_CONV = 3

OFF_SGU = 0
OFF_QKV = OFF_SGU + 2 * SGU_DIM
OFF_HY = OFF_QKV + 3 * NA_DIM
OFF_GATE = OFF_HY + (HY_ORDER + 1) * HY_DIM
IN_COLS = OFF_GATE + N_BRANCH * D_MODEL

kernel_name = 'hybrid_diffusion_trunk'


def rmsnorm(x, g):
    x32 = x.astype(jnp.float32)
    y = x32 * lax.rsqrt(jnp.mean(x32 * x32, axis=-1, keepdims=True) + NORM_EPS)
    return y.astype(x.dtype) * g


def modulate(x, shift, scale):
    return x * (1 + scale) + shift


def depthwise_conv_centered(x, w, b):
    k_w = w.shape[0]
    L = x.shape[1]
    left = (k_w - 1) // 2
    xp = jnp.pad(x, ((0, 0), (left, k_w - 1 - left), (0, 0)))
    y = xp[:, 0:L] * w[0]
    for j in range(1, k_w):
        y = y + xp[:, j:j + L] * w[j]
    return y + b


def spatial_gating_mixer(uv, norm_g, w_s, b_s):
    B, L, _ = uv.shape
    u, v = jnp.split(jax.nn.gelu(uv, approximate=False), 2, axis=-1)
    v = rmsnorm(v, norm_g).reshape(B, L // SGU_CHUNK, SGU_CHUNK, SGU_GROUPS, SGU_GROUP_DIM)
    v = jnp.einsum('gqp,bnpgc->bnqgc', w_s, v) + b_s.T[None, None, :, :, None]
    return u * v.reshape(B, L, SGU_DIM)


def hyena_filter_spectrum(L, w1, b1, w2, b2, w3, sin_freq):
    f32 = jnp.float32
    t = jnp.linspace(0.0, 1.0, L, dtype=f32)[:, None]
    n_bands = (HY_EMB - 1) // 2
    bands = jnp.linspace(1e-4, n_bands - 1, n_bands, dtype=f32)[None, :]
    ang = (2.0 * math.pi) * jnp.arange(L, dtype=f32)[:, None] / L * bands
    z = jnp.concatenate([t, jnp.cos(ang), -jnp.sin(ang)], axis=-1)
    freq = sin_freq.astype(f32)
    h = jnp.sin(freq * (z @ w1.astype(f32) + b1.astype(f32)))
    h = jnp.sin(freq * (h @ w2.astype(f32) + b2.astype(f32)))
    h = (h @ w3.astype(f32)).reshape(L, HY_ORDER, 2, HY_DIM)
    max_decay = math.log(HY_DECAY_TARGET) / HY_FAST_DECAY_PCT
    min_decay = math.log(HY_DECAY_TARGET) / HY_SLOW_DECAY_PCT
    deltas = jnp.abs(jnp.linspace(min_decay, max_decay, HY_DIM, dtype=f32))
    h = h * jnp.exp(-t * deltas)[:, None, None, :]
    k = jnp.concatenate([h[:, :, 0], jnp.zeros((1, HY_ORDER, HY_DIM), f32), h[:0:-1, :, 1]], axis=0)
    k = k / jnp.sum(jnp.abs(k), axis=0, keepdims=True)
    return jnp.fft.rfft(k, axis=0)


def long_conv(z, kf, skip):
    L = z.shape[1]
    zf = jnp.fft.rfft(z, n=2 * L, axis=1)
    y = jnp.fft.irfft(zf * kf[None], n=2 * L, axis=1)[:, :L]
    return y + z * skip


def hyena_mixer(p, lp):
    L = p.shape[1]
    pc = depthwise_conv_centered(p, lp['hy_conv_w'], lp['hy_conv_b']).astype(jnp.float32)
    parts = jnp.split(pc, HY_ORDER + 1, axis=-1)
    kf = hyena_filter_spectrum(L, lp['hy_filt_w1'], lp['hy_filt_b1'], lp['hy_filt_w2'],
                               lp['hy_filt_b2'], lp['hy_filt_w3'], lp['hy_sin_freq'])
    skip = lp['hy_skip'].astype(jnp.float32)
    z = parts[0]
    for n in range(HY_ORDER):
        z = parts[n + 1] * long_conv(z, kf[:, n], skip[n])
    return z.astype(p.dtype)


def axial_rope(x, rows, cols):
    f32 = jnp.float32
    half = NA_HEAD_DIM // 2
    quarter = half // 2
    inv_freq = ROPE_THETA ** (-jnp.arange(quarter, dtype=f32) * 2.0 / half)

    def rotate(xa, pos):
        ang = pos.astype(f32)[:, None] * inv_freq[None, :]
        cos = jnp.cos(ang)[None, :, None, :]
        sin = jnp.sin(ang)[None, :, None, :]
        a, b = xa[..., :quarter], xa[..., quarter:]
        return jnp.concatenate([a * cos - b * sin, a * sin + b * cos], axis=-1)

    x32 = x.astype(f32)
    out = jnp.concatenate([rotate(x32[..., :half], rows), rotate(x32[..., half:], cols)], axis=-1)
    return out.astype(x.dtype)


def neighborhood_attention(q, k, v, kc, vc, rpb):
    B, L, H, hd = q.shape
    rows = L // GRID_W
    kr = min(NA_WIN_ROWS, rows)
    scale = hd ** -0.5
    qb = q.reshape(B, rows, GRID_W, H, hd)
    kb = k.reshape(B, rows, GRID_W, H, hd)
    vb = v.reshape(B, rows, GRID_W, H, hd)
    col = jnp.arange(GRID_W)
    cstart = jnp.clip(col - NA_WIN_COLS // 2, 0, GRID_W - NA_WIN_COLS)
    col_mask = (col[None, :] >= cstart[:, None]) & (col[None, :] < cstart[:, None] + NA_WIN_COLS)
    dc_idx = jnp.clip(col[None, :] - col[:, None] + NA_WIN_COLS - 1, 0, 2 * NA_WIN_COLS - 2)

    def row_block(r):
        r0 = jnp.clip(r - kr // 2, 0, rows - kr)
        q_r = lax.dynamic_index_in_dim(qb, r, axis=1, keepdims=False)
        k_r = lax.dynamic_slice_in_dim(kb, r0, kr, axis=1)
        v_r = lax.dynamic_slice_in_dim(vb, r0, kr, axis=1)
        dr_idx = r0 + jnp.arange(kr) - r + NA_WIN_ROWS - 1
        bias = rpb[:, dr_idx][:, :, dc_idx].transpose(0, 2, 1, 3)
        s_loc = jnp.einsum('bqhd,brkhd->bhqrk', q_r, k_r).astype(jnp.float32) * scale
        s_loc = s_loc + bias[None].astype(jnp.float32)
        s_loc = jnp.where(col_mask[None, None, :, None, :], s_loc, NEG_INF)
        s_ctx = jnp.einsum('bqhd,bchd->bhqc', q_r, kc).astype(jnp.float32) * scale
        s = jnp.concatenate([s_loc.reshape(B, H, GRID_W, kr * GRID_W), s_ctx], axis=-1)
        p = jax.nn.softmax(s, axis=-1).astype(v.dtype)
        p_loc = p[..., :kr * GRID_W].reshape(B, H, GRID_W, kr, GRID_W)
        p_ctx = p[..., kr * GRID_W:]
        return (jnp.einsum('bhqrk,brkhd->bqhd', p_loc, v_r)
                + jnp.einsum('bhqc,bchd->bqhd', p_ctx, vc))

    out = lax.map(row_block, jnp.arange(rows))
    return out.transpose(1, 0, 2, 3, 4).reshape(B, L, H * hd)


def context_attention(qc, kc, vc):
    B, Lc, H, hd = qc.shape
    s = jnp.einsum('bqhd,bkhd->bhqk', qc, kc).astype(jnp.float32) * (hd ** -0.5)
    p = jax.nn.softmax(s, axis=-1).astype(vc.dtype)
    return jnp.einsum('bhqk,bkhd->bqhd', p, vc).reshape(B, Lc, H * hd)


def split_heads(t):
    B, L, _ = t.shape
    return t.reshape(B, L, NA_HEADS, NA_HEAD_DIM)


def merge_branches(p, attn, lp):
    a = spatial_gating_mixer(p[..., OFF_SGU:OFF_QKV], lp['sgu_norm_g'], lp['sgu_w'], lp['sgu_b'])
    hy = hyena_mixer(p[..., OFF_HY:OFF_GATE], lp)
    g_a, g_b, g_c = jnp.split(jax.nn.sigmoid(p[..., OFF_GATE:]), N_BRANCH, axis=-1)
    wb = lp['w_branch']
    m = g_a * (a @ wb[0]) + g_b * (attn @ wb[1]) + g_c * (hy @ wb[2])
    return m @ lp['w_out']


def conv_ffn(h, lp):
    a, b = jnp.split(h @ lp['ffn_w_up'], 2, axis=-1)
    a = depthwise_conv_centered(a, lp['ffn_conv_w'], lp['ffn_conv_b'])
    return (jax.nn.gelu(a, approximate=False) * b) @ lp['ffn_w_down']


def layer_forward(x, xc, mod, mod_c, lp, update_ctx):
    B, L, _ = x.shape
    sh1, sc1, g1, sh2, sc2, g2 = jnp.split(mod, N_MOD, axis=-1)
    csh1, csc1, cg1, csh2, csc2, cg2 = jnp.split(mod_c, N_MOD, axis=-1)

    hc = modulate(rmsnorm(xc, lp['norm1_g']), csh1, csc1)
    if update_ctx:
        pc = hc @ lp['w_in']
        qc = split_heads(pc[..., OFF_QKV:OFF_QKV + NA_DIM])
        kc = split_heads(pc[..., OFF_QKV + NA_DIM:OFF_QKV + 2 * NA_DIM])
        vc = split_heads(pc[..., OFF_QKV + 2 * NA_DIM:OFF_HY])
    else:
        kvc = hc @ lp['w_in'][:, OFF_QKV + NA_DIM:OFF_HY]
        kc = split_heads(kvc[..., :NA_DIM])
        vc = split_heads(kvc[..., NA_DIM:])

    h = modulate(rmsnorm(x, lp['norm1_g']), sh1, sc1)
    p = h @ lp['w_in']
    pos = jnp.arange(L)
    rows, cols = pos // GRID_W, pos % GRID_W
    q = axial_rope(split_heads(p[..., OFF_QKV:OFF_QKV + NA_DIM]), rows, cols)
    k = axial_rope(split_heads(p[..., OFF_QKV + NA_DIM:OFF_QKV + 2 * NA_DIM]), rows, cols)
    v = split_heads(p[..., OFF_QKV + 2 * NA_DIM:OFF_HY])
    attn = neighborhood_attention(q, k, v, kc, vc, lp['na_rpb'])
    x = x + g1 * merge_branches(p, attn, lp)
    h = modulate(rmsnorm(x, lp['norm2_g']), sh2, sc2)
    x = x + g2 * conv_ffn(h, lp)

    if update_ctx:
        xc = xc + cg1 * merge_branches(pc, context_attention(qc, kc, vc), lp)
        hc2 = modulate(rmsnorm(xc, lp['norm2_g']), csh2, csc2)
        xc = xc + cg2 * conv_ffn(hc2, lp)
    return x, xc


def setup_inputs(seed: int = 0) -> dict:
    key = jax.random.key(seed)
    ks = jax.random.split(key, 32)
    f32 = jnp.float32
    D = D_MODEL

    def nrm(k, shape, scale):
        return jax.random.normal(k, shape, f32) * scale

    return {
        'x': nrm(ks[0], (BATCH, SEQ, D), 1.0),
        'c': nrm(ks[1], (BATCH, D), 1.0),
        'ctx': nrm(ks[2], (BATCH, CTX_LEN, D), 1.0),
        'c_ctx': nrm(ks[3], (D,), 1.0),
        'ada_w': nrm(ks[4], (DEPTH, D, N_MOD * D), D ** -0.5),
        'ada_b': nrm(ks[5], (DEPTH, N_MOD * D), 0.02),
        'norm1_g': 1.0 + nrm(ks[6], (DEPTH, D), 0.02),
        'norm2_g': 1.0 + nrm(ks[7], (DEPTH, D), 0.02),
        'w_in': nrm(ks[8], (DEPTH, D, IN_COLS), D ** -0.5),
        'sgu_norm_g': 1.0 + nrm(ks[9], (DEPTH, SGU_DIM), 0.02),
        'sgu_w': nrm(ks[10], (DEPTH, SGU_GROUPS, SGU_CHUNK, SGU_CHUNK), 0.5 * SGU_CHUNK ** -0.5),
        'sgu_b': 1.0 + nrm(ks[11], (DEPTH, SGU_GROUPS, SGU_CHUNK), 0.02),
        'na_rpb': nrm(ks[12], (DEPTH, NA_HEADS, 2 * NA_WIN_ROWS - 1, 2 * NA_WIN_COLS - 1), 0.1),
        'hy_conv_w': nrm(ks[13], (DEPTH, HY_SHORT_CONV, (HY_ORDER + 1) * HY_DIM), HY_SHORT_CONV ** -0.5),
        'hy_conv_b': nrm(ks[14], (DEPTH, (HY_ORDER + 1) * HY_DIM), 0.02),
        'hy_filt_w1': nrm(ks[15], (DEPTH, HY_EMB, HY_FILTER_HIDDEN), HY_EMB ** -0.5),
        'hy_filt_b1': nrm(ks[16], (DEPTH, HY_FILTER_HIDDEN), 0.02),
        'hy_filt_w2': nrm(ks[17], (DEPTH, HY_FILTER_HIDDEN, HY_FILTER_HIDDEN), HY_FILTER_HIDDEN ** -0.5),
        'hy_filt_b2': nrm(ks[18], (DEPTH, HY_FILTER_HIDDEN), 0.02),
        'hy_filt_w3': nrm(ks[19], (DEPTH, HY_FILTER_HIDDEN, HY_ORDER * 2 * HY_DIM), HY_FILTER_HIDDEN ** -0.5),
        'hy_sin_freq': 1.0 + nrm(ks[20], (DEPTH, HY_FILTER_HIDDEN), 0.02),
        'hy_skip': nrm(ks[21], (DEPTH, HY_ORDER, HY_DIM), 1.0),
        'w_branch': nrm(ks[22], (DEPTH, N_BRANCH, BRANCH_DIM, D), BRANCH_DIM ** -0.5),
        'w_out': nrm(ks[23], (DEPTH, D, D), D ** -0.5),
        'ffn_w_up': nrm(ks[24], (DEPTH, D, 2 * D_FF), D ** -0.5),
        'ffn_conv_w': nrm(ks[25], (DEPTH, FFN_CONV, D_FF), FFN_CONV ** -0.5),
        'ffn_conv_b': nrm(ks[26], (DEPTH, D_FF), 0.02),
        'ffn_w_down': nrm(ks[27], (DEPTH, D_FF, D), D_FF ** -0.5),
        'final_norm_g': 1.0 + nrm(ks[28], (D,), 0.02),
    }


def reference(x, c, ctx, c_ctx, ada_w, ada_b, norm1_g, norm2_g, w_in, sgu_norm_g, sgu_w, sgu_b,
              na_rpb, hy_conv_w, hy_conv_b, hy_filt_w1, hy_filt_b1, hy_filt_w2, hy_filt_b2,
              hy_filt_w3, hy_sin_freq, hy_skip, w_branch, w_out, ffn_w_up, ffn_conv_w, ffn_conv_b,
              ffn_w_down, final_norm_g):
    xc = ctx
    for i in range(DEPTH):
        lp = {
            'norm1_g': norm1_g[i], 'norm2_g': norm2_g[i], 'w_in': w_in[i],
            'sgu_norm_g': sgu_norm_g[i], 'sgu_w': sgu_w[i], 'sgu_b': sgu_b[i],
            'na_rpb': na_rpb[i], 'hy_conv_w': hy_conv_w[i], 'hy_conv_b': hy_conv_b[i],
            'hy_filt_w1': hy_filt_w1[i], 'hy_filt_b1': hy_filt_b1[i], 'hy_filt_w2': hy_filt_w2[i],
            'hy_filt_b2': hy_filt_b2[i], 'hy_filt_w3': hy_filt_w3[i], 'hy_sin_freq': hy_sin_freq[i],
            'hy_skip': hy_skip[i], 'w_branch': w_branch[i], 'w_out': w_out[i],
            'ffn_w_up': ffn_w_up[i], 'ffn_conv_w': ffn_conv_w[i], 'ffn_conv_b': ffn_conv_b[i],
            'ffn_w_down': ffn_w_down[i],
        }
        mod = (jax.nn.silu(c) @ ada_w[i] + ada_b[i])[:, None, :]
        mod_c = (jax.nn.silu(c_ctx) @ ada_w[i] + ada_b[i])[None, None, :]
        x, xc = layer_forward(x, xc, mod, mod_c, lp, i < DEPTH - 1)
    return rmsnorm(x, final_norm_g)
```

```python
import functools
import math

import jax
import jax.numpy as jnp
import numpy as np
from jax import lax
from jax.experimental import pallas as pl
from jax.experimental.pallas import tpu as pltpu

D_MODEL = 1024
DEPTH = 2
GRID_W = 64
NORM_EPS = 1e-6
N_MOD = 6
BRANCH_DIM = D_MODEL // 2
SGU_GROUP_DIM = 128
SGU_GROUPS = BRANCH_DIM // SGU_GROUP_DIM
SGU_CHUNK = 128
NA_HEAD_DIM = 64
NA_HEADS = BRANCH_DIM // NA_HEAD_DIM
NA_WIN_ROWS = 8
NA_WIN_COLS = 16
ROPE_THETA = 10000.0
NEG_INF = -1e30
HY_DIM = BRANCH_DIM
HY_ORDER = 2
HY_EMB = 33
HY_FILTER_HIDDEN = 64
HY_FAST_DECAY_PCT = 0.3
HY_SLOW_DECAY_PCT = 1.5
HY_DECAY_TARGET = 1e-2
D_FF = 128 * ((8 * D_MODEL + 3 * 128 - 1) // (3 * 128))
OFF_SGU = 0
OFF_QKV = OFF_SGU + 2 * BRANCH_DIM
OFF_HY = OFF_QKV + 3 * BRANCH_DIM
OFF_GATE = OFF_HY + (HY_ORDER + 1) * HY_DIM
IN_COLS = OFF_GATE + 3 * D_MODEL

LANES = 128
MXU_WIDTH = 256
VMEM_LIMIT_BYTES = 56 * 1024 * 1024

MXU_DTYPE = jnp.bfloat16
ACT_DTYPE = jnp.bfloat16
F32 = jnp.float32
HIGHEST = lax.Precision.HIGHEST

Q_ROWS_PER_BLOCK = 4
K_ROWS_PER_BLOCK = 12
ATT_TOK = Q_ROWS_PER_BLOCK * GRID_W


def _cparams(sem):
    return pltpu.CompilerParams(dimension_semantics=sem, vmem_limit_bytes=VMEM_LIMIT_BYTES)


def _const_spec(shape):
    nd = len(shape)
    return pl.BlockSpec(shape, lambda *_: (0,) * nd, pipeline_mode=pl.Buffered(1))


def _dot(a, b, **kw):
    return jnp.dot(a, b, preferred_element_type=F32, **kw)


def _dot_nt(a, b):
    return lax.dot_general(a, b, (((1,), (1,)), ((), ())), preferred_element_type=F32)


def _gelu(x):
    return 0.5 * x * (1.0 + lax.erf(x * (1.0 / math.sqrt(2.0))))


def _mod_body(c_ref, w_ref, b_ref, o_ref):
    c = c_ref[...]
    s = c * jax.nn.sigmoid(c)
    o_ref[...] = _dot(s, w_ref[...], precision=HIGHEST) + b_ref[...]


def _modulation(cc, ada_w, ada_b):
    depth, d, n = ada_w.shape
    r = cc.shape[0]
    tn = 1024
    return pl.pallas_call(
        _mod_body,
        grid=(depth, n // tn),
        in_specs=[
            pl.BlockSpec((r, d), lambda i, j: (0, 0)),
            pl.BlockSpec((None, d, tn), lambda i, j: (i, 0, j)),
            pl.BlockSpec((None, 1, tn), lambda i, j: (i, 0, j)),
        ],
        out_specs=pl.BlockSpec((None, r, tn), lambda i, j: (i, 0, j)),
        out_shape=jax.ShapeDtypeStruct((depth, r, n), F32),
        compiler_params=_cparams(("arbitrary", "arbitrary")),
        name="adaln_modulation",
    )(cc, ada_w, ada_b.reshape(depth, 1, n))


def _mod_spec(row_fn, chunk):
    return pl.BlockSpec((None, 1, D_MODEL), lambda *g: (row_fn(*g), 0, chunk))


def _rope(acc, cos, sin):
    lane = lax.broadcasted_iota(jnp.int32, cos.shape, 1)
    first_half = (lane % 32) < 16
    outs = []
    for c0 in range(0, acc.shape[1], LANES):
        xc = acc[:, c0:c0 + LANES]
        partner = jnp.where(first_half, pltpu.roll(xc, LANES - 16, 1), pltpu.roll(xc, 16, 1))
        outs.append(xc * cos + partner * sin)
    return outs


def _in_proj_body(*refs, segs, rope, n_out):
    x_ref, sh_ref, sc_ref, g_ref, w_ref = refs[:5]
    pos = 5
    if rope:
        cos_ref, sin_ref = refs[5:7]
        pos = 7
    out_refs = refs[pos:pos + n_out]
    x = x_ref[...]
    ms = jnp.mean(x * x, axis=-1, keepdims=True)
    y = x * lax.rsqrt(ms + NORM_EPS) * g_ref[...]
    h = (y * (1.0 + sc_ref[...]) + sh_ref[...]).astype(MXU_DTYPE)
    for (c0, width, kind, oi) in segs:
        step = min(width, 512)
        for cc in range(0, width, step):
            acc = _dot(h, w_ref[:, c0 + cc:c0 + cc + step])
            if kind == "gelu":
                acc = _gelu(acc)
            elif kind == "sigmoid":
                acc = jax.nn.sigmoid(acc)
            if kind == "rope" and rope:
                for k, piece in enumerate(_rope(acc, cos_ref[...], sin_ref[...])):
                    out_refs[oi][:, cc + k * LANES:cc + (k + 1) * LANES] = piece.astype(ACT_DTYPE)
            else:
                out_refs[oi][:, cc:cc + step] = acc.astype(ACT_DTYPE)


def _in_proj(x, mod, mod_row, norm_g, w, segs, rope_tabs, name):
    bx, l, d = x.shape
    tm = min(512, l)
    n_out = max(s[3] for s in segs) + 1
    widths = [sum(s[1] for s in segs if s[3] == oi) for oi in range(n_out)]
    rope = rope_tabs is not None
    in_specs = [
        pl.BlockSpec((None, tm, d), lambda b, i: (b, i, 0)),
        _mod_spec(lambda b, i: mod_row(b), 0),
        _mod_spec(lambda b, i: mod_row(b), 1),
        _const_spec((1, d)),
        _const_spec(w.shape),
    ]
    args = [x, mod, mod, norm_g.reshape(1, d), w]
    if rope:
        in_specs += [pl.BlockSpec((tm, LANES), lambda b, i: (i, 0))] * 2
        args += list(rope_tabs)
    return pl.pallas_call(
        functools.partial(_in_proj_body, segs=segs, rope=rope, n_out=n_out),
        grid=(bx, l // tm),
        in_specs=in_specs,
        out_specs=[pl.BlockSpec((None, tm, wd), lambda b, i: (b, i, 0)) for wd in widths],
        out_shape=[jax.ShapeDtypeStruct((bx, l, wd), ACT_DTYPE) for wd in widths],
        compiler_params=_cparams(("arbitrary", "arbitrary")),
        name=name,
    )(*args)


FULL_SEGS = (
    (OFF_SGU, 2 * BRANCH_DIM, "gelu", 0),
    (OFF_QKV, BRANCH_DIM, "rope", 1),
    (OFF_QKV + BRANCH_DIM, BRANCH_DIM, "rope", 2),
    (OFF_QKV + 2 * BRANCH_DIM, BRANCH_DIM, "plain", 3),
    (OFF_HY, 3 * HY_DIM, "plain", 4),
    (OFF_GATE, 3 * D_MODEL, "sigmoid", 5),
)
KV_SEGS = ((0, BRANCH_DIM, "plain", 0), (BRANCH_DIM, BRANCH_DIM, "plain", 1))


def _sgu_body(uv_ref, g_ref, w_ref, b_ref, o_ref):
    ts = uv_ref.shape[0]
    v = uv_ref[:, BRANCH_DIM:].astype(F32)
    ms = jnp.mean(v * v, axis=-1, keepdims=True)
    vb = (v * lax.rsqrt(ms + NORM_EPS) * g_ref[...]).astype(MXU_DTYPE)
    for n in range(ts // SGU_CHUNK):
        rs = slice(n * SGU_CHUNK, (n + 1) * SGU_CHUNK)
        for g in range(SGU_GROUPS):
            cs = slice(g * SGU_GROUP_DIM, (g + 1) * SGU_GROUP_DIM)
            mixed = _dot(w_ref[g], vb[rs, cs]) + b_ref[g]
            o_ref[rs, cs] = (uv_ref[rs, cs].astype(F32) * mixed).astype(ACT_DTYPE)


def _sgu(uv, norm_g, w_s, b_s):
    bx, l, _ = uv.shape
    ts = min(512, l)
    return pl.pallas_call(
        _sgu_body,
        grid=(bx, l // ts),
        in_specs=[
            pl.BlockSpec((None, ts, 2 * BRANCH_DIM), lambda b, i: (b, i, 0)),
            _const_spec((1, BRANCH_DIM)),
            _const_spec(w_s.shape),
            _const_spec((SGU_GROUPS, SGU_CHUNK, 1)),
        ],
        out_specs=pl.BlockSpec((None, ts, BRANCH_DIM), lambda b, i: (b, i, 0)),
        out_shape=jax.ShapeDtypeStruct((bx, l, BRANCH_DIM), ACT_DTYPE),
        compiler_params=_cparams(("arbitrary", "arbitrary")),
        name="spatial_gating",
    )(uv, norm_g.reshape(1, BRANCH_DIM), w_s.astype(MXU_DTYPE), b_s.reshape(SGU_GROUPS, SGU_CHUNK, 1))


def _att_plan(rows):
    kr = min(NA_WIN_ROWS, rows)
    assert rows % Q_ROWS_PER_BLOCK == 0 and rows >= K_ROWS_PER_BLOCK and kr == NA_WIN_ROWS
    patterns, type_of, key_start = [], [], []
    for rb in range(0, rows, Q_ROWS_PER_BLOCK):
        ks = min(max(rb - NA_WIN_ROWS // 2, 0), rows - K_ROWS_PER_BLOCK)
        assert ks % Q_ROWS_PER_BLOCK == 0
        pat = []
        for qi in range(Q_ROWS_PER_BLOCK):
            rq = rb + qi
            r0 = min(max(rq - kr // 2, 0), rows - kr)
            for j in range(K_ROWS_PER_BLOCK):
                rk = ks + j
                pat.append(rk - rq + NA_WIN_ROWS - 1 if r0 <= rk < r0 + kr else None)
        pat = tuple(pat)
        if pat not in patterns:
            patterns.append(pat)
        type_of.append(patterns.index(pat))
        key_start.append(ks // Q_ROWS_PER_BLOCK)
    return patterns, type_of, key_start


def _bias_body(rpb_ref, o_ref, *, patterns):
    n_dr, n_dc = 2 * NA_WIN_ROWS - 1, 2 * NA_WIN_COLS - 1
    h = pl.program_id(0)
    qc = lax.broadcasted_iota(jnp.int32, (GRID_W, GRID_W), 0)
    kc = lax.broadcasted_iota(jnp.int32, (GRID_W, GRID_W), 1)
    cstart = jnp.clip(qc - NA_WIN_COLS // 2, 0, GRID_W - NA_WIN_COLS)
    col_ok = (kc >= cstart) & (kc < cstart + NA_WIN_COLS)
    dc = jnp.clip(kc - qc + NA_WIN_COLS - 1, 0, n_dc - 1)
    neg = jnp.full((GRID_W, GRID_W), NEG_INF, F32)
    tiles = []
    for dr in range(n_dr):
        t = jnp.zeros((GRID_W, GRID_W), F32)
        for d in range(n_dc):
            t = jnp.where(dc == d, rpb_ref[(h * n_dr + dr) * n_dc + d], t)
        tiles.append(jnp.where(col_ok, t, neg))
    for ty, pat in enumerate(patterns):
        for qi in range(Q_ROWS_PER_BLOCK):
            for j in range(K_ROWS_PER_BLOCK):
                dr = pat[qi * K_ROWS_PER_BLOCK + j]
                o_ref[ty, qi * GRID_W:(qi + 1) * GRID_W, j * GRID_W:(j + 1) * GRID_W] = neg if dr is None else tiles[dr]


def _bias_tables(rpb, patterns):
    nt = len(patterns)
    return pl.pallas_call(
        functools.partial(_bias_body, patterns=patterns),
        grid=(NA_HEADS,),
        in_specs=[pl.BlockSpec(memory_space=pltpu.SMEM)],
        out_specs=pl.BlockSpec((nt, None, ATT_TOK, K_ROWS_PER_BLOCK * GRID_W), lambda h: (0, h, 0, 0)),
        out_shape=jax.ShapeDtypeStruct((nt, NA_HEADS, ATT_TOK, K_ROWS_PER_BLOCK * GRID_W), F32),
        compiler_params=_cparams(("arbitrary",)),
        name="attention_bias_tables",
    )(rpb.reshape(-1))


def _head_mask(shape, hh):
    lane = lax.broadcasted_iota(jnp.int32, shape, 1)
    return (lane >= hh * NA_HEAD_DIM) & (lane < (hh + 1) * NA_HEAD_DIM)


def _softmax_pv(scores, values):
    m = functools.reduce(jnp.maximum, [jnp.max(s, axis=-1, keepdims=True) for s in scores])
    es = [jnp.exp(s - m) for s in scores]
    den = functools.reduce(lambda a, b: a + b, [jnp.sum(e, axis=-1, keepdims=True) for e in es])
    acc = functools.reduce(lambda a, b: a + b, [_dot(e.astype(MXU_DTYPE), v) for e, v in zip(es, values)])
    return acc / den


def _attn_body(ty_ref, kb_ref, q_ref, k0, k1, k2, v0, v1, v2, kc_ref, vc_ref, bias_ref, o_ref):
    del ty_ref, kb_ref
    scale = NA_HEAD_DIM ** -0.5
    for p in range(NA_HEADS // 2):
        cs = slice(p * LANES, (p + 1) * LANES)
        qp = q_ref[:, cs] * scale
        ks = [k0[:, cs], k1[:, cs], k2[:, cs]]
        vs = [v0[:, cs], v1[:, cs], v2[:, cs], vc_ref[:, cs]]
        kc = kc_ref[:, cs]
        outs = []
        for hh in range(2):
            h = 2 * p + hh
            qm = jnp.where(_head_mask(qp.shape, hh), qp, jnp.zeros_like(qp))
            scores = [_dot_nt(qm, kj) + bias_ref[h, :, j * ATT_TOK:(j + 1) * ATT_TOK] for j, kj in enumerate(ks)]
            scores.append(_dot_nt(qm, kc))
            outs.append(_softmax_pv(scores, vs))
        o_ref[:, cs] = jnp.where(_head_mask(outs[0].shape, 0), outs[0], outs[1]).astype(ACT_DTYPE)


def _neighborhood_attention(q, k, v, kc, vc, bias, type_of, key_start):
    b, l, _ = q.shape
    lc = kc.shape[1]
    n_blk = l // ATT_TOK
    tok = lambda off: pl.BlockSpec((None, ATT_TOK, BRANCH_DIM), lambda r, bb, ty, kb: (bb, kb[r] + off, 0))
    ctx = pl.BlockSpec((None, lc, BRANCH_DIM), lambda r, bb, ty, kb: (bb, 0, 0))
    grid_spec = pltpu.PrefetchScalarGridSpec(
        num_scalar_prefetch=2,
        grid=(n_blk, b),
        in_specs=[
            pl.BlockSpec((None, ATT_TOK, BRANCH_DIM), lambda r, bb, ty, kb: (bb, r, 0)),
            tok(0), tok(1), tok(2), tok(0), tok(1), tok(2), ctx, ctx,
            pl.BlockSpec((None, NA_HEADS, ATT_TOK, K_ROWS_PER_BLOCK * GRID_W), lambda r, bb, ty, kb: (ty[r], 0, 0, 0)),
        ],
        out_specs=pl.BlockSpec((None, ATT_TOK, BRANCH_DIM), lambda r, bb, ty, kb: (bb, r, 0)),
    )
    return pl.pallas_call(
        _attn_body,
        grid_spec=grid_spec,
        out_shape=jax.ShapeDtypeStruct((b, l, BRANCH_DIM), ACT_DTYPE),
        compiler_params=_cparams(("arbitrary", "arbitrary")),
        name="neighborhood_attention",
    )(jnp.asarray(type_of, jnp.int32), jnp.asarray(key_start, jnp.int32), q, k, k, k, v, v, v, kc, vc, bias)


def _ctx_attn_body(q_ref, k_ref, v_ref, o_ref):
    scale = NA_HEAD_DIM ** -0.5
    for p in range(NA_HEADS // 2):
        cs = slice(p * LANES, (p + 1) * LANES)
        qp = q_ref[:, cs] * scale
        kp, vp = k_ref[:, cs], v_ref[:, cs]
        outs = []
        for hh in range(2):
            qm = jnp.where(_head_mask(qp.shape, hh), qp, jnp.zeros_like(qp))
            outs.append(_softmax_pv([_dot_nt(qm, kp)], [vp]))
        o_ref[:, cs] = jnp.where(_head_mask(outs[0].shape, 0), outs[0], outs[1]).astype(ACT_DTYPE)


def _context_attention(q, k, v):
    b, lc, _ = q.shape
    spec = pl.BlockSpec((None, lc, BRANCH_DIM), lambda bb: (bb, 0, 0))
    return pl.pallas_call(
        _ctx_attn_body,
        grid=(b,),
        in_specs=[spec, spec, spec],
        out_specs=spec,
        out_shape=jax.ShapeDtypeStruct((b, lc, BRANCH_DIM), ACT_DTYPE),
        compiler_params=_cparams(("arbitrary",)),
        name="context_attention",
    )(q, k, v)


HY_CW = 256


def _dft_consts(l):
    n = 2 * l
    idx = jnp.arange(l, dtype=jnp.int32)
    ang = ((idx[:, None] * idx[None, :]) % n).astype(F32) * (2.0 * math.pi / n)
    cmat = jnp.cos(ang).astype(MXU_DTYPE)
    smat = (-jnp.sin(ang)).astype(MXU_DTYPE)
    sign = jnp.where(idx % 2 == 0, 1.0, -1.0).astype(F32).reshape(l, 1)
    wf = jnp.where(idx == 0, 1.0 / n, 2.0 / n).astype(F32).reshape(l, 1)
    return cmat, smat, sign, wf


def _filter_consts(l):
    t = jnp.linspace(0.0, 1.0, l, dtype=F32)[:, None]
    n_bands = (HY_EMB - 1) // 2
    bands = jnp.linspace(1e-4, n_bands - 1, n_bands, dtype=F32)[None, :]
    ang = (2.0 * math.pi) * jnp.arange(l, dtype=F32)[:, None] / l * bands
    z = jnp.concatenate([t, jnp.cos(ang), -jnp.sin(ang)], axis=-1)
    z = jnp.pad(z, ((0, 0), (0, LANES - HY_EMB)))
    max_decay = math.log(HY_DECAY_TARGET) / HY_FAST_DECAY_PCT
    min_decay = math.log(HY_DECAY_TARGET) / HY_SLOW_DECAY_PCT
    deltas = jnp.abs(jnp.linspace(min_decay, max_decay, HY_DIM, dtype=F32)).reshape(1, HY_DIM)
    return z, t, deltas


def _filter_body(z_ref, w1_ref, b1_ref, w2_ref, b2_ref, fr_ref, w3a_ref, w3b_ref, t_ref, dl_ref,
                 sg_ref, wf_ref, c_ref, s_ref, kc_ref, ks_ref, kn_ref):
    l = z_ref.shape[0]
    fr = fr_ref[...]
    h = jnp.sin(fr * (_dot(z_ref[...], w1_ref[...], precision=HIGHEST) + b1_ref[...]))
    h = jnp.sin(fr * (_dot(h, w2_ref[...], precision=HIGHEST) + b2_ref[...]))
    decay = jnp.exp(-t_ref[...] * dl_ref[...])
    h0 = _dot(h, w3a_ref[...], precision=HIGHEST) * decay
    h1 = _dot(h, w3b_ref[...], precision=HIGHEST) * decay
    row = lax.broadcasted_iota(jnp.int32, h1.shape, 0)
    h1 = jnp.where(row == 0, 0.0, h1)
    den = jnp.sum(jnp.abs(h0), axis=0, keepdims=True) + jnp.sum(jnp.abs(h1), axis=0, keepdims=True)
    hs = (h0 + h1) / den
    hd = (h0 - h1) / den
    kn_ref[...] = jnp.sum(sg_ref[...] * hs, axis=0, keepdims=True) * (1.0 / (2 * l))
    kc_ref[...] = (wf_ref[...] * _dot(c_ref[...], hs.astype(MXU_DTYPE))).astype(kc_ref.dtype)
    ks_ref[...] = (wf_ref[...] * _dot(s_ref[...], hd.astype(MXU_DTYPE))).astype(ks_ref.dtype)


def _hyena_filters(l, dft, fconsts, w1, b1, w2, b2, w3, freq):
    cmat, smat, sign, wf = dft
    z, t, deltas = fconsts
    cw = HY_CW
    nb = HY_DIM // cw
    w1p = jnp.pad(w1, ((0, LANES - HY_EMB), (0, 0)))
    hid = HY_FILTER_HIDDEN
    small = lambda shape: pl.BlockSpec(shape, lambda n, cb: (0,) * len(shape))
    spec_out = pl.BlockSpec((None, l, cw), lambda n, cb: (n, 0, cb))
    return pl.pallas_call(
        _filter_body,
        grid=(HY_ORDER, nb),
        in_specs=[
            small((l, LANES)), small((LANES, hid)), small((1, hid)), small((hid, hid)), small((1, hid)), small((1, hid)),
            pl.BlockSpec((hid, cw), lambda n, cb: (0, n * 2 * nb + cb)),
            pl.BlockSpec((hid, cw), lambda n, cb: (0, n * 2 * nb + nb + cb)),
            small((l, 1)),
            pl.BlockSpec((1, cw), lambda n, cb: (0, cb)),
            small((l, 1)), small((l, 1)),
            _const_spec((l, l)), _const_spec((l, l)),
        ],
        out_specs=[spec_out, spec_out, pl.BlockSpec((None, 1, cw), lambda n, cb: (n, 0, cb))],
        out_shape=[jax.ShapeDtypeStruct((HY_ORDER, l, HY_DIM), ACT_DTYPE)] * 2
        + [jax.ShapeDtypeStruct((HY_ORDER, 1, HY_DIM), F32)],
        compiler_params=_cparams(("arbitrary", "arbitrary")),
        name="hyena_filter_spectrum",
    )(z, w1p, b1.reshape(1, hid), w2, b2.reshape(1, hid), freq.reshape(1, hid), w3, w3, t, deltas, sign, wf, cmat, smat)


def _conv3(x, w_ref, b_ref):
    l = x.shape[0]
    row = lax.broadcasted_iota(jnp.int32, x.shape, 0)
    prev = jnp.where(row == 0, 0.0, pltpu.roll(x, 1, 0))
    nxt = jnp.where(row == l - 1, 0.0, pltpu.roll(x, l - 1, 0))
    return prev * w_ref[0:1, :] + x * w_ref[1:2, :] + nxt * w_ref[2:3, :] + b_ref[...]


def _hyena_body(pv_ref, p1_ref, p2_ref, wv_ref, w1_ref, w2_ref, bv_ref, b1_ref, b2_ref, skip_ref,
                kc_ref, ks_ref, kn_ref, sg_ref, c_ref, s_ref, o_ref):
    z = _conv3(pv_ref[...].astype(F32), wv_ref, bv_ref)
    gates = ((p1_ref, w1_ref, b1_ref), (p2_ref, w2_ref, b2_ref))
    for n, (p_ref, w_ref, b_ref) in enumerate(gates):
        zb = z.astype(MXU_DTYPE)
        zc = _dot(c_ref[...], zb)
        zs = _dot(s_ref[...], zb)
        zn = jnp.sum(sg_ref[...] * z, axis=0, keepdims=True)
        kc, ks = kc_ref[n].astype(F32), ks_ref[n].astype(F32)
        yc = (zc * kc - zs * ks).astype(MXU_DTYPE)
        ys = (zc * ks + zs * kc).astype(MXU_DTYPE)
        y = _dot(c_ref[...], yc) + _dot(s_ref[...], ys) + sg_ref[...] * (zn * kn_ref[n])
        z = _conv3(p_ref[...].astype(F32), w_ref, b_ref) * (y + z * skip_ref[n:n + 1, :])
    o_ref[...] = z.astype(ACT_DTYPE)


def _hyena(p, conv_w, conv_b, skip, filters, dft):
    bx, l, _ = p.shape
    cmat, smat, sign, _ = dft
    kcs, kss, kns = filters
    cw = HY_CW
    nb = HY_DIM // cw
    pspec = lambda part: pl.BlockSpec((None, l, cw), lambda cb, b: (b, 0, part * nb + cb))
    wspec = lambda part: pl.BlockSpec((3, cw), lambda cb, b: (0, part * nb + cb))
    bspec = lambda part: pl.BlockSpec((1, cw), lambda cb, b: (0, part * nb + cb))
    fspec = pl.BlockSpec((HY_ORDER, l, cw), lambda cb, b: (0, 0, cb), pipeline_mode=pl.Buffered(1))
    return pl.pallas_call(
        _hyena_body,
        grid=(nb, bx),
        in_specs=[
            pspec(0), pspec(1), pspec(2), wspec(0), wspec(1), wspec(2), bspec(0), bspec(1), bspec(2),
            pl.BlockSpec((HY_ORDER, cw), lambda cb, b: (0, cb)),
            fspec, fspec,
            pl.BlockSpec((HY_ORDER, 1, cw), lambda cb, b: (0, 0, cb)),
            _const_spec((l, 1)), _const_spec((l, l)), _const_spec((l, l)),
        ],
        out_specs=pl.BlockSpec((None, l, cw), lambda cb, b: (b, 0, cb)),
        out_shape=jax.ShapeDtypeStruct((bx, l, HY_DIM), ACT_DTYPE),
        compiler_params=_cparams(("arbitrary", "arbitrary")),
        name="hyena_long_conv",
    )(p, p, p, conv_w, conv_w, conv_w, conv_b.reshape(1, -1), conv_b.reshape(1, -1), conv_b.reshape(1, -1),
      skip, kcs, kss, kns, sign, cmat, smat)


def _merge_body(x_ref, a_ref, at_ref, hy_ref, gt_ref, g1_ref, wb_ref, wo_ref, o_ref):
    d = D_MODEL
    m = gt_ref[:, 0:d].astype(F32) * _dot(a_ref[...], wb_ref[0])
    m = m + gt_ref[:, d:2 * d].astype(F32) * _dot(at_ref[...], wb_ref[1])
    m = m + gt_ref[:, 2 * d:3 * d].astype(F32) * _dot(hy_ref[...], wb_ref[2])
    o_ref[...] = x_ref[...] + g1_ref[...] * _dot(m.astype(MXU_DTYPE), wo_ref[...])


def _merge(x, a, attn, hy, gates, mod, mod_row, wb, wo):
    bx, l, d = x.shape
    tm = min(512, l)
    tile = lambda w: pl.BlockSpec((None, tm, w), lambda b, i: (b, i, 0))
    return pl.pallas_call(
        _merge_body,
        grid=(bx, l // tm),
        in_specs=[tile(d), tile(BRANCH_DIM), tile(BRANCH_DIM), tile(BRANCH_DIM), tile(3 * d),
                  _mod_spec(lambda b, i: mod_row(b), 2), _const_spec(wb.shape), _const_spec(wo.shape)],
        out_specs=tile(d),
        out_shape=jax.ShapeDtypeStruct((bx, l, d), F32),
        compiler_params=_cparams(("arbitrary", "arbitrary")),
        name="branch_merge",
    )(x, a, attn, hy, gates, mod, wb, wo)


FFN_TN = 256


def _ffn_body(x_ref, sh_ref, sc_ref, g2_ref, ng_ref, wa_ref, wb_ref, cw_ref, cb_ref, wd_ref, fg_ref,
              o_ref, h_ref, *, final):
    j = pl.program_id(1)

    @pl.when(j == 0)
    def _():
        x = x_ref[...]
        ms = jnp.mean(x * x, axis=-1, keepdims=True)
        y = x * lax.rsqrt(ms + NORM_EPS) * ng_ref[...]
        h_ref[...] = (y * (1.0 + sc_ref[...]) + sh_ref[...]).astype(MXU_DTYPE)
        o_ref[...] = jnp.zeros_like(o_ref)

    h = h_ref[...]
    a = _conv3(_dot(h, wa_ref[...]), cw_ref, cb_ref)
    gated = (_gelu(a) * _dot(h, wb_ref[...])).astype(MXU_DTYPE)
    o_ref[...] += _dot(gated, wd_ref[...])

    @pl.when(j == pl.num_programs(1) - 1)
    def _():
        xn = x_ref[...] + g2_ref[...] * o_ref[...]
        if final:
            ms = jnp.mean(xn * xn, axis=-1, keepdims=True)
            xn = xn * lax.rsqrt(ms + NORM_EPS) * fg_ref[...]
        o_ref[...] = xn


def _ffn(x, mod, mod_row, norm_g, w_up, conv_w, conv_b, w_down, final_g, final):
    bx, l, d = x.shape
    nj = D_FF // FFN_TN
    seq = pl.BlockSpec((None, l, d), lambda b, j: (b, 0, 0))
    return pl.pallas_call(
        functools.partial(_ffn_body, final=final),
        grid=(bx, nj),
        in_specs=[
            seq,
            _mod_spec(lambda b, j: mod_row(b), 3), _mod_spec(lambda b, j: mod_row(b), 4),
            _mod_spec(lambda b, j: mod_row(b), 5),
            _const_spec((1, d)),
            pl.BlockSpec((d, FFN_TN), lambda b, j: (0, j)),
            pl.BlockSpec((d, FFN_TN), lambda b, j: (0, nj + j)),
            pl.BlockSpec((3, FFN_TN), lambda b, j: (0, j)),
            pl.BlockSpec((1, FFN_TN), lambda b, j: (0, j)),
            pl.BlockSpec((FFN_TN, d), lambda b, j: (j, 0)),
            _const_spec((1, d)),
        ],
        out_specs=seq,
        out_shape=jax.ShapeDtypeStruct((bx, l, d), F32),
        scratch_shapes=[pltpu.VMEM((l, d), MXU_DTYPE)],
        compiler_params=_cparams(("arbitrary", "arbitrary")),
        name="conv_ffn",
    )(x, mod, mod, mod, norm_g.reshape(1, d), w_up, w_up, conv_w, conv_b.reshape(1, -1), w_down,
      final_g.reshape(1, d))


def _rope_tables(l):
    half = NA_HEAD_DIM // 2
    quarter = half // 2
    inv_freq = ROPE_THETA ** (-jnp.arange(quarter, dtype=F32) * 2.0 / half)
    pos = jnp.arange(l)
    ang_r = (pos // GRID_W).astype(F32)[:, None] * inv_freq[None, :]
    ang_c = (pos % GRID_W).astype(F32)[:, None] * inv_freq[None, :]
    cos_h = jnp.concatenate([jnp.cos(ang_r), jnp.cos(ang_r), jnp.cos(ang_c), jnp.cos(ang_c)], axis=-1)
    sin_h = jnp.concatenate([-jnp.sin(ang_r), jnp.sin(ang_r), -jnp.sin(ang_c), jnp.sin(ang_c)], axis=-1)
    reps = LANES // NA_HEAD_DIM
    return jnp.tile(cos_h, (1, reps)), jnp.tile(sin_h, (1, reps))


def kernel(x, c, ctx, c_ctx, ada_w, ada_b, norm1_g, norm2_g, w_in, sgu_norm_g, sgu_w, sgu_b, na_rpb, hy_conv_w, hy_conv_b, hy_filt_w1, hy_filt_b1, hy_filt_w2, hy_filt_b2, hy_filt_w3, hy_sin_freq, hy_skip, w_branch, w_out, ffn_w_up, ffn_conv_w, ffn_conv_b, ffn_w_down, final_norm_g):
    b, l, d = x.shape
    lc = ctx.shape[1]
    assert d == D_MODEL and l % ATT_TOK == 0 and l % GRID_W == 0

    n_rows = -(-(b + 1) // 8) * 8
    cc = jnp.concatenate([c, c_ctx[None, :], jnp.zeros((n_rows - b - 1, d), F32)], axis=0)
    mod = _modulation(cc, ada_w, ada_b).reshape(DEPTH * n_rows, 1, N_MOD * d)

    rope_tabs = _rope_tables(l)
    dft_l, fc_l = _dft_consts(l), _filter_consts(l)
    dft_c, fc_c = _dft_consts(lc), _filter_consts(lc)
    patterns, type_of, key_start = _att_plan(l // GRID_W)

    xc = ctx
    for i in range(DEPTH):
        update_ctx = i < DEPTH - 1
        lat_row = lambda bb, i=i: i * n_rows + bb
        ctx_row = lambda bb, i=i: i * n_rows + b
        w_in_i = w_in[i].astype(MXU_DTYPE)
        wb_i, wo_i = w_branch[i].astype(MXU_DTYPE), w_out[i].astype(MXU_DTYPE)
        wup_i, wdn_i = ffn_w_up[i].astype(MXU_DTYPE), ffn_w_down[i].astype(MXU_DTYPE)
        filt_args = (hy_filt_w1[i], hy_filt_b1[i], hy_filt_w2[i], hy_filt_b2[i], hy_filt_w3[i], hy_sin_freq[i])
        bias = _bias_tables(na_rpb[i], patterns)

        if update_ctx:
            uv_c, q_c, k_c, v_c, hy_c, gt_c = _in_proj(xc, mod, ctx_row, norm1_g[i], w_in_i, FULL_SEGS, None,
                                                       "context_in_proj")
        else:
            w_kv = w_in_i[:, OFF_QKV + BRANCH_DIM:OFF_HY]
            k_c, v_c = _in_proj(xc, mod, ctx_row, norm1_g[i], w_kv, KV_SEGS, None, "context_kv_proj")

        uv, q, k, v, hy, gt = _in_proj(x, mod, lat_row, norm1_g[i], w_in_i, FULL_SEGS, rope_tabs, "latent_in_proj")
        attn = _neighborhood_attention(q, k, v, k_c, v_c, bias, type_of, key_start)
        a = _sgu(uv, sgu_norm_g[i], sgu_w[i], sgu_b[i])
        filt = _hyena_filters(l, dft_l, fc_l, *filt_args)
        hyo = _hyena(hy, hy_conv_w[i], hy_conv_b[i], hy_skip[i], filt, dft_l)
        x = _merge(x, a, attn, hyo, gt, mod, lat_row, wb_i, wo_i)
        x = _ffn(x, mod, lat_row, norm2_g[i], wup_i, ffn_conv_w[i], ffn_conv_b[i], wdn_i, final_norm_g,
                 final=not update_ctx)

        if update_ctx:
            attn_c = _context_attention(q_c, k_c, v_c)
            a_c = _sgu(uv_c, sgu_norm_g[i], sgu_w[i], sgu_b[i])
            filt_c = _hyena_filters(lc, dft_c, fc_c, *filt_args)
            hyo_c = _hyena(hy_c, hy_conv_w[i], hy_conv_b[i], hy_skip[i], filt_c, dft_c)
            xc = _merge(xc, a_c, attn_c, hyo_c, gt_c, mod, ctx_row, wb_i, wo_i)
            xc = _ffn(xc, mod, ctx_row, norm2_g[i], wup_i, ffn_conv_w[i], ffn_conv_b[i], wdn_i, final_norm_g,
                      final=False)
    return x
```

```python
import functools
import math

import jax
import jax.numpy as jnp
import numpy as np
from jax import lax
from jax.experimental import pallas as pl
from jax.experimental.pallas import tpu as pltpu

D_MODEL = 1024
DEPTH = 2
GRID_W = 64
NORM_EPS = 1e-6
N_MOD = 6
BRANCH_DIM = D_MODEL // 2
SGU_GROUP_DIM = 128
SGU_GROUPS = BRANCH_DIM // SGU_GROUP_DIM
SGU_CHUNK = 128
NA_HEAD_DIM = 64
NA_HEADS = BRANCH_DIM // NA_HEAD_DIM
NA_WIN_ROWS = 8
NA_WIN_COLS = 16
ROPE_THETA = 10000.0
NEG_INF = -1e30
HY_DIM = BRANCH_DIM
HY_ORDER = 2
HY_EMB = 33
HY_FILTER_HIDDEN = 64
HY_FAST_DECAY_PCT = 0.3
HY_SLOW_DECAY_PCT = 1.5
HY_DECAY_TARGET = 1e-2
D_FF = 128 * ((8 * D_MODEL + 3 * 128 - 1) // (3 * 128))
OFF_SGU = 0
OFF_QKV = OFF_SGU + 2 * BRANCH_DIM
OFF_HY = OFF_QKV + 3 * BRANCH_DIM
OFF_GATE = OFF_HY + (HY_ORDER + 1) * HY_DIM
IN_COLS = OFF_GATE + 3 * D_MODEL

LANES = 128
MXU_WIDTH = 256
VMEM_LIMIT_BYTES = 56 * 1024 * 1024

MXU_DTYPE = jnp.bfloat16
ACT_DTYPE = jnp.bfloat16
F32 = jnp.float32
HIGHEST = lax.Precision.HIGHEST

Q_ROWS_PER_BLOCK = 4
K_ROWS_PER_BLOCK = 12
ATT_TOK = Q_ROWS_PER_BLOCK * GRID_W


def _cparams(sem):
    return pltpu.CompilerParams(dimension_semantics=sem, vmem_limit_bytes=VMEM_LIMIT_BYTES)


def _const_spec(shape):
    nd = len(shape)
    return pl.BlockSpec(shape, lambda *_: (0,) * nd, pipeline_mode=pl.Buffered(1))


def _dot(a, b, **kw):
    return jnp.dot(a, b, preferred_element_type=F32, **kw)


def _dot_nt(a, b):
    return lax.dot_general(a, b, (((1,), (1,)), ((), ())), preferred_element_type=F32)


def _gelu(x):
    return 0.5 * x * (1.0 + lax.erf(x * (1.0 / math.sqrt(2.0))))


def _mod_body(c_ref, w_ref, b_ref, o_ref):
    c = c_ref[...]
    s = c * jax.nn.sigmoid(c)
    o_ref[...] = _dot(s, w_ref[...], precision=HIGHEST) + b_ref[...]


def _modulation(cc, ada_w, ada_b):
    depth, d, n = ada_w.shape
    r = cc.shape[0]
    tn = 1024
    return pl.pallas_call(
        _mod_body,
        grid=(depth, n // tn),
        in_specs=[
            pl.BlockSpec((r, d), lambda i, j: (0, 0)),
            pl.BlockSpec((None, d, tn), lambda i, j: (i, 0, j)),
            pl.BlockSpec((None, 1, tn), lambda i, j: (i, 0, j)),
        ],
        out_specs=pl.BlockSpec((None, r, tn), lambda i, j: (i, 0, j)),
        out_shape=jax.ShapeDtypeStruct((depth, r, n), F32),
        compiler_params=_cparams(("arbitrary", "arbitrary")),
        name="adaln_modulation",
    )(cc, ada_w, ada_b.reshape(depth, 1, n))


def _mod_spec(row_fn, chunk):
    return pl.BlockSpec((None, 1, D_MODEL), lambda *g: (row_fn(*g), 0, chunk))


def _rope(acc, cos, sin):
    lane = lax.broadcasted_iota(jnp.int32, cos.shape, 1)
    first_half = (lane % 32) < 16
    outs = []
    for c0 in range(0, acc.shape[1], LANES):
        xc = acc[:, c0:c0 + LANES]
        partner = jnp.where(first_half, pltpu.roll(xc, LANES - 16, 1), pltpu.roll(xc, 16, 1))
        outs.append(xc * cos + partner * sin)
    return outs


def _in_proj_body(*refs, segs, rope, n_out):
    x_ref, sh_ref, sc_ref, g_ref, w_ref = refs[:5]
    pos = 5
    if rope:
        cos_ref, sin_ref = refs[5:7]
        pos = 7
    out_refs = refs[pos:pos + n_out]
    x = x_ref[...]
    ms = jnp.mean(x * x, axis=-1, keepdims=True)
    y = x * lax.rsqrt(ms + NORM_EPS) * g_ref[...]
    h = (y * (1.0 + sc_ref[...]) + sh_ref[...]).astype(MXU_DTYPE)
    for (c0, width, kind, oi) in segs:
        step = min(width, 512)
        for cc in range(0, width, step):
            acc = _dot(h, w_ref[:, c0 + cc:c0 + cc + step])
            if kind == "gelu":
                acc = _gelu(acc)
            elif kind == "sigmoid":
                acc = jax.nn.sigmoid(acc)
            if kind == "rope" and rope:
                for k, piece in enumerate(_rope(acc, cos_ref[...], sin_ref[...])):
                    out_refs[oi][:, cc + k * LANES:cc + (k + 1) * LANES] = piece.astype(ACT_DTYPE)
            else:
                out_refs[oi][:, cc:cc + step] = acc.astype(ACT_DTYPE)


def _in_proj(x, mod, mod_row, norm_g, w, segs, rope_tabs, name):
    bx, l, d = x.shape
    tm = min(512, l)
    n_out = max(s[3] for s in segs) + 1
    widths = [sum(s[1] for s in segs if s[3] == oi) for oi in range(n_out)]
    rope = rope_tabs is not None
    in_specs = [
        pl.BlockSpec((None, tm, d), lambda b, i: (b, i, 0)),
        _mod_spec(lambda b, i: mod_row(b), 0),
        _mod_spec(lambda b, i: mod_row(b), 1),
        _const_spec((1, d)),
        _const_spec(w.shape),
    ]
    args = [x, mod, mod, norm_g.reshape(1, d), w]
    if rope:
        in_specs += [pl.BlockSpec((tm, LANES), lambda b, i: (i, 0))] * 2
        args += list(rope_tabs)
    return pl.pallas_call(
        functools.partial(_in_proj_body, segs=segs, rope=rope, n_out=n_out),
        grid=(bx, l // tm),
        in_specs=in_specs,
        out_specs=[pl.BlockSpec((None, tm, wd), lambda b, i: (b, i, 0)) for wd in widths],
        out_shape=[jax.ShapeDtypeStruct((bx, l, wd), ACT_DTYPE) for wd in widths],
        compiler_params=_cparams(("arbitrary", "arbitrary")),
        name=name,
    )(*args)


FULL_SEGS = (
    (OFF_SGU, 2 * BRANCH_DIM, "gelu", 0),
    (OFF_QKV, BRANCH_DIM, "rope", 1),
    (OFF_QKV + BRANCH_DIM, BRANCH_DIM, "rope", 2),
    (OFF_QKV + 2 * BRANCH_DIM, BRANCH_DIM, "plain", 3),
    (OFF_HY, 3 * HY_DIM, "plain", 4),
    (OFF_GATE, 3 * D_MODEL, "sigmoid", 5),
)
KV_SEGS = ((0, BRANCH_DIM, "plain", 0), (BRANCH_DIM, BRANCH_DIM, "plain", 1))


def _sgu_body(uv_ref, g_ref, w_ref, b_ref, o_ref):
    ts = uv_ref.shape[0]
    v = uv_ref[:, BRANCH_DIM:].astype(F32)
    ms = jnp.mean(v * v, axis=-1, keepdims=True)
    vb = (v * lax.rsqrt(ms + NORM_EPS) * g_ref[...]).astype(MXU_DTYPE)
    for n in range(ts // SGU_CHUNK):
        rs = slice(n * SGU_CHUNK, (n + 1) * SGU_CHUNK)
        for g in range(SGU_GROUPS):
            cs = slice(g * SGU_GROUP_DIM, (g + 1) * SGU_GROUP_DIM)
            mixed = _dot(w_ref[g], vb[rs, cs]) + b_ref[g]
            o_ref[rs, cs] = (uv_ref[rs, cs].astype(F32) * mixed).astype(ACT_DTYPE)


def _sgu(uv, norm_g, w_s, b_s):
    bx, l, _ = uv.shape
    ts = min(512, l)
    return pl.pallas_call(
        _sgu_body,
        grid=(bx, l // ts),
        in_specs=[
            pl.BlockSpec((None, ts, 2 * BRANCH_DIM), lambda b, i: (b, i, 0)),
            _const_spec((1, BRANCH_DIM)),
            _const_spec(w_s.shape),
            _const_spec((SGU_GROUPS, SGU_CHUNK, 1)),
        ],
        out_specs=pl.BlockSpec((None, ts, BRANCH_DIM), lambda b, i: (b, i, 0)),
        out_shape=jax.ShapeDtypeStruct((bx, l, BRANCH_DIM), ACT_DTYPE),
        compiler_params=_cparams(("arbitrary", "arbitrary")),
        name="spatial_gating",
    )(uv, norm_g.reshape(1, BRANCH_DIM), w_s.astype(MXU_DTYPE), b_s.reshape(SGU_GROUPS, SGU_CHUNK, 1))


def _att_plan(rows):
    kr = min(NA_WIN_ROWS, rows)
    assert rows % Q_ROWS_PER_BLOCK == 0 and rows >= K_ROWS_PER_BLOCK and kr == NA_WIN_ROWS
    patterns, type_of, key_start = [], [], []
    for rb in range(0, rows, Q_ROWS_PER_BLOCK):
        ks = min(max(rb - NA_WIN_ROWS // 2, 0), rows - K_ROWS_PER_BLOCK)
        assert ks % Q_ROWS_PER_BLOCK == 0
        pat = []
        for qi in range(Q_ROWS_PER_BLOCK):
            rq = rb + qi
            r0 = min(max(rq - kr // 2, 0), rows - kr)
            for j in range(K_ROWS_PER_BLOCK):
                rk = ks + j
                pat.append(rk - rq + NA_WIN_ROWS - 1 if r0 <= rk < r0 + kr else None)
        pat = tuple(pat)
        if pat not in patterns:
            patterns.append(pat)
        type_of.append(patterns.index(pat))
        key_start.append(ks // Q_ROWS_PER_BLOCK)
    return patterns, type_of, key_start


def _bias_body(rpb_ref, o_ref, *, patterns):
    n_dr, n_dc = 2 * NA_WIN_ROWS - 1, 2 * NA_WIN_COLS - 1
    h = pl.program_id(0)
    qc = lax.broadcasted_iota(jnp.int32, (GRID_W, GRID_W), 0)
    kc = lax.broadcasted_iota(jnp.int32, (GRID_W, GRID_W), 1)
    cstart = jnp.clip(qc - NA_WIN_COLS // 2, 0, GRID_W - NA_WIN_COLS)
    col_ok = (kc >= cstart) & (kc < cstart + NA_WIN_COLS)
    dc = jnp.clip(kc - qc + NA_WIN_COLS - 1, 0, n_dc - 1)
    neg = jnp.full((GRID_W, GRID_W), NEG_INF, F32)
    tiles = []
    for dr in range(n_dr):
        t = jnp.zeros((GRID_W, GRID_W), F32)
        for d in range(n_dc):
            t = jnp.where(dc == d, rpb_ref[(h * n_dr + dr) * n_dc + d], t)
        tiles.append(jnp.where(col_ok, t * LOG2E, neg))
    for ty, pat in enumerate(patterns):
        for qi in range(Q_ROWS_PER_BLOCK):
            for j in range(K_ROWS_PER_BLOCK):
                dr = pat[qi * K_ROWS_PER_BLOCK + j]
                o_ref[ty, qi * GRID_W:(qi + 1) * GRID_W, j * GRID_W:(j + 1) * GRID_W] = neg if dr is None else tiles[dr]


def _bias_tables(rpb, patterns):
    nt = len(patterns)
    return pl.pallas_call(
        functools.partial(_bias_body, patterns=patterns),
        grid=(NA_HEADS,),
        in_specs=[pl.BlockSpec(memory_space=pltpu.SMEM)],
        out_specs=pl.BlockSpec((nt, None, ATT_TOK, K_ROWS_PER_BLOCK * GRID_W), lambda h: (0, h, 0, 0)),
        out_shape=jax.ShapeDtypeStruct((nt, NA_HEADS, ATT_TOK, K_ROWS_PER_BLOCK * GRID_W), F32),
        compiler_params=_cparams(("arbitrary",)),
        name="attention_bias_tables",
    )(rpb.reshape(-1))


def _head_mask(shape, hh):
    lane = lax.broadcasted_iota(jnp.int32, shape, 1)
    return (lane >= hh * NA_HEAD_DIM) & (lane < (hh + 1) * NA_HEAD_DIM)


LOG2E = math.log2(math.e)
QK_SCALE = NA_HEAD_DIM ** -0.5 * LOG2E


def _pair_attention(qp, keys, biases, values):
    accs = []
    for hh in range(2):
        qm = jnp.where(_head_mask(qp.shape, hh), qp, jnp.zeros_like(qp))
        scores = []
        for kj, bj in zip(keys, biases[hh]):
            s = _dot_nt(qm, kj)
            scores.append(s if bj is None else s + bj)
        m = functools.reduce(jnp.maximum, [jnp.max(s, axis=-1, keepdims=True) for s in scores])
        acc = None
        for s, v in zip(scores, values):
            e = jnp.exp2(s - m).astype(MXU_DTYPE)
            part = _dot(e, jnp.where(_head_mask(v.shape, hh), v, jnp.ones_like(v)))
            acc = part if acc is None else acc + part
        accs.append(acc)
    first = _head_mask(accs[0].shape, 0)
    num = jnp.where(first, accs[0], accs[1])
    den = jnp.where(first, pltpu.roll(accs[0], NA_HEAD_DIM, 1), pltpu.roll(accs[1], NA_HEAD_DIM, 1))
    return num / den


def _attn_body(ty_ref, kb_ref, q_ref, k0, k1, k2, v0, v1, v2, kc_ref, vc_ref, bias_ref, o_ref):
    del ty_ref, kb_ref
    for p in range(NA_HEADS // 2):
        cs = slice(p * LANES, (p + 1) * LANES)
        qp = q_ref[:, cs] * QK_SCALE
        keys = [k0[:, cs], k1[:, cs], k2[:, cs], kc_ref[:, cs]]
        values = [v0[:, cs], v1[:, cs], v2[:, cs], vc_ref[:, cs]]
        biases = [[bias_ref[2 * p + hh, :, j * ATT_TOK:(j + 1) * ATT_TOK] for j in range(3)] + [None]
                  for hh in range(2)]
        o_ref[:, cs] = _pair_attention(qp, keys, biases, values).astype(ACT_DTYPE)


def _neighborhood_attention(q, k, v, kc, vc, bias, type_of, key_start):
    b, l, _ = q.shape
    lc = kc.shape[1]
    n_blk = l // ATT_TOK
    tok = lambda off: pl.BlockSpec((None, ATT_TOK, BRANCH_DIM), lambda r, bb, ty, kb: (bb, kb[r] + off, 0))
    ctx = pl.BlockSpec((None, lc, BRANCH_DIM), lambda r, bb, ty, kb: (bb, 0, 0))
    grid_spec = pltpu.PrefetchScalarGridSpec(
        num_scalar_prefetch=2,
        grid=(n_blk, b),
        in_specs=[
            pl.BlockSpec((None, ATT_TOK, BRANCH_DIM), lambda r, bb, ty, kb: (bb, r, 0)),
            tok(0), tok(1), tok(2), tok(0), tok(1), tok(2), ctx, ctx,
            pl.BlockSpec((None, NA_HEADS, ATT_TOK, K_ROWS_PER_BLOCK * GRID_W), lambda r, bb, ty, kb: (ty[r], 0, 0, 0)),
        ],
        out_specs=pl.BlockSpec((None, ATT_TOK, BRANCH_DIM), lambda r, bb, ty, kb: (bb, r, 0)),
    )
    return pl.pallas_call(
        _attn_body,
        grid_spec=grid_spec,
        out_shape=jax.ShapeDtypeStruct((b, l, BRANCH_DIM), ACT_DTYPE),
        compiler_params=_cparams(("arbitrary", "arbitrary")),
        name="neighborhood_attention",
    )(jnp.asarray(type_of, jnp.int32), jnp.asarray(key_start, jnp.int32), q, k, k, k, v, v, v, kc, vc, bias)


def _ctx_attn_body(q_ref, k_ref, v_ref, o_ref):
    for p in range(NA_HEADS // 2):
        cs = slice(p * LANES, (p + 1) * LANES)
        qp = q_ref[:, cs] * QK_SCALE
        o_ref[:, cs] = _pair_attention(qp, [k_ref[:, cs]], [[None], [None]], [v_ref[:, cs]]).astype(ACT_DTYPE)


def _context_attention(q, k, v):
    b, lc, _ = q.shape
    spec = pl.BlockSpec((None, lc, BRANCH_DIM), lambda bb: (bb, 0, 0))
    return pl.pallas_call(
        _ctx_attn_body,
        grid=(b,),
        in_specs=[spec, spec, spec],
        out_specs=spec,
        out_shape=jax.ShapeDtypeStruct((b, lc, BRANCH_DIM), ACT_DTYPE),
        compiler_params=_cparams(("arbitrary",)),
        name="context_attention",
    )(q, k, v)


HY_CW = 256


def _dft_consts(l):
    n = 2 * l
    idx = jnp.arange(l, dtype=jnp.int32)
    ang = ((idx[:, None] * idx[None, :]) % n).astype(F32) * (2.0 * math.pi / n)
    cs = jnp.concatenate([jnp.cos(ang), -jnp.sin(ang)], axis=1).astype(MXU_DTYPE)
    sign = jnp.where(idx % 2 == 0, 1.0, -1.0).astype(F32).reshape(l, 1)
    wf = jnp.where(idx == 0, 1.0 / n, 2.0 / n).astype(F32).reshape(l, 1)
    return cs, sign, wf


def _filter_consts(l):
    t = jnp.linspace(0.0, 1.0, l, dtype=F32)[:, None]
    n_bands = (HY_EMB - 1) // 2
    bands = jnp.linspace(1e-4, n_bands - 1, n_bands, dtype=F32)[None, :]
    ang = (2.0 * math.pi) * jnp.arange(l, dtype=F32)[:, None] / l * bands
    z = jnp.concatenate([t, jnp.cos(ang), -jnp.sin(ang)], axis=-1)
    z = jnp.pad(z, ((0, 0), (0, LANES - HY_EMB)))
    max_decay = math.log(HY_DECAY_TARGET) / HY_FAST_DECAY_PCT
    min_decay = math.log(HY_DECAY_TARGET) / HY_SLOW_DECAY_PCT
    deltas = jnp.abs(jnp.linspace(min_decay, max_decay, HY_DIM, dtype=F32)).reshape(1, HY_DIM)
    return z, t, deltas


def _filter_body(z_ref, w1_ref, b1_ref, w2_ref, b2_ref, fr_ref, w3a_ref, w3b_ref, t_ref, dl_ref,
                 sg_ref, wf_ref, cs_ref, kc_ref, ks_ref, kn_ref):
    l = z_ref.shape[0]
    fr = fr_ref[...]
    h = jnp.sin(fr * (_dot(z_ref[...], w1_ref[...], precision=HIGHEST) + b1_ref[...]))
    h = jnp.sin(fr * (_dot(h, w2_ref[...], precision=HIGHEST) + b2_ref[...]))
    decay = jnp.exp(-t_ref[...] * dl_ref[...])
    h0 = _dot(h, w3a_ref[...], precision=HIGHEST) * decay
    h1 = _dot(h, w3b_ref[...], precision=HIGHEST) * decay
    row = lax.broadcasted_iota(jnp.int32, h1.shape, 0)
    h1 = jnp.where(row == 0, 0.0, h1)
    den = jnp.sum(jnp.abs(h0), axis=0, keepdims=True) + jnp.sum(jnp.abs(h1), axis=0, keepdims=True)
    hs = (h0 + h1) / den
    hd = (h0 - h1) / den
    kn_ref[...] = jnp.sum(sg_ref[...] * hs, axis=0, keepdims=True) * (1.0 / (2 * l))
    kc_ref[...] = (wf_ref[...] * _dot(cs_ref[:, :l], hs.astype(MXU_DTYPE))).astype(kc_ref.dtype)
    ks_ref[...] = (wf_ref[...] * _dot(cs_ref[:, l:], hd.astype(MXU_DTYPE))).astype(ks_ref.dtype)


def _hyena_filters(l, dft, fconsts, w1, b1, w2, b2, w3, freq):
    cs, sign, wf = dft
    z, t, deltas = fconsts
    cw = HY_CW
    nb = HY_DIM // cw
    w1p = jnp.pad(w1, ((0, LANES - HY_EMB), (0, 0)))
    hid = HY_FILTER_HIDDEN
    small = lambda shape: pl.BlockSpec(shape, lambda n, cb: (0,) * len(shape))
    spec_out = pl.BlockSpec((None, l, cw), lambda n, cb: (n, 0, cb))
    return pl.pallas_call(
        _filter_body,
        grid=(HY_ORDER, nb),
        in_specs=[
            small((l, LANES)), small((LANES, hid)), small((1, hid)), small((hid, hid)), small((1, hid)), small((1, hid)),
            pl.BlockSpec((hid, cw), lambda n, cb: (0, n * 2 * nb + cb)),
            pl.BlockSpec((hid, cw), lambda n, cb: (0, n * 2 * nb + nb + cb)),
            small((l, 1)),
            pl.BlockSpec((1, cw), lambda n, cb: (0, cb)),
            small((l, 1)), small((l, 1)),
            _const_spec((l, 2 * l)),
        ],
        out_specs=[spec_out, spec_out, pl.BlockSpec((None, 1, cw), lambda n, cb: (n, 0, cb))],
        out_shape=[jax.ShapeDtypeStruct((HY_ORDER, l, HY_DIM), ACT_DTYPE)] * 2
        + [jax.ShapeDtypeStruct((HY_ORDER, 1, HY_DIM), F32)],
        compiler_params=_cparams(("arbitrary", "arbitrary")),
        name="hyena_filter_spectrum",
    )(z, w1p, b1.reshape(1, hid), w2, b2.reshape(1, hid), freq.reshape(1, hid), w3, w3, t, deltas, sign, wf, cs)


def _conv3(x, w_ref, b_ref, seq_len):
    n = x.shape[0]
    pos = lax.broadcasted_iota(jnp.int32, x.shape, 0) % seq_len
    prev = jnp.where(pos == 0, 0.0, pltpu.roll(x, 1, 0))
    nxt = jnp.where(pos == seq_len - 1, 0.0, pltpu.roll(x, n - 1, 0))
    return prev * w_ref[0:1, :] + x * w_ref[1:2, :] + nxt * w_ref[2:3, :] + b_ref[...]


HY_TM = 512
CONV_HALO = 16


def _conv3_rows(p_ref, w_ref, b_ref, r0, rows):
    l = p_ref.shape[0]
    lo, hi = max(r0 - CONV_HALO, 0), min(r0 + rows + CONV_HALO, l)
    x = p_ref[lo:hi, :].astype(F32)
    n = hi - lo
    prev, nxt = pltpu.roll(x, 1, 0), pltpu.roll(x, n - 1, 0)
    row = lax.broadcasted_iota(jnp.int32, x.shape, 0)
    if lo == 0:
        prev = jnp.where(row == 0, 0.0, prev)
    if hi == l:
        nxt = jnp.where(row == n - 1, 0.0, nxt)
    y = prev * w_ref[0:1, :] + x * w_ref[1:2, :] + nxt * w_ref[2:3, :] + b_ref[...]
    return y[r0 - lo:r0 - lo + rows]


def _hyena_body(pv_ref, p1_ref, p2_ref, wv_ref, w1_ref, w2_ref, bv_ref, b1_ref, b2_ref, skip_ref,
                kc_ref, ks_ref, kn_ref, sg_ref, cs_ref, o_ref, zf_ref, zb_ref, y_ref):
    l = pv_ref.shape[0]
    tm = min(HY_TM, l)
    chunks = [slice(r0, r0 + tm) for r0 in range(0, l, tm)]
    zn = jnp.zeros((1, pv_ref.shape[1]), F32)
    for rs in chunks:
        z = _conv3_rows(pv_ref, wv_ref, bv_ref, rs.start, tm)
        zf_ref[rs] = z
        zb_ref[rs] = z.astype(MXU_DTYPE)
        zn = zn + jnp.sum(sg_ref[rs] * z, axis=0, keepdims=True)
    gates = ((p1_ref, w1_ref, b1_ref), (p2_ref, w2_ref, b2_ref))
    for n, (p_ref, w_ref, b_ref) in enumerate(gates):
        for rs in chunks:
            zc = _dot(cs_ref[rs, :l], zb_ref[...])
            zs = _dot(cs_ref[rs, l:], zb_ref[...])
            kc, ks = kc_ref[n, rs].astype(F32), ks_ref[n, rs].astype(F32)
            y_ref[rs] = (zc * kc - zs * ks).astype(MXU_DTYPE)
            y_ref[l + rs.start:l + rs.stop] = (zc * ks + zs * kc).astype(MXU_DTYPE)
        nyq = zn * kn_ref[n]
        zn = jnp.zeros_like(zn)
        for rs in chunks:
            y = _dot(cs_ref[rs, :], y_ref[...]) + sg_ref[rs] * nyq
            gate = _conv3_rows(p_ref, w_ref, b_ref, rs.start, tm)
            z = gate * (y + zf_ref[rs] * skip_ref[n:n + 1, :])
            if n + 1 < HY_ORDER:
                zf_ref[rs] = z
                zb_ref[rs] = z.astype(MXU_DTYPE)
                zn = zn + jnp.sum(sg_ref[rs] * z, axis=0, keepdims=True)
            else:
                o_ref[rs] = z.astype(ACT_DTYPE)


def _hyena(p, conv_w, conv_b, skip, filters, dft):
    bx, l, _ = p.shape
    cs, sign, _ = dft
    kcs, kss, kns = filters
    cw = HY_CW
    nb = HY_DIM // cw
    pspec = lambda part: pl.BlockSpec((None, l, cw), lambda cb, b: (b, 0, part * nb + cb))
    wspec = lambda part: pl.BlockSpec((3, cw), lambda cb, b: (0, part * nb + cb))
    bspec = lambda part: pl.BlockSpec((1, cw), lambda cb, b: (0, part * nb + cb))
    fspec = pl.BlockSpec((HY_ORDER, l, cw), lambda cb, b: (0, 0, cb), pipeline_mode=pl.Buffered(1))
    return pl.pallas_call(
        _hyena_body,
        grid=(nb, bx),
        in_specs=[
            pspec(0), pspec(1), pspec(2), wspec(0), wspec(1), wspec(2), bspec(0), bspec(1), bspec(2),
            pl.BlockSpec((HY_ORDER, cw), lambda cb, b: (0, cb)),
            fspec, fspec,
            pl.BlockSpec((HY_ORDER, 1, cw), lambda cb, b: (0, 0, cb)),
            _const_spec((l, 1)), _const_spec((l, 2 * l)),
        ],
        out_specs=pl.BlockSpec((None, l, cw), lambda cb, b: (b, 0, cb)),
        out_shape=jax.ShapeDtypeStruct((bx, l, HY_DIM), ACT_DTYPE),
        scratch_shapes=[pltpu.VMEM((l, cw), F32), pltpu.VMEM((l, cw), MXU_DTYPE), pltpu.VMEM((2 * l, cw), MXU_DTYPE)],
        compiler_params=_cparams(("arbitrary", "arbitrary")),
        name="hyena_long_conv",
    )(p, p, p, conv_w, conv_w, conv_w, conv_b.reshape(1, -1), conv_b.reshape(1, -1), conv_b.reshape(1, -1),
      skip, kcs, kss, kns, sign, cs)


def _merge_body(x_ref, a_ref, at_ref, hy_ref, gt_ref, g1_ref, wb_ref, wo_ref, o_ref):
    d = D_MODEL
    m = gt_ref[:, 0:d].astype(F32) * _dot(a_ref[...], wb_ref[0])
    m = m + gt_ref[:, d:2 * d].astype(F32) * _dot(at_ref[...], wb_ref[1])
    m = m + gt_ref[:, 2 * d:3 * d].astype(F32) * _dot(hy_ref[...], wb_ref[2])
    o_ref[...] = x_ref[...] + g1_ref[...] * _dot(m.astype(MXU_DTYPE), wo_ref[...])


def _merge(x, a, attn, hy, gates, mod, mod_row, wb, wo):
    bx, l, d = x.shape
    tm = min(512, l)
    tile = lambda w: pl.BlockSpec((None, tm, w), lambda b, i: (b, i, 0))
    return pl.pallas_call(
        _merge_body,
        grid=(bx, l // tm),
        in_specs=[tile(d), tile(BRANCH_DIM), tile(BRANCH_DIM), tile(BRANCH_DIM), tile(3 * d),
                  _mod_spec(lambda b, i: mod_row(b), 2), _const_spec(wb.shape), _const_spec(wo.shape)],
        out_specs=tile(d),
        out_shape=jax.ShapeDtypeStruct((bx, l, d), F32),
        compiler_params=_cparams(("arbitrary", "arbitrary")),
        name="branch_merge",
    )(x, a, attn, hy, gates, mod, wb, wo)


FFN_TN = 256


def _ffn_body(x_ref, sh_ref, sc_ref, g2_ref, ng_ref, wa_ref, wb_ref, cw_ref, cb_ref, wd_ref, fg_ref,
              o_ref, h_ref, *, final, seq_len):
    j = pl.program_id(1)

    @pl.when(j == 0)
    def _():
        x = x_ref[...]
        ms = jnp.mean(x * x, axis=-1, keepdims=True)
        y = x * lax.rsqrt(ms + NORM_EPS) * ng_ref[...]
        h_ref[...] = (y * (1.0 + sc_ref[...]) + sh_ref[...]).astype(MXU_DTYPE)
        o_ref[...] = jnp.zeros_like(o_ref)

    h = h_ref[...]
    a = _conv3(_dot(h, wa_ref[...]), cw_ref, cb_ref, seq_len)
    gated = (_gelu(a) * _dot(h, wb_ref[...])).astype(MXU_DTYPE)
    o_ref[...] += _dot(gated, wd_ref[...])

    @pl.when(j == pl.num_programs(1) - 1)
    def _():
        xn = x_ref[...] + g2_ref[...] * o_ref[...]
        if final:
            ms = jnp.mean(xn * xn, axis=-1, keepdims=True)
            xn = xn * lax.rsqrt(ms + NORM_EPS) * fg_ref[...]
        o_ref[...] = xn


def _ffn(x, mod, mod_row, norm_g, w_up, conv_w, conv_b, w_down, final_g, final, seq_len):
    bx, l, d = x.shape
    nj = D_FF // FFN_TN
    seq = pl.BlockSpec((None, l, d), lambda b, j: (b, 0, 0))
    return pl.pallas_call(
        functools.partial(_ffn_body, final=final, seq_len=seq_len),
        grid=(bx, nj),
        in_specs=[
            seq,
            _mod_spec(lambda b, j: mod_row(b), 3), _mod_spec(lambda b, j: mod_row(b), 4),
            _mod_spec(lambda b, j: mod_row(b), 5),
            _const_spec((1, d)),
            pl.BlockSpec((d, FFN_TN), lambda b, j: (0, j)),
            pl.BlockSpec((d, FFN_TN), lambda b, j: (0, nj + j)),
            pl.BlockSpec((3, FFN_TN), lambda b, j: (0, j)),
            pl.BlockSpec((1, FFN_TN), lambda b, j: (0, j)),
            pl.BlockSpec((FFN_TN, d), lambda b, j: (j, 0)),
            _const_spec((1, d)),
        ],
        out_specs=seq,
        out_shape=jax.ShapeDtypeStruct((bx, l, d), F32),
        scratch_shapes=[pltpu.VMEM((l, d), MXU_DTYPE)],
        compiler_params=_cparams(("arbitrary", "arbitrary")),
        name="conv_ffn",
    )(x, mod, mod, mod, norm_g.reshape(1, d), w_up, w_up, conv_w, conv_b.reshape(1, -1), w_down,
      final_g.reshape(1, d))


def _rope_tables(l):
    half = NA_HEAD_DIM // 2
    quarter = half // 2
    inv_freq = ROPE_THETA ** (-jnp.arange(quarter, dtype=F32) * 2.0 / half)
    pos = jnp.arange(l)
    ang_r = (pos // GRID_W).astype(F32)[:, None] * inv_freq[None, :]
    ang_c = (pos % GRID_W).astype(F32)[:, None] * inv_freq[None, :]
    cos_h = jnp.concatenate([jnp.cos(ang_r), jnp.cos(ang_r), jnp.cos(ang_c), jnp.cos(ang_c)], axis=-1)
    sin_h = jnp.concatenate([-jnp.sin(ang_r), jnp.sin(ang_r), -jnp.sin(ang_c), jnp.sin(ang_c)], axis=-1)
    reps = LANES // NA_HEAD_DIM
    return jnp.tile(cos_h, (1, reps)), jnp.tile(sin_h, (1, reps))


def kernel(x, c, ctx, c_ctx, ada_w, ada_b, norm1_g, norm2_g, w_in, sgu_norm_g, sgu_w, sgu_b, na_rpb, hy_conv_w, hy_conv_b, hy_filt_w1, hy_filt_b1, hy_filt_w2, hy_filt_b2, hy_filt_w3, hy_sin_freq, hy_skip, w_branch, w_out, ffn_w_up, ffn_conv_w, ffn_conv_b, ffn_w_down, final_norm_g):
    b, l, d = x.shape
    lc = ctx.shape[1]
    assert d == D_MODEL and l % ATT_TOK == 0 and l % GRID_W == 0

    n_rows = -(-(b + 1) // 8) * 8
    cc = jnp.concatenate([c, c_ctx[None, :], jnp.zeros((n_rows - b - 1, d), F32)], axis=0)
    mod = _modulation(cc, ada_w, ada_b).reshape(DEPTH * n_rows, 1, N_MOD * d)

    rope_tabs = _rope_tables(l)
    dft_l, fc_l = _dft_consts(l), _filter_consts(l)
    dft_c, fc_c = _dft_consts(lc), _filter_consts(lc)
    patterns, type_of, key_start = _att_plan(l // GRID_W)

    xc = ctx
    for i in range(DEPTH):
        update_ctx = i < DEPTH - 1
        lat_row = lambda bb, i=i: i * n_rows + bb
        ctx_row = lambda bb, i=i: i * n_rows + b
        w_in_i = w_in[i].astype(MXU_DTYPE)
        wb_i, wo_i = w_branch[i].astype(MXU_DTYPE), w_out[i].astype(MXU_DTYPE)
        wup_i, wdn_i = ffn_w_up[i].astype(MXU_DTYPE), ffn_w_down[i].astype(MXU_DTYPE)
        filt_args = (hy_filt_w1[i], hy_filt_b1[i], hy_filt_w2[i], hy_filt_b2[i], hy_filt_w3[i], hy_sin_freq[i])
        bias = _bias_tables(na_rpb[i], patterns)

        if update_ctx:
            uv_c, q_c, k_c, v_c, hy_c, gt_c = _in_proj(xc, mod, ctx_row, norm1_g[i], w_in_i, FULL_SEGS, None,
                                                       "context_in_proj")
        else:
            w_kv = w_in_i[:, OFF_QKV + BRANCH_DIM:OFF_HY]
            k_c, v_c = _in_proj(xc, mod, ctx_row, norm1_g[i], w_kv, KV_SEGS, None, "context_kv_proj")

        uv, q, k, v, hy, gt = _in_proj(x, mod, lat_row, norm1_g[i], w_in_i, FULL_SEGS, rope_tabs, "latent_in_proj")
        attn = _neighborhood_attention(q, k, v, k_c, v_c, bias, type_of, key_start)
        a = _sgu(uv, sgu_norm_g[i], sgu_w[i], sgu_b[i])
        filt = _hyena_filters(l, dft_l, fc_l, *filt_args)
        hyo = _hyena(hy, hy_conv_w[i], hy_conv_b[i], hy_skip[i], filt, dft_l)
        x = _merge(x, a, attn, hyo, gt, mod, lat_row, wb_i, wo_i)
        x = _ffn(x, mod, lat_row, norm2_g[i], wup_i, ffn_conv_w[i], ffn_conv_b[i], wdn_i, final_norm_g,
                 final=not update_ctx, seq_len=l)

        if update_ctx:
            attn_c = _context_attention(q_c, k_c, v_c)
            a_c = _sgu(uv_c, sgu_norm_g[i], sgu_w[i], sgu_b[i])
            filt_c = _hyena_filters(lc, dft_c, fc_c, *filt_args)
            hyo_c = _hyena(hy_c, hy_conv_w[i], hy_conv_b[i], hy_skip[i], filt_c, dft_c)
            xc = _merge(xc, a_c, attn_c, hyo_c, gt_c, mod, ctx_row, wb_i, wo_i)
            grp = math.gcd(b, max(1, l // lc))
            xc = _ffn(xc.reshape(b // grp, grp * lc, d), mod, ctx_row, norm2_g[i], wup_i, ffn_conv_w[i],
                      ffn_conv_b[i], wdn_i, final_norm_g, final=False, seq_len=lc).reshape(b, lc, d)
    return x
```

```python
import functools
import math

import jax
import jax.numpy as jnp
import numpy as np
from jax import lax
from jax.experimental import pallas as pl
from jax.experimental.pallas import tpu as pltpu

D_MODEL = 1024
DEPTH = 2
GRID_W = 64
NORM_EPS = 1e-6
N_MOD = 6
BRANCH_DIM = D_MODEL // 2
SGU_GROUP_DIM = 128
SGU_GROUPS = BRANCH_DIM // SGU_GROUP_DIM
SGU_CHUNK = 128
NA_HEAD_DIM = 64
NA_HEADS = BRANCH_DIM // NA_HEAD_DIM
NA_WIN_ROWS = 8
NA_WIN_COLS = 16
ROPE_THETA = 10000.0
NEG_INF = -1e30
HY_DIM = BRANCH_DIM
HY_ORDER = 2
HY_EMB = 33
HY_FILTER_HIDDEN = 64
HY_FAST_DECAY_PCT = 0.3
HY_SLOW_DECAY_PCT = 1.5
HY_DECAY_TARGET = 1e-2
D_FF = 128 * ((8 * D_MODEL + 3 * 128 - 1) // (3 * 128))
OFF_SGU = 0
OFF_QKV = OFF_SGU + 2 * BRANCH_DIM
OFF_HY = OFF_QKV + 3 * BRANCH_DIM
OFF_GATE = OFF_HY + (HY_ORDER + 1) * HY_DIM
IN_COLS = OFF_GATE + 3 * D_MODEL

LANES = 128
MXU_WIDTH = 256
VMEM_LIMIT_BYTES = 56 * 1024 * 1024

MXU_DTYPE = jnp.bfloat16
ACT_DTYPE = jnp.bfloat16
F32 = jnp.float32
HIGHEST = lax.Precision.HIGHEST

Q_ROWS_PER_BLOCK = 4
K_ROWS_PER_BLOCK = 12
ATT_TOK = Q_ROWS_PER_BLOCK * GRID_W


def _cparams(sem):
    return pltpu.CompilerParams(dimension_semantics=sem, vmem_limit_bytes=VMEM_LIMIT_BYTES)


def _const_spec(shape):
    nd = len(shape)
    return pl.BlockSpec(shape, lambda *_: (0,) * nd, pipeline_mode=pl.Buffered(1))


def _dot(a, b, **kw):
    return jnp.dot(a, b, preferred_element_type=F32, **kw)


def _dot_nt(a, b):
    return lax.dot_general(a, b, (((1,), (1,)), ((), ())), preferred_element_type=F32)


def _gelu(x):
    return 0.5 * x * (1.0 + lax.erf(x * (1.0 / math.sqrt(2.0))))


def _mod_body(c_ref, w_ref, b_ref, o_ref):
    c = c_ref[...]
    s = c * jax.nn.sigmoid(c)
    o_ref[...] = _dot(s, w_ref[...], precision=HIGHEST) + b_ref[...]


def _modulation(cc, ada_w, ada_b):
    depth, d, n = ada_w.shape
    r = cc.shape[0]
    tn = 1024
    return pl.pallas_call(
        _mod_body,
        grid=(depth, n // tn),
        in_specs=[
            pl.BlockSpec((r, d), lambda i, j: (0, 0)),
            pl.BlockSpec((None, d, tn), lambda i, j: (i, 0, j)),
            pl.BlockSpec((None, 1, tn), lambda i, j: (i, 0, j)),
        ],
        out_specs=pl.BlockSpec((None, r, tn), lambda i, j: (i, 0, j)),
        out_shape=jax.ShapeDtypeStruct((depth, r, n), F32),
        compiler_params=_cparams(("arbitrary", "arbitrary")),
        name="adaln_modulation",
    )(cc, ada_w, ada_b.reshape(depth, 1, n))


def _mod_spec(row_fn, chunk):
    return pl.BlockSpec((None, 1, D_MODEL), lambda *g: (row_fn(*g), 0, chunk))


def _rope(acc, cos, sin):
    lane = lax.broadcasted_iota(jnp.int32, cos.shape, 1)
    first_half = (lane % 32) < 16
    outs = []
    for c0 in range(0, acc.shape[1], LANES):
        xc = acc[:, c0:c0 + LANES]
        partner = jnp.where(first_half, pltpu.roll(xc, LANES - 16, 1), pltpu.roll(xc, 16, 1))
        outs.append(xc * cos + partner * sin)
    return outs


def _in_proj_body(*refs, segs, rope, n_out):
    x_ref, sh_ref, sc_ref, g_ref, w_ref = refs[:5]
    pos = 5
    if rope:
        cos_ref, sin_ref = refs[5:7]
        pos = 7
    out_refs = refs[pos:pos + n_out]
    x = x_ref[...]
    ms = jnp.mean(x * x, axis=-1, keepdims=True)
    y = x * lax.rsqrt(ms + NORM_EPS) * g_ref[...]
    h = (y * (1.0 + sc_ref[...]) + sh_ref[...]).astype(MXU_DTYPE)
    for (c0, width, kind, oi) in segs:
        step = min(width, 512)
        for cc in range(0, width, step):
            acc = _dot(h, w_ref[:, c0 + cc:c0 + cc + step])
            if kind == "gelu":
                acc = _gelu(acc)
            elif kind == "sigmoid":
                acc = jax.nn.sigmoid(acc)
            if kind == "rope" and rope:
                for k, piece in enumerate(_rope(acc, cos_ref[...], sin_ref[...])):
                    out_refs[oi][:, cc + k * LANES:cc + (k + 1) * LANES] = piece.astype(ACT_DTYPE)
            else:
                out_refs[oi][:, cc:cc + step] = acc.astype(ACT_DTYPE)


def _in_proj(x, mod, mod_row, norm_g, w, segs, rope_tabs, name):
    bx, l, d = x.shape
    tm = min(512, l)
    n_out = max(s[3] for s in segs) + 1
    widths = [sum(s[1] for s in segs if s[3] == oi) for oi in range(n_out)]
    rope = rope_tabs is not None
    in_specs = [
        pl.BlockSpec((None, tm, d), lambda b, i: (b, i, 0)),
        _mod_spec(lambda b, i: mod_row(b), 0),
        _mod_spec(lambda b, i: mod_row(b), 1),
        _const_spec((1, d)),
        _const_spec(w.shape),
    ]
    args = [x, mod, mod, norm_g.reshape(1, d), w]
    if rope:
        in_specs += [pl.BlockSpec((tm, LANES), lambda b, i: (i, 0))] * 2
        args += list(rope_tabs)
    return pl.pallas_call(
        functools.partial(_in_proj_body, segs=segs, rope=rope, n_out=n_out),
        grid=(bx, l // tm),
        in_specs=in_specs,
        out_specs=[pl.BlockSpec((None, tm, wd), lambda b, i: (b, i, 0)) for wd in widths],
        out_shape=[jax.ShapeDtypeStruct((bx, l, wd), ACT_DTYPE) for wd in widths],
        compiler_params=_cparams(("arbitrary", "arbitrary")),
        name=name,
    )(*args)


FULL_SEGS = (
    (OFF_SGU, 2 * BRANCH_DIM, "gelu", 0),
    (OFF_QKV, BRANCH_DIM, "rope", 1),
    (OFF_QKV + BRANCH_DIM, BRANCH_DIM, "rope", 2),
    (OFF_QKV + 2 * BRANCH_DIM, BRANCH_DIM, "plain", 3),
    (OFF_HY, 3 * HY_DIM, "plain", 4),
    (OFF_GATE, 3 * D_MODEL, "sigmoid", 5),
)
KV_SEGS = ((0, BRANCH_DIM, "plain", 0), (BRANCH_DIM, BRANCH_DIM, "plain", 1))


def _sgu_body(uv_ref, g_ref, w_ref, b_ref, o_ref):
    ts = uv_ref.shape[0]
    v = uv_ref[:, BRANCH_DIM:].astype(F32)
    ms = jnp.mean(v * v, axis=-1, keepdims=True)
    vb = (v * lax.rsqrt(ms + NORM_EPS) * g_ref[...]).astype(MXU_DTYPE)
    for n in range(ts // SGU_CHUNK):
        rs = slice(n * SGU_CHUNK, (n + 1) * SGU_CHUNK)
        for g in range(SGU_GROUPS):
            cs = slice(g * SGU_GROUP_DIM, (g + 1) * SGU_GROUP_DIM)
            mixed = _dot(w_ref[g], vb[rs, cs]) + b_ref[g]
            o_ref[rs, cs] = (uv_ref[rs, cs].astype(F32) * mixed).astype(ACT_DTYPE)


def _sgu(uv, norm_g, w_s, b_s):
    bx, l, _ = uv.shape
    ts = min(512, l)
    return pl.pallas_call(
        _sgu_body,
        grid=(bx, l // ts),
        in_specs=[
            pl.BlockSpec((None, ts, 2 * BRANCH_DIM), lambda b, i: (b, i, 0)),
            _const_spec((1, BRANCH_DIM)),
            _const_spec(w_s.shape),
            _const_spec((SGU_GROUPS, SGU_CHUNK, 1)),
        ],
        out_specs=pl.BlockSpec((None, ts, BRANCH_DIM), lambda b, i: (b, i, 0)),
        out_shape=jax.ShapeDtypeStruct((bx, l, BRANCH_DIM), ACT_DTYPE),
        compiler_params=_cparams(("arbitrary", "arbitrary")),
        name="spatial_gating",
    )(uv, norm_g.reshape(1, BRANCH_DIM), w_s.astype(MXU_DTYPE), b_s.reshape(SGU_GROUPS, SGU_CHUNK, 1))


def _att_plan(rows):
    kr = min(NA_WIN_ROWS, rows)
    assert rows % Q_ROWS_PER_BLOCK == 0 and rows >= K_ROWS_PER_BLOCK and kr == NA_WIN_ROWS
    patterns, type_of, key_start = [], [], []
    for rb in range(0, rows, Q_ROWS_PER_BLOCK):
        ks = min(max(rb - NA_WIN_ROWS // 2, 0), rows - K_ROWS_PER_BLOCK)
        assert ks % Q_ROWS_PER_BLOCK == 0
        pat = []
        for qi in range(Q_ROWS_PER_BLOCK):
            rq = rb + qi
            r0 = min(max(rq - kr // 2, 0), rows - kr)
            for j in range(K_ROWS_PER_BLOCK):
                rk = ks + j
                pat.append(rk - rq + NA_WIN_ROWS - 1 if r0 <= rk < r0 + kr else None)
        pat = tuple(pat)
        if pat not in patterns:
            patterns.append(pat)
        type_of.append(patterns.index(pat))
        key_start.append(ks // Q_ROWS_PER_BLOCK)
    return patterns, type_of, key_start


def _bias_body(rpb_ref, o_ref, *, patterns):
    n_dr, n_dc = 2 * NA_WIN_ROWS - 1, 2 * NA_WIN_COLS - 1
    h = pl.program_id(0)
    qc = lax.broadcasted_iota(jnp.int32, (GRID_W, GRID_W), 0)
    kc = lax.broadcasted_iota(jnp.int32, (GRID_W, GRID_W), 1)
    cstart = jnp.clip(qc - NA_WIN_COLS // 2, 0, GRID_W - NA_WIN_COLS)
    col_ok = (kc >= cstart) & (kc < cstart + NA_WIN_COLS)
    dc = jnp.clip(kc - qc + NA_WIN_COLS - 1, 0, n_dc - 1)
    neg = jnp.full((GRID_W, GRID_W), NEG_INF, F32)
    tiles = []
    for dr in range(n_dr):
        t = jnp.zeros((GRID_W, GRID_W), F32)
        for d in range(n_dc):
            t = jnp.where(dc == d, rpb_ref[(h * n_dr + dr) * n_dc + d], t)
        tiles.append(jnp.where(col_ok, t * LOG2E, neg))
    for ty, pat in enumerate(patterns):
        for qi in range(Q_ROWS_PER_BLOCK):
            for j in range(K_ROWS_PER_BLOCK):
                dr = pat[qi * K_ROWS_PER_BLOCK + j]
                o_ref[ty, qi * GRID_W:(qi + 1) * GRID_W, j * GRID_W:(j + 1) * GRID_W] = neg if dr is None else tiles[dr]


def _bias_tables(rpb, patterns):
    nt = len(patterns)
    return pl.pallas_call(
        functools.partial(_bias_body, patterns=patterns),
        grid=(NA_HEADS,),
        in_specs=[pl.BlockSpec(memory_space=pltpu.SMEM)],
        out_specs=pl.BlockSpec((nt, None, ATT_TOK, K_ROWS_PER_BLOCK * GRID_W), lambda h: (0, h, 0, 0)),
        out_shape=jax.ShapeDtypeStruct((nt, NA_HEADS, ATT_TOK, K_ROWS_PER_BLOCK * GRID_W), F32),
        compiler_params=_cparams(("arbitrary",)),
        name="attention_bias_tables",
    )(rpb.reshape(-1))


def _head_mask(shape, hh):
    lane = lax.broadcasted_iota(jnp.int32, shape, 1)
    return (lane >= hh * NA_HEAD_DIM) & (lane < (hh + 1) * NA_HEAD_DIM)


LOG2E = math.log2(math.e)
QK_SCALE = NA_HEAD_DIM ** -0.5 * LOG2E


def _pair_attention(qp, keys, biases, values):
    accs = []
    for hh in range(2):
        qm = jnp.where(_head_mask(qp.shape, hh), qp, jnp.zeros_like(qp))
        scores = []
        for kj, bj in zip(keys, biases[hh]):
            s = _dot_nt(qm, kj)
            scores.append(s if bj is None else s + bj)
        m = functools.reduce(jnp.maximum, [jnp.max(s, axis=-1, keepdims=True) for s in scores])
        acc = None
        for s, v in zip(scores, values):
            e = jnp.exp2(s - m).astype(MXU_DTYPE)
            part = _dot(e, jnp.where(_head_mask(v.shape, hh), v, jnp.ones_like(v)))
            acc = part if acc is None else acc + part
        accs.append(acc)
    first = _head_mask(accs[0].shape, 0)
    num = jnp.where(first, accs[0], accs[1])
    den = jnp.where(first, pltpu.roll(accs[0], NA_HEAD_DIM, 1), pltpu.roll(accs[1], NA_HEAD_DIM, 1))
    return num / den


def _attn_body(ty_ref, kb_ref, q_ref, k0, k1, k2, v0, v1, v2, kc_ref, vc_ref, bias_ref, o_ref):
    del ty_ref, kb_ref
    for p in range(NA_HEADS // 2):
        cs = slice(p * LANES, (p + 1) * LANES)
        qp = q_ref[:, cs] * QK_SCALE
        keys = [k0[:, cs], k1[:, cs], k2[:, cs], kc_ref[:, cs]]
        values = [v0[:, cs], v1[:, cs], v2[:, cs], vc_ref[:, cs]]
        biases = [[bias_ref[2 * p + hh, :, j * ATT_TOK:(j + 1) * ATT_TOK] for j in range(3)] + [None]
                  for hh in range(2)]
        o_ref[:, cs] = _pair_attention(qp, keys, biases, values).astype(ACT_DTYPE)


def _neighborhood_attention(q, k, v, kc, vc, bias, type_of, key_start):
    b, l, _ = q.shape
    lc = kc.shape[1]
    n_blk = l // ATT_TOK
    tok = lambda off: pl.BlockSpec((None, ATT_TOK, BRANCH_DIM), lambda r, bb, ty, kb: (bb, kb[r] + off, 0))
    ctx = pl.BlockSpec((None, lc, BRANCH_DIM), lambda r, bb, ty, kb: (bb, 0, 0))
    grid_spec = pltpu.PrefetchScalarGridSpec(
        num_scalar_prefetch=2,
        grid=(n_blk, b),
        in_specs=[
            pl.BlockSpec((None, ATT_TOK, BRANCH_DIM), lambda r, bb, ty, kb: (bb, r, 0)),
            tok(0), tok(1), tok(2), tok(0), tok(1), tok(2), ctx, ctx,
            pl.BlockSpec((None, NA_HEADS, ATT_TOK, K_ROWS_PER_BLOCK * GRID_W), lambda r, bb, ty, kb: (ty[r], 0, 0, 0)),
        ],
        out_specs=pl.BlockSpec((None, ATT_TOK, BRANCH_DIM), lambda r, bb, ty, kb: (bb, r, 0)),
    )
    return pl.pallas_call(
        _attn_body,
        grid_spec=grid_spec,
        out_shape=jax.ShapeDtypeStruct((b, l, BRANCH_DIM), ACT_DTYPE),
        compiler_params=_cparams(("arbitrary", "arbitrary")),
        name="neighborhood_attention",
    )(jnp.asarray(type_of, jnp.int32), jnp.asarray(key_start, jnp.int32), q, k, k, k, v, v, v, kc, vc, bias)


def _ctx_attn_body(q_ref, k_ref, v_ref, o_ref):
    for p in range(NA_HEADS // 2):
        cs = slice(p * LANES, (p + 1) * LANES)
        qp = q_ref[:, cs] * QK_SCALE
        o_ref[:, cs] = _pair_attention(qp, [k_ref[:, cs]], [[None], [None]], [v_ref[:, cs]]).astype(ACT_DTYPE)


def _context_attention(q, k, v):
    b, lc, _ = q.shape
    spec = pl.BlockSpec((None, lc, BRANCH_DIM), lambda bb: (bb, 0, 0))
    return pl.pallas_call(
        _ctx_attn_body,
        grid=(b,),
        in_specs=[spec, spec, spec],
        out_specs=spec,
        out_shape=jax.ShapeDtypeStruct((b, lc, BRANCH_DIM), ACT_DTYPE),
        compiler_params=_cparams(("arbitrary",)),
        name="context_attention",
    )(q, k, v)


HY_CW = 256


def _dft_consts(l):
    n = 2 * l
    idx = jnp.arange(l, dtype=jnp.int32)
    ang = ((idx[:, None] * idx[None, :]) % n).astype(F32) * (2.0 * math.pi / n)
    cs = jnp.concatenate([jnp.cos(ang), -jnp.sin(ang)], axis=1).astype(MXU_DTYPE)
    sign = jnp.where(idx % 2 == 0, 1.0, -1.0).astype(F32).reshape(l, 1)
    wf = jnp.where(idx == 0, 1.0 / n, 2.0 / n).astype(F32).reshape(l, 1)
    return cs, sign, wf


def _filter_consts(l):
    t = jnp.linspace(0.0, 1.0, l, dtype=F32)[:, None]
    n_bands = (HY_EMB - 1) // 2
    bands = jnp.linspace(1e-4, n_bands - 1, n_bands, dtype=F32)[None, :]
    ang = (2.0 * math.pi) * jnp.arange(l, dtype=F32)[:, None] / l * bands
    z = jnp.concatenate([t, jnp.cos(ang), -jnp.sin(ang)], axis=-1)
    z = jnp.pad(z, ((0, 0), (0, LANES - HY_EMB)))
    max_decay = math.log(HY_DECAY_TARGET) / HY_FAST_DECAY_PCT
    min_decay = math.log(HY_DECAY_TARGET) / HY_SLOW_DECAY_PCT
    deltas = jnp.abs(jnp.linspace(min_decay, max_decay, HY_DIM, dtype=F32)).reshape(1, HY_DIM)
    return z, t, deltas


def _filter_body(z_ref, w1_ref, b1_ref, w2_ref, b2_ref, fr_ref, w3a_ref, w3b_ref, t_ref, dl_ref,
                 sg_ref, wf_ref, cs_ref, kc_ref, ks_ref, kn_ref):
    l = z_ref.shape[0]
    fr = fr_ref[...]
    h = jnp.sin(fr * (_dot(z_ref[...], w1_ref[...], precision=HIGHEST) + b1_ref[...]))
    h = jnp.sin(fr * (_dot(h, w2_ref[...], precision=HIGHEST) + b2_ref[...]))
    decay = jnp.exp(-t_ref[...] * dl_ref[...])
    h0 = _dot(h, w3a_ref[...], precision=HIGHEST) * decay
    h1 = _dot(h, w3b_ref[...], precision=HIGHEST) * decay
    row = lax.broadcasted_iota(jnp.int32, h1.shape, 0)
    h1 = jnp.where(row == 0, 0.0, h1)
    den = jnp.sum(jnp.abs(h0), axis=0, keepdims=True) + jnp.sum(jnp.abs(h1), axis=0, keepdims=True)
    hs = (h0 + h1) / den
    hd = (h0 - h1) / den
    kn_ref[...] = jnp.sum(sg_ref[...] * hs, axis=0, keepdims=True) * (1.0 / (2 * l))
    kc_ref[...] = (wf_ref[...] * _dot(cs_ref[:, :l], hs.astype(MXU_DTYPE))).astype(kc_ref.dtype)
    ks_ref[...] = (wf_ref[...] * _dot(cs_ref[:, l:], hd.astype(MXU_DTYPE))).astype(ks_ref.dtype)


def _hyena_filters(l, dft, fconsts, w1, b1, w2, b2, w3, freq):
    cs, sign, wf = dft
    z, t, deltas = fconsts
    cw = HY_CW
    nb = HY_DIM // cw
    w1p = jnp.pad(w1, ((0, LANES - HY_EMB), (0, 0)))
    hid = HY_FILTER_HIDDEN
    small = lambda shape: pl.BlockSpec(shape, lambda n, cb: (0,) * len(shape))
    spec_out = pl.BlockSpec((None, l, cw), lambda n, cb: (n, 0, cb))
    return pl.pallas_call(
        _filter_body,
        grid=(HY_ORDER, nb),
        in_specs=[
            small((l, LANES)), small((LANES, hid)), small((1, hid)), small((hid, hid)), small((1, hid)), small((1, hid)),
            pl.BlockSpec((hid, cw), lambda n, cb: (0, n * 2 * nb + cb)),
            pl.BlockSpec((hid, cw), lambda n, cb: (0, n * 2 * nb + nb + cb)),
            small((l, 1)),
            pl.BlockSpec((1, cw), lambda n, cb: (0, cb)),
            small((l, 1)), small((l, 1)),
            _const_spec((l, 2 * l)),
        ],
        out_specs=[spec_out, spec_out, pl.BlockSpec((None, 1, cw), lambda n, cb: (n, 0, cb))],
        out_shape=[jax.ShapeDtypeStruct((HY_ORDER, l, HY_DIM), ACT_DTYPE)] * 2
        + [jax.ShapeDtypeStruct((HY_ORDER, 1, HY_DIM), F32)],
        compiler_params=_cparams(("arbitrary", "arbitrary")),
        name="hyena_filter_spectrum",
    )(z, w1p, b1.reshape(1, hid), w2, b2.reshape(1, hid), freq.reshape(1, hid), w3, w3, t, deltas, sign, wf, cs)


HY_TM = 512
CONV_HALO = 16


def _conv3_rows(p_ref, w_ref, b_ref, r0, rows):
    l = p_ref.shape[0]
    lo, hi = max(r0 - CONV_HALO, 0), min(r0 + rows + CONV_HALO, l)
    x = p_ref[lo:hi, :].astype(F32)
    n = hi - lo
    prev, nxt = pltpu.roll(x, 1, 0), pltpu.roll(x, n - 1, 0)
    row = lax.broadcasted_iota(jnp.int32, x.shape, 0)
    if lo == 0:
        prev = jnp.where(row == 0, 0.0, prev)
    if hi == l:
        nxt = jnp.where(row == n - 1, 0.0, nxt)
    y = prev * w_ref[0:1, :] + x * w_ref[1:2, :] + nxt * w_ref[2:3, :] + b_ref[...]
    return y[r0 - lo:r0 - lo + rows]


def _hyena_body(pv_ref, p1_ref, p2_ref, wv_ref, w1_ref, w2_ref, bv_ref, b1_ref, b2_ref, skip_ref,
                kc_ref, ks_ref, kn_ref, sg_ref, cs_ref, o_ref, zf_ref, zb_ref, y_ref):
    l = pv_ref.shape[0]
    tm = min(HY_TM, l)
    chunks = [slice(r0, r0 + tm) for r0 in range(0, l, tm)]
    zn = jnp.zeros((1, pv_ref.shape[1]), F32)
    for rs in chunks:
        z = _conv3_rows(pv_ref, wv_ref, bv_ref, rs.start, tm)
        zf_ref[rs] = z
        zb_ref[rs] = z.astype(MXU_DTYPE)
        zn = zn + jnp.sum(sg_ref[rs] * z, axis=0, keepdims=True)
    gates = ((p1_ref, w1_ref, b1_ref), (p2_ref, w2_ref, b2_ref))
    for n, (p_ref, w_ref, b_ref) in enumerate(gates):
        for rs in chunks:
            zc = _dot(cs_ref[rs, :l], zb_ref[...])
            zs = _dot(cs_ref[rs, l:], zb_ref[...])
            kc, ks = kc_ref[n, rs].astype(F32), ks_ref[n, rs].astype(F32)
            y_ref[rs] = (zc * kc - zs * ks).astype(MXU_DTYPE)
            y_ref[l + rs.start:l + rs.stop] = (zc * ks + zs * kc).astype(MXU_DTYPE)
        nyq = zn * kn_ref[n]
        zn = jnp.zeros_like(zn)
        for rs in chunks:
            y = _dot(cs_ref[rs, :], y_ref[...]) + sg_ref[rs] * nyq
            gate = _conv3_rows(p_ref, w_ref, b_ref, rs.start, tm)
            z = gate * (y + zf_ref[rs] * skip_ref[n:n + 1, :])
            if n + 1 < HY_ORDER:
                zf_ref[rs] = z
                zb_ref[rs] = z.astype(MXU_DTYPE)
                zn = zn + jnp.sum(sg_ref[rs] * z, axis=0, keepdims=True)
            else:
                o_ref[rs] = z.astype(ACT_DTYPE)


def _hyena(p, conv_w, conv_b, skip, filters, dft):
    bx, l, _ = p.shape
    cs, sign, _ = dft
    kcs, kss, kns = filters
    cw = HY_CW
    nb = HY_DIM // cw
    pspec = lambda part: pl.BlockSpec((None, l, cw), lambda cb, b: (b, 0, part * nb + cb))
    wspec = lambda part: pl.BlockSpec((3, cw), lambda cb, b: (0, part * nb + cb))
    bspec = lambda part: pl.BlockSpec((1, cw), lambda cb, b: (0, part * nb + cb))
    fspec = pl.BlockSpec((HY_ORDER, l, cw), lambda cb, b: (0, 0, cb), pipeline_mode=pl.Buffered(1))
    return pl.pallas_call(
        _hyena_body,
        grid=(nb, bx),
        in_specs=[
            pspec(0), pspec(1), pspec(2), wspec(0), wspec(1), wspec(2), bspec(0), bspec(1), bspec(2),
            pl.BlockSpec((HY_ORDER, cw), lambda cb, b: (0, cb)),
            fspec, fspec,
            pl.BlockSpec((HY_ORDER, 1, cw), lambda cb, b: (0, 0, cb)),
            _const_spec((l, 1)), _const_spec((l, 2 * l)),
        ],
        out_specs=pl.BlockSpec((None, l, cw), lambda cb, b: (b, 0, cb)),
        out_shape=jax.ShapeDtypeStruct((bx, l, HY_DIM), ACT_DTYPE),
        scratch_shapes=[pltpu.VMEM((l, cw), F32), pltpu.VMEM((l, cw), MXU_DTYPE), pltpu.VMEM((2 * l, cw), MXU_DTYPE)],
        compiler_params=_cparams(("arbitrary", "arbitrary")),
        name="hyena_long_conv",
    )(p, p, p, conv_w, conv_w, conv_w, conv_b.reshape(1, -1), conv_b.reshape(1, -1), conv_b.reshape(1, -1),
      skip, kcs, kss, kns, sign, cs)


def _merge_body(x_ref, a_ref, at_ref, hy_ref, gt_ref, g1_ref, wb_ref, wo_ref, o_ref):
    d = D_MODEL
    m = gt_ref[:, 0:d].astype(F32) * _dot(a_ref[...], wb_ref[0])
    m = m + gt_ref[:, d:2 * d].astype(F32) * _dot(at_ref[...], wb_ref[1])
    m = m + gt_ref[:, 2 * d:3 * d].astype(F32) * _dot(hy_ref[...], wb_ref[2])
    o_ref[...] = x_ref[...] + g1_ref[...] * _dot(m.astype(MXU_DTYPE), wo_ref[...])


def _merge(x, a, attn, hy, gates, mod, mod_row, wb, wo):
    bx, l, d = x.shape
    tm = min(512, l)
    tile = lambda w: pl.BlockSpec((None, tm, w), lambda b, i: (b, i, 0))
    return pl.pallas_call(
        _merge_body,
        grid=(bx, l // tm),
        in_specs=[tile(d), tile(BRANCH_DIM), tile(BRANCH_DIM), tile(BRANCH_DIM), tile(3 * d),
                  _mod_spec(lambda b, i: mod_row(b), 2), _const_spec(wb.shape), _const_spec(wo.shape)],
        out_specs=tile(d),
        out_shape=jax.ShapeDtypeStruct((bx, l, d), F32),
        compiler_params=_cparams(("arbitrary", "arbitrary")),
        name="branch_merge",
    )(x, a, attn, hy, gates, mod, wb, wo)


FFN_TM = 512
FFN_TN = 256


def _ffn_body(xp_ref, x_ref, xn_ref, sh_ref, sc_ref, g2_ref, ng_ref, wu_ref, cw_ref, cb_ref, wd_ref, fg_ref,
              o_ref, h_ref, gt_ref, *, final, seq_len):
    tm = x_ref.shape[0]
    halo = CONV_HALO
    n_ext = tm + 2 * halo

    def norm_mod(x):
        ms = jnp.mean(x * x, axis=-1, keepdims=True)
        y = x * lax.rsqrt(ms + NORM_EPS) * ng_ref[...]
        return (y * (1.0 + sc_ref[...]) + sh_ref[...]).astype(MXU_DTYPE)

    h_ref[0:halo] = norm_mod(xp_ref[...])
    h_ref[halo:halo + tm] = norm_mod(x_ref[...])
    h_ref[halo + tm:n_ext] = norm_mod(xn_ref[...])
    row = lax.broadcasted_iota(jnp.int32, (n_ext, 1), 0)
    pos = (pl.program_id(1) * tm + row + (seq_len - halo)) % seq_len
    first, last = pos == 0, pos == seq_len - 1
    for j in range(D_FF // FFN_TN):
        cs = slice(j * FFN_TN, (j + 1) * FFN_TN)
        a = _dot(h_ref[...], wu_ref[:, cs])
        prev = jnp.where(first, 0.0, pltpu.roll(a, 1, 0))
        nxt = jnp.where(last, 0.0, pltpu.roll(a, n_ext - 1, 0))
        a = prev * cw_ref[0:1, cs] + a * cw_ref[1:2, cs] + nxt * cw_ref[2:3, cs] + cb_ref[:, cs]
        gate = _dot(h_ref[halo:halo + tm], wu_ref[:, D_FF + j * FFN_TN:D_FF + (j + 1) * FFN_TN])
        gt_ref[:, cs] = (_gelu(a[halo:halo + tm]) * gate).astype(MXU_DTYPE)
    xn = x_ref[...] + g2_ref[...] * _dot(gt_ref[...], wd_ref[...])
    if final:
        ms = jnp.mean(xn * xn, axis=-1, keepdims=True)
        xn = xn * lax.rsqrt(ms + NORM_EPS) * fg_ref[...]
    o_ref[...] = xn


def _ffn(x, mod, mod_row, norm_g, w_up, conv_w, conv_b, w_down, final_g, final, seq_len):
    bx, l, d = x.shape
    tm = min(FFN_TM, l)
    halo = CONV_HALO
    per, last_blk = tm // halo, l // halo - 1
    tile = pl.BlockSpec((None, tm, d), lambda b, i: (b, i, 0))
    return pl.pallas_call(
        functools.partial(_ffn_body, final=final, seq_len=seq_len),
        grid=(bx, l // tm),
        in_specs=[
            pl.BlockSpec((None, halo, d), lambda b, i: (b, jnp.maximum(i * per - 1, 0), 0)),
            tile,
            pl.BlockSpec((None, halo, d), lambda b, i: (b, jnp.minimum((i + 1) * per, last_blk), 0)),
            _mod_spec(lambda b, i: mod_row(b), 3), _mod_spec(lambda b, i: mod_row(b), 4),
            _mod_spec(lambda b, i: mod_row(b), 5),
            _const_spec((1, d)), _const_spec(w_up.shape), _const_spec(conv_w.shape), _const_spec((1, D_FF)),
            _const_spec(w_down.shape), _const_spec((1, d)),
        ],
        out_specs=tile,
        out_shape=jax.ShapeDtypeStruct((bx, l, d), F32),
        scratch_shapes=[pltpu.VMEM((tm + 2 * halo, d), MXU_DTYPE), pltpu.VMEM((tm, D_FF), MXU_DTYPE)],
        compiler_params=_cparams(("arbitrary", "arbitrary")),
        name="conv_ffn",
    )(x, x, x, mod, mod, mod, norm_g.reshape(1, d), w_up, conv_w, conv_b.reshape(1, -1), w_down,
      final_g.reshape(1, d))


def _rope_tables(l):
    half = NA_HEAD_DIM // 2
    quarter = half // 2
    inv_freq = ROPE_THETA ** (-jnp.arange(quarter, dtype=F32) * 2.0 / half)
    pos = jnp.arange(l)
    ang_r = (pos // GRID_W).astype(F32)[:, None] * inv_freq[None, :]
    ang_c = (pos % GRID_W).astype(F32)[:, None] * inv_freq[None, :]
    cos_h = jnp.concatenate([jnp.cos(ang_r), jnp.cos(ang_r), jnp.cos(ang_c), jnp.cos(ang_c)], axis=-1)
    sin_h = jnp.concatenate([-jnp.sin(ang_r), jnp.sin(ang_r), -jnp.sin(ang_c), jnp.sin(ang_c)], axis=-1)
    reps = LANES // NA_HEAD_DIM
    return jnp.tile(cos_h, (1, reps)), jnp.tile(sin_h, (1, reps))


def kernel(x, c, ctx, c_ctx, ada_w, ada_b, norm1_g, norm2_g, w_in, sgu_norm_g, sgu_w, sgu_b, na_rpb, hy_conv_w, hy_conv_b, hy_filt_w1, hy_filt_b1, hy_filt_w2, hy_filt_b2, hy_filt_w3, hy_sin_freq, hy_skip, w_branch, w_out, ffn_w_up, ffn_conv_w, ffn_conv_b, ffn_w_down, final_norm_g):
    b, l, d = x.shape
    lc = ctx.shape[1]
    assert d == D_MODEL and l % ATT_TOK == 0 and l % GRID_W == 0

    n_rows = -(-(b + 1) // 8) * 8
    cc = jnp.concatenate([c, c_ctx[None, :], jnp.zeros((n_rows - b - 1, d), F32)], axis=0)
    mod = _modulation(cc, ada_w, ada_b).reshape(DEPTH * n_rows, 1, N_MOD * d)

    rope_tabs = _rope_tables(l)
    dft_l, fc_l = _dft_consts(l), _filter_consts(l)
    dft_c, fc_c = _dft_consts(lc), _filter_consts(lc)
    patterns, type_of, key_start = _att_plan(l // GRID_W)

    xc = ctx
    for i in range(DEPTH):
        update_ctx = i < DEPTH - 1
        lat_row = lambda bb, i=i: i * n_rows + bb
        ctx_row = lambda bb, i=i: i * n_rows + b
        w_in_i = w_in[i].astype(MXU_DTYPE)
        wb_i, wo_i = w_branch[i].astype(MXU_DTYPE), w_out[i].astype(MXU_DTYPE)
        wup_i, wdn_i = ffn_w_up[i].astype(MXU_DTYPE), ffn_w_down[i].astype(MXU_DTYPE)
        filt_args = (hy_filt_w1[i], hy_filt_b1[i], hy_filt_w2[i], hy_filt_b2[i], hy_filt_w3[i], hy_sin_freq[i])
        bias = _bias_tables(na_rpb[i], patterns)

        if update_ctx:
            uv_c, q_c, k_c, v_c, hy_c, gt_c = _in_proj(xc, mod, ctx_row, norm1_g[i], w_in_i, FULL_SEGS, None,
                                                       "context_in_proj")
        else:
            w_kv = w_in_i[:, OFF_QKV + BRANCH_DIM:OFF_HY]
            k_c, v_c = _in_proj(xc, mod, ctx_row, norm1_g[i], w_kv, KV_SEGS, None, "context_kv_proj")

        uv, q, k, v, hy, gt = _in_proj(x, mod, lat_row, norm1_g[i], w_in_i, FULL_SEGS, rope_tabs, "latent_in_proj")
        attn = _neighborhood_attention(q, k, v, k_c, v_c, bias, type_of, key_start)
        a = _sgu(uv, sgu_norm_g[i], sgu_w[i], sgu_b[i])
        filt = _hyena_filters(l, dft_l, fc_l, *filt_args)
        hyo = _hyena(hy, hy_conv_w[i], hy_conv_b[i], hy_skip[i], filt, dft_l)
        x = _merge(x, a, attn, hyo, gt, mod, lat_row, wb_i, wo_i)
        x = _ffn(x, mod, lat_row, norm2_g[i], wup_i, ffn_conv_w[i], ffn_conv_b[i], wdn_i, final_norm_g,
                 final=not update_ctx, seq_len=l)

        if update_ctx:
            attn_c = _context_attention(q_c, k_c, v_c)
            a_c = _sgu(uv_c, sgu_norm_g[i], sgu_w[i], sgu_b[i])
            filt_c = _hyena_filters(lc, dft_c, fc_c, *filt_args)
            hyo_c = _hyena(hy_c, hy_conv_w[i], hy_conv_b[i], hy_skip[i], filt_c, dft_c)
            xc = _merge(xc, a_c, attn_c, hyo_c, gt_c, mod, ctx_row, wb_i, wo_i)
            grp = math.gcd(b, max(1, l // lc))
            xc = _ffn(xc.reshape(b // grp, grp * lc, d), mod, ctx_row, norm2_g[i], wup_i, ffn_conv_w[i],
                      ffn_conv_b[i], wdn_i, final_norm_g, final=False, seq_len=lc).reshape(b, lc, d)
    return x
```

```python
import functools
import math

import jax
import jax.numpy as jnp
import numpy as np
from jax import lax
from jax.experimental import pallas as pl
from jax.experimental.pallas import tpu as pltpu

D_MODEL = 1024
DEPTH = 2
GRID_W = 64
NORM_EPS = 1e-6
N_MOD = 6
BRANCH_DIM = D_MODEL // 2
SGU_GROUP_DIM = 128
SGU_GROUPS = BRANCH_DIM // SGU_GROUP_DIM
SGU_CHUNK = 128
NA_HEAD_DIM = 64
NA_HEADS = BRANCH_DIM // NA_HEAD_DIM
NA_WIN_ROWS = 8
NA_WIN_COLS = 16
ROPE_THETA = 10000.0
NEG_INF = -1e30
HY_DIM = BRANCH_DIM
HY_ORDER = 2
HY_EMB = 33
HY_FILTER_HIDDEN = 64
HY_FAST_DECAY_PCT = 0.3
HY_SLOW_DECAY_PCT = 1.5
HY_DECAY_TARGET = 1e-2
D_FF = 128 * ((8 * D_MODEL + 3 * 128 - 1) // (3 * 128))
OFF_SGU = 0
OFF_QKV = OFF_SGU + 2 * BRANCH_DIM
OFF_HY = OFF_QKV + 3 * BRANCH_DIM
OFF_GATE = OFF_HY + (HY_ORDER + 1) * HY_DIM
IN_COLS = OFF_GATE + 3 * D_MODEL

LANES = 128
MXU_WIDTH = 256
VMEM_LIMIT_BYTES = 56 * 1024 * 1024

MXU_DTYPE = jnp.bfloat16
ACT_DTYPE = jnp.bfloat16
F32 = jnp.float32
HIGHEST = lax.Precision.HIGHEST

Q_ROWS_PER_BLOCK = 4
K_ROWS_PER_BLOCK = 12
ATT_TOK = Q_ROWS_PER_BLOCK * GRID_W


def _cparams(sem):
    return pltpu.CompilerParams(dimension_semantics=sem, vmem_limit_bytes=VMEM_LIMIT_BYTES)


def _const_spec(shape):
    nd = len(shape)
    return pl.BlockSpec(shape, lambda *_: (0,) * nd, pipeline_mode=pl.Buffered(1))


def _dot(a, b, **kw):
    return jnp.dot(a, b, preferred_element_type=F32, **kw)


def _dot_nt(a, b):
    return lax.dot_general(a, b, (((1,), (1,)), ((), ())), preferred_element_type=F32)


def _gelu(x):
    return 0.5 * x * (1.0 + lax.erf(x * (1.0 / math.sqrt(2.0))))


def _mod_body(c_ref, w_ref, b_ref, o_ref):
    c = c_ref[...]
    s = c * jax.nn.sigmoid(c)
    o_ref[...] = _dot(s, w_ref[...], precision=HIGHEST) + b_ref[...]


def _modulation(cc, ada_w, ada_b):
    depth, d, n = ada_w.shape
    r = cc.shape[0]
    tn = 1024
    return pl.pallas_call(
        _mod_body,
        grid=(depth, n // tn),
        in_specs=[
            pl.BlockSpec((r, d), lambda i, j: (0, 0)),
            pl.BlockSpec((None, d, tn), lambda i, j: (i, 0, j)),
            pl.BlockSpec((None, 1, tn), lambda i, j: (i, 0, j)),
        ],
        out_specs=pl.BlockSpec((None, r, tn), lambda i, j: (i, 0, j)),
        out_shape=jax.ShapeDtypeStruct((depth, r, n), F32),
        compiler_params=_cparams(("arbitrary", "arbitrary")),
        name="adaln_modulation",
    )(cc, ada_w, ada_b.reshape(depth, 1, n))


def _mod_spec(row_fn, chunk):
    return pl.BlockSpec((None, 1, D_MODEL), lambda *g: (row_fn(*g), 0, chunk))


def _rope(acc, cos, sin):
    lane = lax.broadcasted_iota(jnp.int32, cos.shape, 1)
    first_half = (lane % 32) < 16
    outs = []
    for c0 in range(0, acc.shape[1], LANES):
        xc = acc[:, c0:c0 + LANES]
        partner = jnp.where(first_half, pltpu.roll(xc, LANES - 16, 1), pltpu.roll(xc, 16, 1))
        outs.append(xc * cos + partner * sin)
    return outs


def _in_proj_body(*refs, segs, rope, n_out):
    x_ref, sh_ref, sc_ref, g_ref, w_ref = refs[:5]
    pos = 5
    if rope:
        cos_ref, sin_ref = refs[5:7]
        pos = 7
    out_refs = refs[pos:pos + n_out]
    x = x_ref[...]
    ms = jnp.mean(x * x, axis=-1, keepdims=True)
    y = x * lax.rsqrt(ms + NORM_EPS) * g_ref[...]
    h = (y * (1.0 + sc_ref[...]) + sh_ref[...]).astype(MXU_DTYPE)
    for (c0, width, kind, oi) in segs:
        step = min(width, 512)
        for cc in range(0, width, step):
            acc = _dot(h, w_ref[:, c0 + cc:c0 + cc + step])
            if kind == "gelu":
                acc = _gelu(acc)
            elif kind == "sigmoid":
                acc = jax.nn.sigmoid(acc)
            if kind == "rope" and rope:
                for k, piece in enumerate(_rope(acc, cos_ref[...], sin_ref[...])):
                    out_refs[oi][:, cc + k * LANES:cc + (k + 1) * LANES] = piece.astype(ACT_DTYPE)
            else:
                out_refs[oi][:, cc:cc + step] = acc.astype(ACT_DTYPE)


def _in_proj(x, mod, mod_row, norm_g, w, segs, rope_tabs, name):
    bx, l, d = x.shape
    tm = min(512, l)
    n_out = max(s[3] for s in segs) + 1
    widths = [sum(s[1] for s in segs if s[3] == oi) for oi in range(n_out)]
    rope = rope_tabs is not None
    in_specs = [
        pl.BlockSpec((None, tm, d), lambda b, i: (b, i, 0)),
        _mod_spec(lambda b, i: mod_row(b), 0),
        _mod_spec(lambda b, i: mod_row(b), 1),
        _const_spec((1, d)),
        _const_spec(w.shape),
    ]
    args = [x, mod, mod, norm_g.reshape(1, d), w]
    if rope:
        in_specs += [pl.BlockSpec((tm, LANES), lambda b, i: (i, 0))] * 2
        args += list(rope_tabs)
    return pl.pallas_call(
        functools.partial(_in_proj_body, segs=segs, rope=rope, n_out=n_out),
        grid=(bx, l // tm),
        in_specs=in_specs,
        out_specs=[pl.BlockSpec((None, tm, wd), lambda b, i: (b, i, 0)) for wd in widths],
        out_shape=[jax.ShapeDtypeStruct((bx, l, wd), ACT_DTYPE) for wd in widths],
        compiler_params=_cparams(("arbitrary", "arbitrary")),
        name=name,
    )(*args)


FULL_SEGS = (
    (OFF_SGU, 2 * BRANCH_DIM, "gelu", 0),
    (OFF_QKV, BRANCH_DIM, "rope", 1),
    (OFF_QKV + BRANCH_DIM, BRANCH_DIM, "rope", 2),
    (OFF_QKV + 2 * BRANCH_DIM, BRANCH_DIM, "plain", 3),
    (OFF_HY, 3 * HY_DIM, "plain", 4),
    (OFF_GATE, 3 * D_MODEL, "sigmoid", 5),
)
KV_SEGS = ((0, BRANCH_DIM, "plain", 0), (BRANCH_DIM, BRANCH_DIM, "plain", 1))


def _sgu_body(uv_ref, g_ref, w_ref, b_ref, o_ref):
    ts = uv_ref.shape[0]
    v = uv_ref[:, BRANCH_DIM:].astype(F32)
    ms = jnp.mean(v * v, axis=-1, keepdims=True)
    vb = (v * lax.rsqrt(ms + NORM_EPS) * g_ref[...]).astype(MXU_DTYPE)
    for n in range(ts // SGU_CHUNK):
        rs = slice(n * SGU_CHUNK, (n + 1) * SGU_CHUNK)
        for g in range(SGU_GROUPS):
            cs = slice(g * SGU_GROUP_DIM, (g + 1) * SGU_GROUP_DIM)
            mixed = _dot(w_ref[g], vb[rs, cs]) + b_ref[g]
            o_ref[rs, cs] = (uv_ref[rs, cs].astype(F32) * mixed).astype(ACT_DTYPE)


def _sgu(uv, norm_g, w_s, b_s):
    bx, l, _ = uv.shape
    ts = min(512, l)
    return pl.pallas_call(
        _sgu_body,
        grid=(bx, l // ts),
        in_specs=[
            pl.BlockSpec((None, ts, 2 * BRANCH_DIM), lambda b, i: (b, i, 0)),
            _const_spec((1, BRANCH_DIM)),
            _const_spec(w_s.shape),
            _const_spec((SGU_GROUPS, SGU_CHUNK, 1)),
        ],
        out_specs=pl.BlockSpec((None, ts, BRANCH_DIM), lambda b, i: (b, i, 0)),
        out_shape=jax.ShapeDtypeStruct((bx, l, BRANCH_DIM), ACT_DTYPE),
        compiler_params=_cparams(("arbitrary", "arbitrary")),
        name="spatial_gating",
    )(uv, norm_g.reshape(1, BRANCH_DIM), w_s.astype(MXU_DTYPE), b_s.reshape(SGU_GROUPS, SGU_CHUNK, 1))


def _att_plan(rows):
    kr = min(NA_WIN_ROWS, rows)
    assert rows % Q_ROWS_PER_BLOCK == 0 and rows >= K_ROWS_PER_BLOCK and kr == NA_WIN_ROWS
    patterns, type_of, key_start = [], [], []
    for rb in range(0, rows, Q_ROWS_PER_BLOCK):
        ks = min(max(rb - NA_WIN_ROWS // 2, 0), rows - K_ROWS_PER_BLOCK)
        assert ks % Q_ROWS_PER_BLOCK == 0
        pat = []
        for qi in range(Q_ROWS_PER_BLOCK):
            rq = rb + qi
            r0 = min(max(rq - kr // 2, 0), rows - kr)
            for j in range(K_ROWS_PER_BLOCK):
                rk = ks + j
                pat.append(rk - rq + NA_WIN_ROWS - 1 if r0 <= rk < r0 + kr else None)
        pat = tuple(pat)
        if pat not in patterns:
            patterns.append(pat)
        type_of.append(patterns.index(pat))
        key_start.append(ks // Q_ROWS_PER_BLOCK)
    return patterns, type_of, key_start


def _bias_body(rpb_ref, o_ref, *, patterns):
    n_dr, n_dc = 2 * NA_WIN_ROWS - 1, 2 * NA_WIN_COLS - 1
    h = pl.program_id(0)
    qc = lax.broadcasted_iota(jnp.int32, (GRID_W, GRID_W), 0)
    kc = lax.broadcasted_iota(jnp.int32, (GRID_W, GRID_W), 1)
    cstart = jnp.clip(qc - NA_WIN_COLS // 2, 0, GRID_W - NA_WIN_COLS)
    col_ok = (kc >= cstart) & (kc < cstart + NA_WIN_COLS)
    dc = jnp.clip(kc - qc + NA_WIN_COLS - 1, 0, n_dc - 1)
    neg = jnp.full((GRID_W, GRID_W), NEG_INF, F32)
    tiles = []
    for dr in range(n_dr):
        t = jnp.zeros((GRID_W, GRID_W), F32)
        for d in range(n_dc):
            t = jnp.where(dc == d, rpb_ref[(h * n_dr + dr) * n_dc + d], t)
        tiles.append(jnp.where(col_ok, t * LOG2E, neg))
    for ty, pat in enumerate(patterns):
        for qi in range(Q_ROWS_PER_BLOCK):
            for j in range(K_ROWS_PER_BLOCK):
                dr = pat[qi * K_ROWS_PER_BLOCK + j]
                o_ref[ty, qi * GRID_W:(qi + 1) * GRID_W, j * GRID_W:(j + 1) * GRID_W] = neg if dr is None else tiles[dr]


def _bias_tables(rpb, patterns):
    nt = len(patterns)
    return pl.pallas_call(
        functools.partial(_bias_body, patterns=patterns),
        grid=(NA_HEADS,),
        in_specs=[pl.BlockSpec(memory_space=pltpu.SMEM)],
        out_specs=pl.BlockSpec((nt, None, ATT_TOK, K_ROWS_PER_BLOCK * GRID_W), lambda h: (0, h, 0, 0)),
        out_shape=jax.ShapeDtypeStruct((nt, NA_HEADS, ATT_TOK, K_ROWS_PER_BLOCK * GRID_W), F32),
        compiler_params=_cparams(("arbitrary",)),
        name="attention_bias_tables",
    )(rpb.reshape(-1))


def _head_mask(shape, hh):
    lane = lax.broadcasted_iota(jnp.int32, shape, 1)
    return (lane >= hh * NA_HEAD_DIM) & (lane < (hh + 1) * NA_HEAD_DIM)


LOG2E = math.log2(math.e)
QK_SCALE = NA_HEAD_DIM ** -0.5 * LOG2E


def _pair_attention(qp, keys, biases, values):
    m_rows = qp.shape[0]
    q2 = jnp.concatenate([jnp.where(_head_mask(qp.shape, hh), qp, jnp.zeros_like(qp)) for hh in range(2)], axis=0)
    s_all = _dot_nt(q2, jnp.concatenate(keys, axis=0))
    scores, c0 = [], 0
    for j, kj in enumerate(keys):
        s = s_all[:, c0:c0 + kj.shape[0]]
        c0 += kj.shape[0]
        if biases[0][j] is not None:
            s = s + jnp.concatenate([biases[0][j], biases[1][j]], axis=0)
        scores.append(s)
    m = functools.reduce(jnp.maximum, [jnp.max(s, axis=-1, keepdims=True) for s in scores])
    e = jnp.concatenate([jnp.exp2(s - m).astype(MXU_DTYPE) for s in scores], axis=1)
    vcat = jnp.concatenate(values, axis=0)
    acc = _dot(e, jnp.concatenate([vcat, jnp.ones_like(vcat)], axis=1))
    out = acc[:, :LANES] / acc[:, LANES:]
    return jnp.where(_head_mask((m_rows, LANES), 0), out[:m_rows], out[m_rows:])


ATT_BATCH = 4


def _attn_body(ty_ref, kb_ref, q_ref, k0, k1, k2, v0, v1, v2, kc_ref, vc_ref, bias_ref, o_ref):
    del ty_ref, kb_ref
    for p in range(NA_HEADS // 2):
        cs = slice(p * LANES, (p + 1) * LANES)
        biases = [[bias_ref[2 * p + hh, :, j * ATT_TOK:(j + 1) * ATT_TOK] for j in range(3)] + [None]
                  for hh in range(2)]
        for e in range(q_ref.shape[0]):
            qp = q_ref[e, :, cs] * QK_SCALE
            keys = [k0[e, :, cs], k1[e, :, cs], k2[e, :, cs], kc_ref[e, :, cs]]
            values = [v0[e, :, cs], v1[e, :, cs], v2[e, :, cs], vc_ref[e, :, cs]]
            o_ref[e, :, cs] = _pair_attention(qp, keys, biases, values).astype(ACT_DTYPE)


def _neighborhood_attention(q, k, v, kc, vc, bias, type_of, key_start):
    b, l, _ = q.shape
    lc = kc.shape[1]
    n_blk = l // ATT_TOK
    nb = math.gcd(b, ATT_BATCH)
    tok = lambda off: pl.BlockSpec((nb, ATT_TOK, BRANCH_DIM), lambda r, bb, ty, kb: (bb, kb[r] + off, 0))
    ctx = pl.BlockSpec((nb, lc, BRANCH_DIM), lambda r, bb, ty, kb: (bb, 0, 0))
    grid_spec = pltpu.PrefetchScalarGridSpec(
        num_scalar_prefetch=2,
        grid=(n_blk, b // nb),
        in_specs=[
            pl.BlockSpec((nb, ATT_TOK, BRANCH_DIM), lambda r, bb, ty, kb: (bb, r, 0)),
            tok(0), tok(1), tok(2), tok(0), tok(1), tok(2), ctx, ctx,
            pl.BlockSpec((None, NA_HEADS, ATT_TOK, K_ROWS_PER_BLOCK * GRID_W), lambda r, bb, ty, kb: (ty[r], 0, 0, 0)),
        ],
        out_specs=pl.BlockSpec((nb, ATT_TOK, BRANCH_DIM), lambda r, bb, ty, kb: (bb, r, 0)),
    )
    return pl.pallas_call(
        _attn_body,
        grid_spec=grid_spec,
        out_shape=jax.ShapeDtypeStruct((b, l, BRANCH_DIM), ACT_DTYPE),
        compiler_params=_cparams(("arbitrary", "arbitrary")),
        name="neighborhood_attention",
    )(jnp.asarray(type_of, jnp.int32), jnp.asarray(key_start, jnp.int32), q, k, k, k, v, v, v, kc, vc, bias)


def _ctx_attn_body(q_ref, k_ref, v_ref, o_ref):
    for p in range(NA_HEADS // 2):
        cs = slice(p * LANES, (p + 1) * LANES)
        qp = q_ref[:, cs] * QK_SCALE
        o_ref[:, cs] = _pair_attention(qp, [k_ref[:, cs]], [[None], [None]], [v_ref[:, cs]]).astype(ACT_DTYPE)


def _context_attention(q, k, v):
    b, lc, _ = q.shape
    spec = pl.BlockSpec((None, lc, BRANCH_DIM), lambda bb: (bb, 0, 0))
    return pl.pallas_call(
        _ctx_attn_body,
        grid=(b,),
        in_specs=[spec, spec, spec],
        out_specs=spec,
        out_shape=jax.ShapeDtypeStruct((b, lc, BRANCH_DIM), ACT_DTYPE),
        compiler_params=_cparams(("arbitrary",)),
        name="context_attention",
    )(q, k, v)


HY_CW = 256


def _dft_consts(l):
    n = 2 * l
    idx = jnp.arange(l, dtype=jnp.int32)
    ang = ((idx[:, None] * idx[None, :]) % n).astype(F32) * (2.0 * math.pi / n)
    cs = jnp.concatenate([jnp.cos(ang), -jnp.sin(ang)], axis=1).astype(MXU_DTYPE)
    sign = jnp.where(idx % 2 == 0, 1.0, -1.0).astype(F32).reshape(l, 1)
    wf = jnp.where(idx == 0, 1.0 / n, 2.0 / n).astype(F32).reshape(l, 1)
    return cs, sign, wf


def _filter_consts(l):
    t = jnp.linspace(0.0, 1.0, l, dtype=F32)[:, None]
    n_bands = (HY_EMB - 1) // 2
    bands = jnp.linspace(1e-4, n_bands - 1, n_bands, dtype=F32)[None, :]
    ang = (2.0 * math.pi) * jnp.arange(l, dtype=F32)[:, None] / l * bands
    z = jnp.concatenate([t, jnp.cos(ang), -jnp.sin(ang)], axis=-1)
    z = jnp.pad(z, ((0, 0), (0, LANES - HY_EMB)))
    max_decay = math.log(HY_DECAY_TARGET) / HY_FAST_DECAY_PCT
    min_decay = math.log(HY_DECAY_TARGET) / HY_SLOW_DECAY_PCT
    deltas = jnp.abs(jnp.linspace(min_decay, max_decay, HY_DIM, dtype=F32)).reshape(1, HY_DIM)
    return z, t, deltas


def _filter_body(z_ref, w1_ref, b1_ref, w2_ref, b2_ref, fr_ref, w3a_ref, w3b_ref, t_ref, dl_ref,
                 sg_ref, wf_ref, cs_ref, kc_ref, ks_ref, kn_ref):
    l = z_ref.shape[0]
    fr = fr_ref[...]
    h = jnp.sin(fr * (_dot(z_ref[...], w1_ref[...], precision=HIGHEST) + b1_ref[...]))
    h = jnp.sin(fr * (_dot(h, w2_ref[...], precision=HIGHEST) + b2_ref[...]))
    decay = jnp.exp(-t_ref[...] * dl_ref[...])
    h0 = _dot(h, w3a_ref[...], precision=HIGHEST) * decay
    h1 = _dot(h, w3b_ref[...], precision=HIGHEST) * decay
    row = lax.broadcasted_iota(jnp.int32, h1.shape, 0)
    h1 = jnp.where(row == 0, 0.0, h1)
    den = jnp.sum(jnp.abs(h0), axis=0, keepdims=True) + jnp.sum(jnp.abs(h1), axis=0, keepdims=True)
    hs = (h0 + h1) / den
    hd = (h0 - h1) / den
    kn_ref[...] = jnp.sum(sg_ref[...] * hs, axis=0, keepdims=True) * (1.0 / (2 * l))
    kc_ref[...] = (wf_ref[...] * _dot(cs_ref[:, :l], hs.astype(MXU_DTYPE))).astype(kc_ref.dtype)
    ks_ref[...] = (wf_ref[...] * _dot(cs_ref[:, l:], hd.astype(MXU_DTYPE))).astype(ks_ref.dtype)


def _hyena_filters(l, dft, fconsts, w1, b1, w2, b2, w3, freq):
    cs, sign, wf = dft
    z, t, deltas = fconsts
    cw = HY_CW
    nb = HY_DIM // cw
    w1p = jnp.pad(w1, ((0, LANES - HY_EMB), (0, 0)))
    hid = HY_FILTER_HIDDEN
    small = lambda shape: pl.BlockSpec(shape, lambda n, cb: (0,) * len(shape))
    spec_out = pl.BlockSpec((None, l, cw), lambda n, cb: (n, 0, cb))
    return pl.pallas_call(
        _filter_body,
        grid=(HY_ORDER, nb),
        in_specs=[
            small((l, LANES)), small((LANES, hid)), small((1, hid)), small((hid, hid)), small((1, hid)), small((1, hid)),
            pl.BlockSpec((hid, cw), lambda n, cb: (0, n * 2 * nb + cb)),
            pl.BlockSpec((hid, cw), lambda n, cb: (0, n * 2 * nb + nb + cb)),
            small((l, 1)),
            pl.BlockSpec((1, cw), lambda n, cb: (0, cb)),
            small((l, 1)), small((l, 1)),
            _const_spec((l, 2 * l)),
        ],
        out_specs=[spec_out, spec_out, pl.BlockSpec((None, 1, cw), lambda n, cb: (n, 0, cb))],
        out_shape=[jax.ShapeDtypeStruct((HY_ORDER, l, HY_DIM), ACT_DTYPE)] * 2
        + [jax.ShapeDtypeStruct((HY_ORDER, 1, HY_DIM), F32)],
        compiler_params=_cparams(("arbitrary", "arbitrary")),
        name="hyena_filter_spectrum",
    )(z, w1p, b1.reshape(1, hid), w2, b2.reshape(1, hid), freq.reshape(1, hid), w3, w3, t, deltas, sign, wf, cs)


HY_TM = 512
CONV_HALO = 16


def _conv3_rows(p_ref, w_ref, b_ref, r0, rows):
    l = p_ref.shape[0]
    lo, hi = max(r0 - CONV_HALO, 0), min(r0 + rows + CONV_HALO, l)
    x = p_ref[lo:hi, :].astype(F32)
    n = hi - lo
    prev, nxt = pltpu.roll(x, 1, 0), pltpu.roll(x, n - 1, 0)
    row = lax.broadcasted_iota(jnp.int32, x.shape, 0)
    if lo == 0:
        prev = jnp.where(row == 0, 0.0, prev)
    if hi == l:
        nxt = jnp.where(row == n - 1, 0.0, nxt)
    y = prev * w_ref[0:1, :] + x * w_ref[1:2, :] + nxt * w_ref[2:3, :] + b_ref[...]
    return y[r0 - lo:r0 - lo + rows]


def _hyena_body(pv_ref, p1_ref, p2_ref, wv_ref, w1_ref, w2_ref, bv_ref, b1_ref, b2_ref, skip_ref,
                kc_ref, ks_ref, kn_ref, sg_ref, cs_ref, o_ref, zf_ref, zb_ref, y_ref):
    l = pv_ref.shape[0]
    tm = min(HY_TM, l)
    chunks = [slice(r0, r0 + tm) for r0 in range(0, l, tm)]
    zn = jnp.zeros((1, pv_ref.shape[1]), F32)
    for rs in chunks:
        z = _conv3_rows(pv_ref, wv_ref, bv_ref, rs.start, tm)
        zf_ref[rs] = z
        zb_ref[rs] = z.astype(MXU_DTYPE)
        zn = zn + jnp.sum(sg_ref[rs] * z, axis=0, keepdims=True)
    gates = ((p1_ref, w1_ref, b1_ref), (p2_ref, w2_ref, b2_ref))
    for n, (p_ref, w_ref, b_ref) in enumerate(gates):
        for rs in chunks:
            zc = _dot(cs_ref[rs, :l], zb_ref[...])
            zs = _dot(cs_ref[rs, l:], zb_ref[...])
            kc, ks = kc_ref[n, rs].astype(F32), ks_ref[n, rs].astype(F32)
            y_ref[rs] = (zc * kc - zs * ks).astype(MXU_DTYPE)
            y_ref[l + rs.start:l + rs.stop] = (zc * ks + zs * kc).astype(MXU_DTYPE)
        nyq = zn * kn_ref[n]
        zn = jnp.zeros_like(zn)
        for rs in chunks:
            y = _dot(cs_ref[rs, :], y_ref[...]) + sg_ref[rs] * nyq
            gate = _conv3_rows(p_ref, w_ref, b_ref, rs.start, tm)
            z = gate * (y + zf_ref[rs] * skip_ref[n:n + 1, :])
            if n + 1 < HY_ORDER:
                zf_ref[rs] = z
                zb_ref[rs] = z.astype(MXU_DTYPE)
                zn = zn + jnp.sum(sg_ref[rs] * z, axis=0, keepdims=True)
            else:
                o_ref[rs] = z.astype(ACT_DTYPE)


def _hyena(p, conv_w, conv_b, skip, filters, dft):
    bx, l, _ = p.shape
    cs, sign, _ = dft
    kcs, kss, kns = filters
    cw = HY_CW
    nb = HY_DIM // cw
    pspec = lambda part: pl.BlockSpec((None, l, cw), lambda cb, b: (b, 0, part * nb + cb))
    wspec = lambda part: pl.BlockSpec((3, cw), lambda cb, b: (0, part * nb + cb))
    bspec = lambda part: pl.BlockSpec((1, cw), lambda cb, b: (0, part * nb + cb))
    fspec = pl.BlockSpec((HY_ORDER, l, cw), lambda cb, b: (0, 0, cb), pipeline_mode=pl.Buffered(1))
    return pl.pallas_call(
        _hyena_body,
        grid=(nb, bx),
        in_specs=[
            pspec(0), pspec(1), pspec(2), wspec(0), wspec(1), wspec(2), bspec(0), bspec(1), bspec(2),
            pl.BlockSpec((HY_ORDER, cw), lambda cb, b: (0, cb)),
            fspec, fspec,
            pl.BlockSpec((HY_ORDER, 1, cw), lambda cb, b: (0, 0, cb)),
            _const_spec((l, 1)), _const_spec((l, 2 * l)),
        ],
        out_specs=pl.BlockSpec((None, l, cw), lambda cb, b: (b, 0, cb)),
        out_shape=jax.ShapeDtypeStruct((bx, l, HY_DIM), ACT_DTYPE),
        scratch_shapes=[pltpu.VMEM((l, cw), F32), pltpu.VMEM((l, cw), MXU_DTYPE), pltpu.VMEM((2 * l, cw), MXU_DTYPE)],
        compiler_params=_cparams(("arbitrary", "arbitrary")),
        name="hyena_long_conv",
    )(p, p, p, conv_w, conv_w, conv_w, conv_b.reshape(1, -1), conv_b.reshape(1, -1), conv_b.reshape(1, -1),
      skip, kcs, kss, kns, sign, cs)


def _merge_body(x_ref, a_ref, at_ref, hy_ref, gt_ref, g1_ref, wb_ref, wo_ref, o_ref):
    d = D_MODEL
    m = gt_ref[:, 0:d].astype(F32) * _dot(a_ref[...], wb_ref[0])
    m = m + gt_ref[:, d:2 * d].astype(F32) * _dot(at_ref[...], wb_ref[1])
    m = m + gt_ref[:, 2 * d:3 * d].astype(F32) * _dot(hy_ref[...], wb_ref[2])
    o_ref[...] = x_ref[...] + g1_ref[...] * _dot(m.astype(MXU_DTYPE), wo_ref[...])


def _merge(x, a, attn, hy, gates, mod, mod_row, wb, wo):
    bx, l, d = x.shape
    tm = min(512, l)
    tile = lambda w: pl.BlockSpec((None, tm, w), lambda b, i: (b, i, 0))
    return pl.pallas_call(
        _merge_body,
        grid=(bx, l // tm),
        in_specs=[tile(d), tile(BRANCH_DIM), tile(BRANCH_DIM), tile(BRANCH_DIM), tile(3 * d),
                  _mod_spec(lambda b, i: mod_row(b), 2), _const_spec(wb.shape), _const_spec(wo.shape)],
        out_specs=tile(d),
        out_shape=jax.ShapeDtypeStruct((bx, l, d), F32),
        compiler_params=_cparams(("arbitrary", "arbitrary")),
        name="branch_merge",
    )(x, a, attn, hy, gates, mod, wb, wo)


FFN_TM = 512
FFN_TN = 256


def _ffn_body(xp_ref, x_ref, xn_ref, sh_ref, sc_ref, g2_ref, ng_ref, wu_ref, cw_ref, cb_ref, wd_ref, fg_ref,
              o_ref, h_ref, gt_ref, *, final, seq_len):
    tm = x_ref.shape[0]
    halo = CONV_HALO
    n_ext = tm + 2 * halo

    def norm_mod(x):
        ms = jnp.mean(x * x, axis=-1, keepdims=True)
        y = x * lax.rsqrt(ms + NORM_EPS) * ng_ref[...]
        return (y * (1.0 + sc_ref[...]) + sh_ref[...]).astype(MXU_DTYPE)

    h_ref[0:halo] = norm_mod(xp_ref[...])
    h_ref[halo:halo + tm] = norm_mod(x_ref[...])
    h_ref[halo + tm:n_ext] = norm_mod(xn_ref[...])
    row = lax.broadcasted_iota(jnp.int32, (n_ext, 1), 0)
    pos = (pl.program_id(1) * tm + row + (seq_len - halo)) % seq_len
    first, last = pos == 0, pos == seq_len - 1
    for j in range(D_FF // FFN_TN):
        cs = slice(j * FFN_TN, (j + 1) * FFN_TN)
        a = _dot(h_ref[...], wu_ref[:, cs])
        prev = jnp.where(first, 0.0, pltpu.roll(a, 1, 0))
        nxt = jnp.where(last, 0.0, pltpu.roll(a, n_ext - 1, 0))
        a = prev * cw_ref[0:1, cs] + a * cw_ref[1:2, cs] + nxt * cw_ref[2:3, cs] + cb_ref[:, cs]
        gate = _dot(h_ref[halo:halo + tm], wu_ref[:, D_FF + j * FFN_TN:D_FF + (j + 1) * FFN_TN])
        gt_ref[:, cs] = (_gelu(a[halo:halo + tm]) * gate).astype(MXU_DTYPE)
    xn = x_ref[...] + g2_ref[...] * _dot(gt_ref[...], wd_ref[...])
    if final:
        ms = jnp.mean(xn * xn, axis=-1, keepdims=True)
        xn = xn * lax.rsqrt(ms + NORM_EPS) * fg_ref[...]
    o_ref[...] = xn


def _ffn(x, mod, mod_row, norm_g, w_up, conv_w, conv_b, w_down, final_g, final, seq_len):
    bx, l, d = x.shape
    tm = min(FFN_TM, l)
    halo = CONV_HALO
    per, last_blk = tm // halo, l // halo - 1
    tile = pl.BlockSpec((None, tm, d), lambda b, i: (b, i, 0))
    return pl.pallas_call(
        functools.partial(_ffn_body, final=final, seq_len=seq_len),
        grid=(bx, l // tm),
        in_specs=[
            pl.BlockSpec((None, halo, d), lambda b, i: (b, jnp.maximum(i * per - 1, 0), 0)),
            tile,
            pl.BlockSpec((None, halo, d), lambda b, i: (b, jnp.minimum((i + 1) * per, last_blk), 0)),
            _mod_spec(lambda b, i: mod_row(b), 3), _mod_spec(lambda b, i: mod_row(b), 4),
            _mod_spec(lambda b, i: mod_row(b), 5),
            _const_spec((1, d)), _const_spec(w_up.shape), _const_spec(conv_w.shape), _const_spec((1, D_FF)),
            _const_spec(w_down.shape), _const_spec((1, d)),
        ],
        out_specs=tile,
        out_shape=jax.ShapeDtypeStruct((bx, l, d), F32),
        scratch_shapes=[pltpu.VMEM((tm + 2 * halo, d), MXU_DTYPE), pltpu.VMEM((tm, D_FF), MXU_DTYPE)],
        compiler_params=_cparams(("arbitrary", "arbitrary")),
        name="conv_ffn",
    )(x, x, x, mod, mod, mod, norm_g.reshape(1, d), w_up, conv_w, conv_b.reshape(1, -1), w_down,
      final_g.reshape(1, d))


def _rope_tables(l):
    half = NA_HEAD_DIM // 2
    quarter = half // 2
    inv_freq = ROPE_THETA ** (-jnp.arange(quarter, dtype=F32) * 2.0 / half)
    pos = jnp.arange(l)
    ang_r = (pos // GRID_W).astype(F32)[:, None] * inv_freq[None, :]
    ang_c = (pos % GRID_W).astype(F32)[:, None] * inv_freq[None, :]
    cos_h = jnp.concatenate([jnp.cos(ang_r), jnp.cos(ang_r), jnp.cos(ang_c), jnp.cos(ang_c)], axis=-1)
    sin_h = jnp.concatenate([-jnp.sin(ang_r), jnp.sin(ang_r), -jnp.sin(ang_c), jnp.sin(ang_c)], axis=-1)
    reps = LANES // NA_HEAD_DIM
    return jnp.tile(cos_h, (1, reps)), jnp.tile(sin_h, (1, reps))


def kernel(x, c, ctx, c_ctx, ada_w, ada_b, norm1_g, norm2_g, w_in, sgu_norm_g, sgu_w, sgu_b, na_rpb, hy_conv_w, hy_conv_b, hy_filt_w1, hy_filt_b1, hy_filt_w2, hy_filt_b2, hy_filt_w3, hy_sin_freq, hy_skip, w_branch, w_out, ffn_w_up, ffn_conv_w, ffn_conv_b, ffn_w_down, final_norm_g):
    b, l, d = x.shape
    lc = ctx.shape[1]
    assert d == D_MODEL and l % ATT_TOK == 0 and l % GRID_W == 0

    n_rows = -(-(b + 1) // 8) * 8
    cc = jnp.concatenate([c, c_ctx[None, :], jnp.zeros((n_rows - b - 1, d), F32)], axis=0)
    mod = _modulation(cc, ada_w, ada_b).reshape(DEPTH * n_rows, 1, N_MOD * d)

    rope_tabs = _rope_tables(l)
    dft_l, fc_l = _dft_consts(l), _filter_consts(l)
    dft_c, fc_c = _dft_consts(lc), _filter_consts(lc)
    patterns, type_of, key_start = _att_plan(l // GRID_W)

    xc = ctx
    for i in range(DEPTH):
        update_ctx = i < DEPTH - 1
        lat_row = lambda bb, i=i: i * n_rows + bb
        ctx_row = lambda bb, i=i: i * n_rows + b
        w_in_i = w_in[i].astype(MXU_DTYPE)
        wb_i, wo_i = w_branch[i].astype(MXU_DTYPE), w_out[i].astype(MXU_DTYPE)
        wup_i, wdn_i = ffn_w_up[i].astype(MXU_DTYPE), ffn_w_down[i].astype(MXU_DTYPE)
        filt_args = (hy_filt_w1[i], hy_filt_b1[i], hy_filt_w2[i], hy_filt_b2[i], hy_filt_w3[i], hy_sin_freq[i])
        bias = _bias_tables(na_rpb[i], patterns)

        if update_ctx:
            uv_c, q_c, k_c, v_c, hy_c, gt_c = _in_proj(xc, mod, ctx_row, norm1_g[i], w_in_i, FULL_SEGS, None,
                                                       "context_in_proj")
        else:
            w_kv = w_in_i[:, OFF_QKV + BRANCH_DIM:OFF_HY]
            k_c, v_c = _in_proj(xc, mod, ctx_row, norm1_g[i], w_kv, KV_SEGS, None, "context_kv_proj")

        uv, q, k, v, hy, gt = _in_proj(x, mod, lat_row, norm1_g[i], w_in_i, FULL_SEGS, rope_tabs, "latent_in_proj")
        attn = _neighborhood_attention(q, k, v, k_c, v_c, bias, type_of, key_start)
        a = _sgu(uv, sgu_norm_g[i], sgu_w[i], sgu_b[i])
        filt = _hyena_filters(l, dft_l, fc_l, *filt_args)
        hyo = _hyena(hy, hy_conv_w[i], hy_conv_b[i], hy_skip[i], filt, dft_l)
        x = _merge(x, a, attn, hyo, gt, mod, lat_row, wb_i, wo_i)
        x = _ffn(x, mod, lat_row, norm2_g[i], wup_i, ffn_conv_w[i], ffn_conv_b[i], wdn_i, final_norm_g,
                 final=not update_ctx, seq_len=l)

        if update_ctx:
            attn_c = _context_attention(q_c, k_c, v_c)
            a_c = _sgu(uv_c, sgu_norm_g[i], sgu_w[i], sgu_b[i])
            filt_c = _hyena_filters(lc, dft_c, fc_c, *filt_args)
            hyo_c = _hyena(hy_c, hy_conv_w[i], hy_conv_b[i], hy_skip[i], filt_c, dft_c)
            xc = _merge(xc, a_c, attn_c, hyo_c, gt_c, mod, ctx_row, wb_i, wo_i)
            grp = math.gcd(b, max(1, l // lc))
            xc = _ffn(xc.reshape(b // grp, grp * lc, d), mod, ctx_row, norm2_g[i], wup_i, ffn_conv_w[i],
                      ffn_conv_b[i], wdn_i, final_norm_g, final=False, seq_len=lc).reshape(b, lc, d)
    return x
```

```python
import functools
import math

import jax
import jax.numpy as jnp
import numpy as np
from jax import lax
from jax.experimental import pallas as pl
from jax.experimental.pallas import tpu as pltpu

D_MODEL = 1024
DEPTH = 2
GRID_W = 64
NORM_EPS = 1e-6
N_MOD = 6
BRANCH_DIM = D_MODEL // 2
SGU_GROUP_DIM = 128
SGU_GROUPS = BRANCH_DIM // SGU_GROUP_DIM
SGU_CHUNK = 128
NA_HEAD_DIM = 64
NA_HEADS = BRANCH_DIM // NA_HEAD_DIM
NA_WIN_ROWS = 8
NA_WIN_COLS = 16
ROPE_THETA = 10000.0
NEG_INF = -1e30
HY_DIM = BRANCH_DIM
HY_ORDER = 2
HY_EMB = 33
HY_FILTER_HIDDEN = 64
HY_FAST_DECAY_PCT = 0.3
HY_SLOW_DECAY_PCT = 1.5
HY_DECAY_TARGET = 1e-2
D_FF = 128 * ((8 * D_MODEL + 3 * 128 - 1) // (3 * 128))
OFF_SGU = 0
OFF_QKV = OFF_SGU + 2 * BRANCH_DIM
OFF_HY = OFF_QKV + 3 * BRANCH_DIM
OFF_GATE = OFF_HY + (HY_ORDER + 1) * HY_DIM
IN_COLS = OFF_GATE + 3 * D_MODEL

LANES = 128
MXU_WIDTH = 256
VMEM_LIMIT_BYTES = 56 * 1024 * 1024

MXU_DTYPE = jnp.bfloat16
ACT_DTYPE = jnp.bfloat16
F32 = jnp.float32
HIGHEST = lax.Precision.HIGHEST

Q_ROWS_PER_BLOCK = 4
K_ROWS_PER_BLOCK = 12
ATT_TOK = Q_ROWS_PER_BLOCK * GRID_W


def _cparams(sem):
    return pltpu.CompilerParams(dimension_semantics=sem, vmem_limit_bytes=VMEM_LIMIT_BYTES)


def _const_spec(shape):
    nd = len(shape)
    return pl.BlockSpec(shape, lambda *_: (0,) * nd, pipeline_mode=pl.Buffered(1))


def _layer_spec(arr, layer):
    nd = arr.ndim - 1
    return pl.BlockSpec((None,) + arr.shape[1:], lambda *_: (layer,) + (0,) * nd, pipeline_mode=pl.Buffered(1))


def _dot(a, b, **kw):
    return jnp.dot(a, b, preferred_element_type=F32, **kw)


def _dot_nt(a, b):
    return lax.dot_general(a, b, (((1,), (1,)), ((), ())), preferred_element_type=F32)


def _gelu(x):
    return 0.5 * x * (1.0 + lax.erf(x * (1.0 / math.sqrt(2.0))))


def _mod_body(c_ref, w_ref, b_ref, o_ref):
    c = c_ref[...]
    s = c * jax.nn.sigmoid(c)
    o_ref[...] = _dot(s, w_ref[...], precision=HIGHEST) + b_ref[...]


def _modulation(cc, ada_w, ada_b):
    depth, d, n = ada_w.shape
    r = cc.shape[0]
    tn = 1024
    return pl.pallas_call(
        _mod_body,
        grid=(depth, n // tn),
        in_specs=[
            pl.BlockSpec((r, d), lambda i, j: (0, 0)),
            pl.BlockSpec((None, d, tn), lambda i, j: (i, 0, j)),
            pl.BlockSpec((None, 1, tn), lambda i, j: (i, 0, j)),
        ],
        out_specs=pl.BlockSpec((None, r, tn), lambda i, j: (i, 0, j)),
        out_shape=jax.ShapeDtypeStruct((depth, r, n), F32),
        compiler_params=_cparams(("arbitrary", "arbitrary")),
        name="adaln_modulation",
    )(cc, ada_w, ada_b.reshape(depth, 1, n))


def _mod_spec(row_fn, chunk):
    return pl.BlockSpec((None, 1, D_MODEL), lambda *g: (row_fn(*g), 0, chunk))


def _rope(acc, cos, sin):
    lane = lax.broadcasted_iota(jnp.int32, cos.shape, 1)
    first_half = (lane % 32) < 16
    outs = []
    for c0 in range(0, acc.shape[1], LANES):
        xc = acc[:, c0:c0 + LANES]
        partner = jnp.where(first_half, pltpu.roll(xc, LANES - 16, 1), pltpu.roll(xc, 16, 1))
        outs.append(xc * cos + partner * sin)
    return outs


def _spatial_gating(u, v, g_ref, w_ref, b_ref, o_ref):
    ms = jnp.mean(v * v, axis=-1, keepdims=True)
    vb = (v * lax.rsqrt(ms + NORM_EPS) * g_ref[...]).astype(MXU_DTYPE)
    for n in range(u.shape[0] // SGU_CHUNK):
        rs = slice(n * SGU_CHUNK, (n + 1) * SGU_CHUNK)
        for g in range(SGU_GROUPS):
            cs = slice(g * SGU_GROUP_DIM, (g + 1) * SGU_GROUP_DIM)
            mixed = _dot(w_ref[g], vb[rs, cs]) + b_ref[g]
            o_ref[rs, cs] = (u[rs, cs] * mixed).astype(ACT_DTYPE)


def _in_proj_body(*refs, segs, rope, sgu, n_out):
    x_ref, sh_ref, sc_ref, g_ref, w_ref = refs[:5]
    pos = 5
    if rope:
        cos_ref, sin_ref = refs[pos:pos + 2]
        pos += 2
    if sgu:
        sg_ref, sw_ref, sb_ref = refs[pos:pos + 3]
        pos += 3
    out_refs = refs[pos:pos + n_out]
    x = x_ref[...]
    ms = jnp.mean(x * x, axis=-1, keepdims=True)
    y = x * lax.rsqrt(ms + NORM_EPS) * g_ref[...]
    h = (y * (1.0 + sc_ref[...]) + sh_ref[...]).astype(MXU_DTYPE)
    for (c0, width, kind, oi) in segs:
        if kind == "sgu":
            u = _gelu(_dot(h, w_ref[:, c0:c0 + BRANCH_DIM]))
            v = _gelu(_dot(h, w_ref[:, c0 + BRANCH_DIM:c0 + 2 * BRANCH_DIM]))
            _spatial_gating(u, v, sg_ref, sw_ref, sb_ref, out_refs[oi])
            continue
        step = min(width, 512)
        for cc in range(0, width, step):
            acc = _dot(h, w_ref[:, c0 + cc:c0 + cc + step])
            if kind == "sigmoid":
                acc = jax.nn.sigmoid(acc)
            if kind == "rope" and rope:
                for k, piece in enumerate(_rope(acc, cos_ref[...], sin_ref[...])):
                    out_refs[oi][:, cc + k * LANES:cc + (k + 1) * LANES] = piece.astype(ACT_DTYPE)
            else:
                out_refs[oi][:, cc:cc + step] = acc.astype(ACT_DTYPE)


def _in_proj(x, mod, mod_row, norm_g, w, layer, segs, rope_tabs, sgu, name):
    bx, l, d = x.shape
    tm = min(512, l)
    n_out = max(s[3] for s in segs) + 1
    out_w = lambda s: BRANCH_DIM if s[2] == "sgu" else s[1]
    widths = [sum(out_w(s) for s in segs if s[3] == oi) for oi in range(n_out)]
    rope = rope_tabs is not None
    use_sgu = any(s[2] == "sgu" for s in segs)
    in_specs = [
        pl.BlockSpec((None, tm, d), lambda b, i: (b, i, 0)),
        _mod_spec(lambda b, i: mod_row(b), 0),
        _mod_spec(lambda b, i: mod_row(b), 1),
        _layer_spec(norm_g, layer),
        _layer_spec(w, layer),
    ]
    args = [x, mod, mod, norm_g, w]
    if rope:
        in_specs += [pl.BlockSpec((tm, LANES), lambda b, i: (i, 0))] * 2
        args += list(rope_tabs)
    if use_sgu:
        in_specs += [_layer_spec(p, layer) for p in sgu]
        args += list(sgu)
    return pl.pallas_call(
        functools.partial(_in_proj_body, segs=segs, rope=rope, sgu=use_sgu, n_out=n_out),
        grid=(bx, l // tm),
        in_specs=in_specs,
        out_specs=[pl.BlockSpec((None, tm, wd), lambda b, i: (b, i, 0)) for wd in widths],
        out_shape=[jax.ShapeDtypeStruct((bx, l, wd), ACT_DTYPE) for wd in widths],
        compiler_params=_cparams(("arbitrary", "arbitrary")),
        name=name,
    )(*args)


FULL_SEGS = (
    (OFF_SGU, 2 * BRANCH_DIM, "sgu", 0),
    (OFF_QKV, BRANCH_DIM, "rope", 1),
    (OFF_QKV + BRANCH_DIM, BRANCH_DIM, "rope", 2),
    (OFF_QKV + 2 * BRANCH_DIM, BRANCH_DIM, "plain", 3),
    (OFF_HY, 3 * HY_DIM, "plain", 4),
    (OFF_GATE, 3 * D_MODEL, "sigmoid", 5),
)
KV_SEGS = ((OFF_QKV + BRANCH_DIM, BRANCH_DIM, "plain", 0), (OFF_QKV + 2 * BRANCH_DIM, BRANCH_DIM, "plain", 1))


def _att_plan(rows):
    kr = min(NA_WIN_ROWS, rows)
    assert rows % Q_ROWS_PER_BLOCK == 0 and rows >= K_ROWS_PER_BLOCK and kr == NA_WIN_ROWS
    patterns, type_of, key_start = [], [], []
    for rb in range(0, rows, Q_ROWS_PER_BLOCK):
        ks = min(max(rb - NA_WIN_ROWS // 2, 0), rows - K_ROWS_PER_BLOCK)
        assert ks % Q_ROWS_PER_BLOCK == 0
        pat = []
        for qi in range(Q_ROWS_PER_BLOCK):
            rq = rb + qi
            r0 = min(max(rq - kr // 2, 0), rows - kr)
            for j in range(K_ROWS_PER_BLOCK):
                rk = ks + j
                pat.append(rk - rq + NA_WIN_ROWS - 1 if r0 <= rk < r0 + kr else None)
        pat = tuple(pat)
        if pat not in patterns:
            patterns.append(pat)
        type_of.append(patterns.index(pat))
        key_start.append(ks // Q_ROWS_PER_BLOCK)
    return patterns, type_of, key_start


def _bias_body(rpb_ref, o_ref, *, patterns):
    n_dr, n_dc = 2 * NA_WIN_ROWS - 1, 2 * NA_WIN_COLS - 1
    h = pl.program_id(0)
    qc = lax.broadcasted_iota(jnp.int32, (GRID_W, GRID_W), 0)
    kc = lax.broadcasted_iota(jnp.int32, (GRID_W, GRID_W), 1)
    cstart = jnp.clip(qc - NA_WIN_COLS // 2, 0, GRID_W - NA_WIN_COLS)
    col_ok = (kc >= cstart) & (kc < cstart + NA_WIN_COLS)
    dc = jnp.clip(kc - qc + NA_WIN_COLS - 1, 0, n_dc - 1)
    neg = jnp.full((GRID_W, GRID_W), NEG_INF, F32)
    tiles = []
    for dr in range(n_dr):
        t = jnp.zeros((GRID_W, GRID_W), F32)
        for d in range(n_dc):
            t = jnp.where(dc == d, rpb_ref[(h * n_dr + dr) * n_dc + d], t)
        tiles.append(jnp.where(col_ok, t * LOG2E, neg))
    for ty, pat in enumerate(patterns):
        for qi in range(Q_ROWS_PER_BLOCK):
            for j in range(K_ROWS_PER_BLOCK):
                dr = pat[qi * K_ROWS_PER_BLOCK + j]
                o_ref[ty, qi * GRID_W:(qi + 1) * GRID_W, j * GRID_W:(j + 1) * GRID_W] = neg if dr is None else tiles[dr]


def _bias_tables(rpb, patterns):
    nt = len(patterns)
    return pl.pallas_call(
        functools.partial(_bias_body, patterns=patterns),
        grid=(NA_HEADS,),
        in_specs=[pl.BlockSpec(memory_space=pltpu.SMEM)],
        out_specs=pl.BlockSpec((nt, None, ATT_TOK, K_ROWS_PER_BLOCK * GRID_W), lambda h: (0, h, 0, 0)),
        out_shape=jax.ShapeDtypeStruct((nt, NA_HEADS, ATT_TOK, K_ROWS_PER_BLOCK * GRID_W), F32),
        compiler_params=_cparams(("arbitrary",)),
        name="attention_bias_tables",
    )(rpb.reshape(-1))


def _head_mask(shape, hh):
    lane = lax.broadcasted_iota(jnp.int32, shape, 1)
    return (lane >= hh * NA_HEAD_DIM) & (lane < (hh + 1) * NA_HEAD_DIM)


LOG2E = math.log2(math.e)
QK_SCALE = NA_HEAD_DIM ** -0.5 * LOG2E


def _pair_attention(qp, keys, biases, values):
    m_rows = qp.shape[0]
    q2 = jnp.concatenate([jnp.where(_head_mask(qp.shape, hh), qp, jnp.zeros_like(qp)) for hh in range(2)], axis=0)
    s_all = _dot_nt(q2, jnp.concatenate(keys, axis=0))
    scores, c0 = [], 0
    for j, kj in enumerate(keys):
        s = s_all[:, c0:c0 + kj.shape[0]]
        c0 += kj.shape[0]
        if biases[0][j] is not None:
            s = s + jnp.concatenate([biases[0][j], biases[1][j]], axis=0)
        scores.append(s)
    m = functools.reduce(jnp.maximum, [jnp.max(s, axis=-1, keepdims=True) for s in scores])
    e = jnp.concatenate([jnp.exp2(s - m).astype(MXU_DTYPE) for s in scores], axis=1)
    vcat = jnp.concatenate(values, axis=0)
    acc = _dot(e, jnp.concatenate([vcat, jnp.ones_like(vcat)], axis=1))
    out = acc[:, :LANES] / acc[:, LANES:]
    return jnp.where(_head_mask((m_rows, LANES), 0), out[:m_rows], out[m_rows:])


ATT_BATCH = 4


def _attn_body(ty_ref, kb_ref, q_ref, k0, k1, k2, v0, v1, v2, kc_ref, vc_ref, bias_ref, o_ref):
    del ty_ref, kb_ref
    for p in range(NA_HEADS // 2):
        cs = slice(p * LANES, (p + 1) * LANES)
        biases = [[bias_ref[2 * p + hh, :, j * ATT_TOK:(j + 1) * ATT_TOK] for j in range(3)] + [None]
                  for hh in range(2)]
        for e in range(q_ref.shape[0]):
            qp = q_ref[e, :, cs] * QK_SCALE
            keys = [k0[e, :, cs], k1[e, :, cs], k2[e, :, cs], kc_ref[e, :, cs]]
            values = [v0[e, :, cs], v1[e, :, cs], v2[e, :, cs], vc_ref[e, :, cs]]
            o_ref[e, :, cs] = _pair_attention(qp, keys, biases, values).astype(ACT_DTYPE)


def _neighborhood_attention(q, k, v, kc, vc, bias, type_of, key_start):
    b, l, _ = q.shape
    lc = kc.shape[1]
    n_blk = l // ATT_TOK
    nb = math.gcd(b, ATT_BATCH)
    tok = lambda off: pl.BlockSpec((nb, ATT_TOK, BRANCH_DIM), lambda r, bb, ty, kb: (bb, kb[r] + off, 0))
    ctx = pl.BlockSpec((nb, lc, BRANCH_DIM), lambda r, bb, ty, kb: (bb, 0, 0))
    grid_spec = pltpu.PrefetchScalarGridSpec(
        num_scalar_prefetch=2,
        grid=(n_blk, b // nb),
        in_specs=[
            pl.BlockSpec((nb, ATT_TOK, BRANCH_DIM), lambda r, bb, ty, kb: (bb, r, 0)),
            tok(0), tok(1), tok(2), tok(0), tok(1), tok(2), ctx, ctx,
            pl.BlockSpec((None, NA_HEADS, ATT_TOK, K_ROWS_PER_BLOCK * GRID_W), lambda r, bb, ty, kb: (ty[r], 0, 0, 0)),
        ],
        out_specs=pl.BlockSpec((nb, ATT_TOK, BRANCH_DIM), lambda r, bb, ty, kb: (bb, r, 0)),
    )
    return pl.pallas_call(
        _attn_body,
        grid_spec=grid_spec,
        out_shape=jax.ShapeDtypeStruct((b, l, BRANCH_DIM), ACT_DTYPE),
        compiler_params=_cparams(("arbitrary", "arbitrary")),
        name="neighborhood_attention",
    )(jnp.asarray(type_of, jnp.int32), jnp.asarray(key_start, jnp.int32), q, k, k, k, v, v, v, kc, vc, bias)


def _ctx_attn_body(q_ref, k_ref, v_ref, o_ref):
    for p in range(NA_HEADS // 2):
        cs = slice(p * LANES, (p + 1) * LANES)
        qp = q_ref[:, cs] * QK_SCALE
        o_ref[:, cs] = _pair_attention(qp, [k_ref[:, cs]], [[None], [None]], [v_ref[:, cs]]).astype(ACT_DTYPE)


def _context_attention(q, k, v):
    b, lc, _ = q.shape
    spec = pl.BlockSpec((None, lc, BRANCH_DIM), lambda bb: (bb, 0, 0))
    return pl.pallas_call(
        _ctx_attn_body,
        grid=(b,),
        in_specs=[spec, spec, spec],
        out_specs=spec,
        out_shape=jax.ShapeDtypeStruct((b, lc, BRANCH_DIM), ACT_DTYPE),
        compiler_params=_cparams(("arbitrary",)),
        name="context_attention",
    )(q, k, v)


HY_CW = 256


@functools.lru_cache(maxsize=None)
def _dft_consts_np(l):
    n = 2 * l
    idx = np.arange(l, dtype=np.int64)
    ang = ((idx[:, None] * idx[None, :]) % n).astype(np.float64) * (2.0 * math.pi / n)
    cs = np.concatenate([np.cos(ang), -np.sin(ang)], axis=1).astype(np.float32)
    sign = np.where(idx % 2 == 0, 1.0, -1.0).astype(np.float32).reshape(l, 1)
    wf = np.where(idx == 0, 1.0 / n, 2.0 / n).astype(np.float32).reshape(l, 1)
    return cs, sign, wf


def _dft_consts(l):
    cs, sign, wf = _dft_consts_np(l)
    return jnp.asarray(cs.astype(MXU_DTYPE)), jnp.asarray(sign), jnp.asarray(wf)


def _filter_consts(l):
    t = np.linspace(0.0, 1.0, l, dtype=np.float32)[:, None]
    n_bands = (HY_EMB - 1) // 2
    bands = np.linspace(1e-4, n_bands - 1, n_bands, dtype=np.float32)[None, :]
    ang = np.float32(2.0 * math.pi) * np.arange(l, dtype=np.float32)[:, None] / np.float32(l) * bands
    z = np.concatenate([t, np.cos(ang), -np.sin(ang)], axis=-1).astype(np.float32)
    z = np.pad(z, ((0, 0), (0, LANES - HY_EMB)))
    max_decay = math.log(HY_DECAY_TARGET) / HY_FAST_DECAY_PCT
    min_decay = math.log(HY_DECAY_TARGET) / HY_SLOW_DECAY_PCT
    deltas = np.abs(np.linspace(min_decay, max_decay, HY_DIM, dtype=np.float32)).reshape(1, HY_DIM)
    return jnp.asarray(z), jnp.asarray(t), jnp.asarray(deltas)


def _filter_body(z_ref, w1_ref, b1_ref, w2_ref, b2_ref, fr_ref, w3a_ref, w3b_ref, t_ref, dl_ref,
                 sg_ref, wf_ref, cs_ref, kc_ref, ks_ref, kn_ref):
    l = z_ref.shape[0]
    fr = fr_ref[...]
    h = jnp.sin(fr * (_dot(z_ref[...], w1_ref[...], precision=HIGHEST) + b1_ref[...]))
    h = jnp.sin(fr * (_dot(h, w2_ref[...], precision=HIGHEST) + b2_ref[...]))
    decay = jnp.exp(-t_ref[...] * dl_ref[...])
    h0 = _dot(h, w3a_ref[...], precision=HIGHEST) * decay
    h1 = _dot(h, w3b_ref[...], precision=HIGHEST) * decay
    row = lax.broadcasted_iota(jnp.int32, h1.shape, 0)
    h1 = jnp.where(row == 0, 0.0, h1)
    den = jnp.sum(jnp.abs(h0), axis=0, keepdims=True) + jnp.sum(jnp.abs(h1), axis=0, keepdims=True)
    hs = (h0 + h1) / den
    hd = (h0 - h1) / den
    kn_ref[...] = jnp.sum(sg_ref[...] * hs, axis=0, keepdims=True) * (1.0 / (2 * l))
    kc_ref[...] = (wf_ref[...] * _dot(cs_ref[:, :l], hs.astype(MXU_DTYPE))).astype(kc_ref.dtype)
    ks_ref[...] = (wf_ref[...] * _dot(cs_ref[:, l:], hd.astype(MXU_DTYPE))).astype(ks_ref.dtype)


def _hyena_filters(l, dft, fconsts, w1, b1, w2, b2, w3, freq):
    cs, sign, wf = dft
    z, t, deltas = fconsts
    cw = HY_CW
    nb = HY_DIM // cw
    w1p = jnp.pad(w1, ((0, LANES - HY_EMB), (0, 0)))
    hid = HY_FILTER_HIDDEN
    small = lambda shape: pl.BlockSpec(shape, lambda n, cb: (0,) * len(shape))
    spec_out = pl.BlockSpec((None, l, cw), lambda n, cb: (n, 0, cb))
    return pl.pallas_call(
        _filter_body,
        grid=(HY_ORDER, nb),
        in_specs=[
            small((l, LANES)), small((LANES, hid)), small((1, hid)), small((hid, hid)), small((1, hid)), small((1, hid)),
            pl.BlockSpec((hid, cw), lambda n, cb: (0, n * 2 * nb + cb)),
            pl.BlockSpec((hid, cw), lambda n, cb: (0, n * 2 * nb + nb + cb)),
            small((l, 1)),
            pl.BlockSpec((1, cw), lambda n, cb: (0, cb)),
            small((l, 1)), small((l, 1)),
            _const_spec((l, 2 * l)),
        ],
        out_specs=[spec_out, spec_out, pl.BlockSpec((None, 1, cw), lambda n, cb: (n, 0, cb))],
        out_shape=[jax.ShapeDtypeStruct((HY_ORDER, l, HY_DIM), ACT_DTYPE)] * 2
        + [jax.ShapeDtypeStruct((HY_ORDER, 1, HY_DIM), F32)],
        compiler_params=_cparams(("arbitrary", "arbitrary")),
        name="hyena_filter_spectrum",
    )(z, w1p, b1.reshape(1, hid), w2, b2.reshape(1, hid), freq.reshape(1, hid), w3, w3, t, deltas, sign, wf, cs)


HY_TM = 512
CONV_HALO = 16


def _conv3_rows(p_ref, w_ref, b_ref, r0, rows):
    l = p_ref.shape[0]
    lo, hi = max(r0 - CONV_HALO, 0), min(r0 + rows + CONV_HALO, l)
    x = p_ref[lo:hi, :].astype(F32)
    n = hi - lo
    prev, nxt = pltpu.roll(x, 1, 0), pltpu.roll(x, n - 1, 0)
    row = lax.broadcasted_iota(jnp.int32, x.shape, 0)
    if lo == 0:
        prev = jnp.where(row == 0, 0.0, prev)
    if hi == l:
        nxt = jnp.where(row == n - 1, 0.0, nxt)
    y = prev * w_ref[0:1, :] + x * w_ref[1:2, :] + nxt * w_ref[2:3, :] + b_ref[...]
    return y[r0 - lo:r0 - lo + rows]


def _hyena_body(pv_ref, p1_ref, p2_ref, wv_ref, w1_ref, w2_ref, bv_ref, b1_ref, b2_ref, skip_ref,
                kc_ref, ks_ref, kn_ref, sg_ref, cs_ref, o_ref, zf_ref, zb_ref, y_ref):
    l = pv_ref.shape[0]
    tm = min(HY_TM, l)
    chunks = [slice(r0, r0 + tm) for r0 in range(0, l, tm)]
    zn = jnp.zeros((1, pv_ref.shape[1]), F32)
    for rs in chunks:
        z = _conv3_rows(pv_ref, wv_ref, bv_ref, rs.start, tm)
        zf_ref[rs] = z
        zb_ref[rs] = z.astype(MXU_DTYPE)
        zn = zn + jnp.sum(sg_ref[rs] * z, axis=0, keepdims=True)
    gates = ((p1_ref, w1_ref, b1_ref), (p2_ref, w2_ref, b2_ref))
    for n, (p_ref, w_ref, b_ref) in enumerate(gates):
        for rs in chunks:
            zc = _dot(cs_ref[rs, :l], zb_ref[...])
            zs = _dot(cs_ref[rs, l:], zb_ref[...])
            kc, ks = kc_ref[n, rs].astype(F32), ks_ref[n, rs].astype(F32)
            y_ref[rs] = (zc * kc - zs * ks).astype(MXU_DTYPE)
            y_ref[l + rs.start:l + rs.stop] = (zc * ks + zs * kc).astype(MXU_DTYPE)
        nyq = zn * kn_ref[n]
        zn = jnp.zeros_like(zn)
        for rs in chunks:
            y = _dot(cs_ref[rs, :], y_ref[...]) + sg_ref[rs] * nyq
            gate = _conv3_rows(p_ref, w_ref, b_ref, rs.start, tm)
            z = gate * (y + zf_ref[rs] * skip_ref[n:n + 1, :])
            if n + 1 < HY_ORDER:
                zf_ref[rs] = z
                zb_ref[rs] = z.astype(MXU_DTYPE)
                zn = zn + jnp.sum(sg_ref[rs] * z, axis=0, keepdims=True)
            else:
                o_ref[rs] = z.astype(ACT_DTYPE)


def _hyena(p, conv_w, conv_b, skip, filters, dft):
    bx, l, _ = p.shape
    cs, sign, _ = dft
    kcs, kss, kns = filters
    cw = HY_CW
    nb = HY_DIM // cw
    pspec = lambda part: pl.BlockSpec((None, l, cw), lambda cb, b: (b, 0, part * nb + cb))
    wspec = lambda part: pl.BlockSpec((3, cw), lambda cb, b: (0, part * nb + cb))
    bspec = lambda part: pl.BlockSpec((1, cw), lambda cb, b: (0, part * nb + cb))
    fspec = pl.BlockSpec((HY_ORDER, l, cw), lambda cb, b: (0, 0, cb), pipeline_mode=pl.Buffered(1))
    return pl.pallas_call(
        _hyena_body,
        grid=(nb, bx),
        in_specs=[
            pspec(0), pspec(1), pspec(2), wspec(0), wspec(1), wspec(2), bspec(0), bspec(1), bspec(2),
            pl.BlockSpec((HY_ORDER, cw), lambda cb, b: (0, cb)),
            fspec, fspec,
            pl.BlockSpec((HY_ORDER, 1, cw), lambda cb, b: (0, 0, cb)),
            _const_spec((l, 1)), _const_spec((l, 2 * l)),
        ],
        out_specs=pl.BlockSpec((None, l, cw), lambda cb, b: (b, 0, cb)),
        out_shape=jax.ShapeDtypeStruct((bx, l, HY_DIM), ACT_DTYPE),
        scratch_shapes=[pltpu.VMEM((l, cw), F32), pltpu.VMEM((l, cw), MXU_DTYPE), pltpu.VMEM((2 * l, cw), MXU_DTYPE)],
        compiler_params=_cparams(("arbitrary", "arbitrary")),
        name="hyena_long_conv",
    )(p, p, p, conv_w, conv_w, conv_w, conv_b.reshape(1, -1), conv_b.reshape(1, -1), conv_b.reshape(1, -1),
      skip, kcs, kss, kns, sign, cs)


def _merge_body(x_ref, a_ref, at_ref, hy_ref, gt_ref, g1_ref, wb_ref, wo_ref, o_ref):
    d = D_MODEL
    m = gt_ref[:, 0:d].astype(F32) * _dot(a_ref[...], wb_ref[0])
    m = m + gt_ref[:, d:2 * d].astype(F32) * _dot(at_ref[...], wb_ref[1])
    m = m + gt_ref[:, 2 * d:3 * d].astype(F32) * _dot(hy_ref[...], wb_ref[2])
    o_ref[...] = x_ref[...] + g1_ref[...] * _dot(m.astype(MXU_DTYPE), wo_ref[...])


def _merge(x, a, attn, hy, gates, mod, mod_row, wb, wo, layer):
    bx, l, d = x.shape
    tm = min(512, l)
    tile = lambda w: pl.BlockSpec((None, tm, w), lambda b, i: (b, i, 0))
    return pl.pallas_call(
        _merge_body,
        grid=(bx, l // tm),
        in_specs=[tile(d), tile(BRANCH_DIM), tile(BRANCH_DIM), tile(BRANCH_DIM), tile(3 * d),
                  _mod_spec(lambda b, i: mod_row(b), 2), _layer_spec(wb, layer), _layer_spec(wo, layer)],
        out_specs=tile(d),
        out_shape=jax.ShapeDtypeStruct((bx, l, d), F32),
        compiler_params=_cparams(("arbitrary", "arbitrary")),
        name="branch_merge",
    )(x, a, attn, hy, gates, mod, wb, wo)


FFN_TM = 512
FFN_TN = 256


def _ffn_body(xp_ref, x_ref, xn_ref, sh_ref, sc_ref, g2_ref, ng_ref, wu_ref, cw_ref, cb_ref, wd_ref, fg_ref,
              o_ref, h_ref, gt_ref, *, final, seq_len):
    tm = x_ref.shape[0]
    halo = CONV_HALO
    n_ext = tm + 2 * halo

    def norm_mod(x):
        ms = jnp.mean(x * x, axis=-1, keepdims=True)
        y = x * lax.rsqrt(ms + NORM_EPS) * ng_ref[...]
        return (y * (1.0 + sc_ref[...]) + sh_ref[...]).astype(MXU_DTYPE)

    h_ref[0:halo] = norm_mod(xp_ref[...])
    h_ref[halo:halo + tm] = norm_mod(x_ref[...])
    h_ref[halo + tm:n_ext] = norm_mod(xn_ref[...])
    row = lax.broadcasted_iota(jnp.int32, (n_ext, 1), 0)
    pos = (pl.program_id(1) * tm + row + (seq_len - halo)) % seq_len
    first, last = pos == 0, pos == seq_len - 1
    for j in range(D_FF // FFN_TN):
        cs = slice(j * FFN_TN, (j + 1) * FFN_TN)
        a = _dot(h_ref[...], wu_ref[:, cs])
        prev = jnp.where(first, 0.0, pltpu.roll(a, 1, 0))
        nxt = jnp.where(last, 0.0, pltpu.roll(a, n_ext - 1, 0))
        a = prev * cw_ref[0:1, cs] + a * cw_ref[1:2, cs] + nxt * cw_ref[2:3, cs] + cb_ref[:, cs]
        gate = _dot(h_ref[halo:halo + tm], wu_ref[:, D_FF + j * FFN_TN:D_FF + (j + 1) * FFN_TN])
        gt_ref[:, cs] = (_gelu(a[halo:halo + tm]) * gate).astype(MXU_DTYPE)
    xn = x_ref[...] + g2_ref[...] * _dot(gt_ref[...], wd_ref[...])
    if final:
        ms = jnp.mean(xn * xn, axis=-1, keepdims=True)
        xn = xn * lax.rsqrt(ms + NORM_EPS) * fg_ref[...]
    o_ref[...] = xn


def _ffn(x, mod, mod_row, norm_g, w_up, conv_w, conv_b, w_down, layer, final_g, final, seq_len):
    bx, l, d = x.shape
    tm = min(FFN_TM, l)
    halo = CONV_HALO
    per, last_blk = tm // halo, l // halo - 1
    tile = pl.BlockSpec((None, tm, d), lambda b, i: (b, i, 0))
    return pl.pallas_call(
        functools.partial(_ffn_body, final=final, seq_len=seq_len),
        grid=(bx, l // tm),
        in_specs=[
            pl.BlockSpec((None, halo, d), lambda b, i: (b, jnp.maximum(i * per - 1, 0), 0)),
            tile,
            pl.BlockSpec((None, halo, d), lambda b, i: (b, jnp.minimum((i + 1) * per, last_blk), 0)),
            _mod_spec(lambda b, i: mod_row(b), 3), _mod_spec(lambda b, i: mod_row(b), 4),
            _mod_spec(lambda b, i: mod_row(b), 5),
            _layer_spec(norm_g, layer), _layer_spec(w_up, layer), _layer_spec(conv_w, layer),
            _layer_spec(conv_b, layer), _layer_spec(w_down, layer), _const_spec((1, d)),
        ],
        out_specs=tile,
        out_shape=jax.ShapeDtypeStruct((bx, l, d), F32),
        scratch_shapes=[pltpu.VMEM((tm + 2 * halo, d), MXU_DTYPE), pltpu.VMEM((tm, D_FF), MXU_DTYPE)],
        compiler_params=_cparams(("arbitrary", "arbitrary")),
        name="conv_ffn",
    )(x, x, x, mod, mod, mod, norm_g, w_up, conv_w, conv_b, w_down, final_g.reshape(1, d))


def _rope_tables(l):
    half = NA_HEAD_DIM // 2
    quarter = half // 2
    inv_freq = np.float32(ROPE_THETA) ** (-np.arange(quarter, dtype=np.float32) * np.float32(2.0) / np.float32(half))
    pos = np.arange(l)
    ang_r = (pos // GRID_W).astype(np.float32)[:, None] * inv_freq[None, :]
    ang_c = (pos % GRID_W).astype(np.float32)[:, None] * inv_freq[None, :]
    cos_h = np.concatenate([np.cos(ang_r), np.cos(ang_r), np.cos(ang_c), np.cos(ang_c)], axis=-1)
    sin_h = np.concatenate([-np.sin(ang_r), np.sin(ang_r), -np.sin(ang_c), np.sin(ang_c)], axis=-1)
    reps = LANES // NA_HEAD_DIM
    return (jnp.asarray(np.tile(cos_h, (1, reps)).astype(np.float32)),
            jnp.asarray(np.tile(sin_h, (1, reps)).astype(np.float32)))


def kernel(x, c, ctx, c_ctx, ada_w, ada_b, norm1_g, norm2_g, w_in, sgu_norm_g, sgu_w, sgu_b, na_rpb, hy_conv_w, hy_conv_b, hy_filt_w1, hy_filt_b1, hy_filt_w2, hy_filt_b2, hy_filt_w3, hy_sin_freq, hy_skip, w_branch, w_out, ffn_w_up, ffn_conv_w, ffn_conv_b, ffn_w_down, final_norm_g):
    b, l, d = x.shape
    lc = ctx.shape[1]
    assert d == D_MODEL and l % ATT_TOK == 0 and l % GRID_W == 0

    n_rows = -(-(b + 1) // 8) * 8
    cc = jnp.concatenate([c, c_ctx[None, :], jnp.zeros((n_rows - b - 1, d), F32)], axis=0)
    mod = _modulation(cc, ada_w, ada_b).reshape(DEPTH * n_rows, 1, N_MOD * d)

    rope_tabs = _rope_tables(l)
    dft_l, fc_l = _dft_consts(l), _filter_consts(l)
    dft_c, fc_c = _dft_consts(lc), _filter_consts(lc)
    patterns, type_of, key_start = _att_plan(l // GRID_W)

    w_in_b, wb_b, wo_b = w_in.astype(MXU_DTYPE), w_branch.astype(MXU_DTYPE), w_out.astype(MXU_DTYPE)
    wup_b, wdn_b = ffn_w_up.astype(MXU_DTYPE), ffn_w_down.astype(MXU_DTYPE)
    n1g, n2g = norm1_g.reshape(DEPTH, 1, d), norm2_g.reshape(DEPTH, 1, d)
    ffn_cb = ffn_conv_b.reshape(DEPTH, 1, D_FF)
    sgu = (sgu_norm_g.reshape(DEPTH, 1, BRANCH_DIM), sgu_w.astype(MXU_DTYPE),
           sgu_b.reshape(DEPTH, SGU_GROUPS, SGU_CHUNK, 1))

    xc = ctx
    for i in range(DEPTH):
        update_ctx = i < DEPTH - 1
        lat_row = lambda bb, i=i: i * n_rows + bb
        ctx_row = lambda bb, i=i: i * n_rows + b
        filt_args = (hy_filt_w1[i], hy_filt_b1[i], hy_filt_w2[i], hy_filt_b2[i], hy_filt_w3[i], hy_sin_freq[i])
        bias = _bias_tables(na_rpb[i], patterns)

        if update_ctx:
            a_c, q_c, k_c, v_c, hy_c, gt_c = _in_proj(xc, mod, ctx_row, n1g, w_in_b, i, FULL_SEGS, None, sgu,
                                                      "context_in_proj")
        else:
            k_c, v_c = _in_proj(xc, mod, ctx_row, n1g, w_in_b, i, KV_SEGS, None, None, "context_kv_proj")

        a, q, k, v, hy, gt = _in_proj(x, mod, lat_row, n1g, w_in_b, i, FULL_SEGS, rope_tabs, sgu, "latent_in_proj")
        attn = _neighborhood_attention(q, k, v, k_c, v_c, bias, type_of, key_start)
        filt = _hyena_filters(l, dft_l, fc_l, *filt_args)
        hyo = _hyena(hy, hy_conv_w[i], hy_conv_b[i], hy_skip[i], filt, dft_l)
        x = _merge(x, a, attn, hyo, gt, mod, lat_row, wb_b, wo_b, i)
        x = _ffn(x, mod, lat_row, n2g, wup_b, ffn_conv_w, ffn_cb, wdn_b, i, final_norm_g,
                 final=not update_ctx, seq_len=l)

        if update_ctx:
            attn_c = _context_attention(q_c, k_c, v_c)
            filt_c = _hyena_filters(lc, dft_c, fc_c, *filt_args)
            hyo_c = _hyena(hy_c, hy_conv_w[i], hy_conv_b[i], hy_skip[i], filt_c, dft_c)
            xc = _merge(xc, a_c, attn_c, hyo_c, gt_c, mod, ctx_row, wb_b, wo_b, i)
            grp = math.gcd(b, max(1, l // lc))
            xc = _ffn(xc.reshape(b // grp, grp * lc, d), mod, ctx_row, n2g, wup_b, ffn_conv_w, ffn_cb, wdn_b, i,
                      final_norm_g, final=False, seq_len=lc).reshape(b, lc, d)
    return x
```

```python
import functools
import math

import jax
import jax.numpy as jnp
import numpy as np
from jax import lax
from jax.experimental import pallas as pl
from jax.experimental.pallas import tpu as pltpu

D_MODEL = 1024
DEPTH = 2
GRID_W = 64
NORM_EPS = 1e-6
N_MOD = 6
BRANCH_DIM = D_MODEL // 2
SGU_GROUP_DIM = 128
SGU_GROUPS = BRANCH_DIM // SGU_GROUP_DIM
SGU_CHUNK = 128
NA_HEAD_DIM = 64
NA_HEADS = BRANCH_DIM // NA_HEAD_DIM
NA_WIN_ROWS = 8
NA_WIN_COLS = 16
ROPE_THETA = 10000.0
NEG_INF = -1e30
HY_DIM = BRANCH_DIM
HY_ORDER = 2
HY_EMB = 33
HY_FILTER_HIDDEN = 64
HY_FAST_DECAY_PCT = 0.3
HY_SLOW_DECAY_PCT = 1.5
HY_DECAY_TARGET = 1e-2
D_FF = 128 * ((8 * D_MODEL + 3 * 128 - 1) // (3 * 128))
OFF_SGU = 0
OFF_QKV = OFF_SGU + 2 * BRANCH_DIM
OFF_HY = OFF_QKV + 3 * BRANCH_DIM
OFF_GATE = OFF_HY + (HY_ORDER + 1) * HY_DIM
IN_COLS = OFF_GATE + 3 * D_MODEL

LANES = 128
MXU_WIDTH = 256
VMEM_LIMIT_BYTES = 56 * 1024 * 1024

MXU_DTYPE = jnp.bfloat16
ACT_DTYPE = jnp.bfloat16
F32 = jnp.float32
HIGHEST = lax.Precision.HIGHEST

Q_ROWS_PER_BLOCK = 4
K_ROWS_PER_BLOCK = 12
ATT_TOK = Q_ROWS_PER_BLOCK * GRID_W


def _cparams(sem):
    return pltpu.CompilerParams(dimension_semantics=sem, vmem_limit_bytes=VMEM_LIMIT_BYTES)


def _const_spec(shape):
    nd = len(shape)
    return pl.BlockSpec(shape, lambda *_: (0,) * nd, pipeline_mode=pl.Buffered(1))


def _layer_spec(arr, layer):
    nd = arr.ndim - 1
    return pl.BlockSpec((None,) + arr.shape[1:], lambda *_: (layer,) + (0,) * nd, pipeline_mode=pl.Buffered(1))


def _dot(a, b, **kw):
    return jnp.dot(a, b, preferred_element_type=F32, **kw)


def _dot_nt(a, b):
    return lax.dot_general(a, b, (((1,), (1,)), ((), ())), preferred_element_type=F32)


def _gelu(x):
    return 0.5 * x * (1.0 + lax.erf(x * (1.0 / math.sqrt(2.0))))


def _mod_body(c_ref, w_ref, b_ref, o_ref):
    c = c_ref[...]
    s = c * jax.nn.sigmoid(c)
    o_ref[...] = _dot(s, w_ref[...], precision=HIGHEST) + b_ref[...]


def _modulation(cc, ada_w, ada_b):
    depth, d, n = ada_w.shape
    r = cc.shape[0]
    tn = 1024
    return pl.pallas_call(
        _mod_body,
        grid=(depth, n // tn),
        in_specs=[
            pl.BlockSpec((r, d), lambda i, j: (0, 0)),
            pl.BlockSpec((None, d, tn), lambda i, j: (i, 0, j)),
            pl.BlockSpec((None, 1, tn), lambda i, j: (i, 0, j)),
        ],
        out_specs=pl.BlockSpec((None, r, tn), lambda i, j: (i, 0, j)),
        out_shape=jax.ShapeDtypeStruct((depth, r, n), F32),
        compiler_params=_cparams(("arbitrary", "arbitrary")),
        name="adaln_modulation",
    )(cc, ada_w, ada_b.reshape(depth, 1, n))


def _mod_spec(row_fn, chunk):
    return pl.BlockSpec((None, 1, D_MODEL), lambda *g: (row_fn(*g), 0, chunk))


def _rope(acc, cos, sin):
    lane = lax.broadcasted_iota(jnp.int32, cos.shape, 1)
    first_half = (lane % 32) < 16
    outs = []
    for c0 in range(0, acc.shape[1], LANES):
        xc = acc[:, c0:c0 + LANES]
        partner = jnp.where(first_half, pltpu.roll(xc, LANES - 16, 1), pltpu.roll(xc, 16, 1))
        outs.append(xc * cos + partner * sin)
    return outs


def _spatial_gating(u, v, g_ref, w_ref, b_ref, o_ref):
    ms = jnp.mean(v * v, axis=-1, keepdims=True)
    vb = (v * lax.rsqrt(ms + NORM_EPS) * g_ref[...]).astype(MXU_DTYPE)
    for n in range(u.shape[0] // SGU_CHUNK):
        rs = slice(n * SGU_CHUNK, (n + 1) * SGU_CHUNK)
        for g in range(SGU_GROUPS):
            cs = slice(g * SGU_GROUP_DIM, (g + 1) * SGU_GROUP_DIM)
            mixed = _dot(w_ref[g], vb[rs, cs]) + b_ref[g]
            o_ref[rs, cs] = (u[rs, cs] * mixed).astype(ACT_DTYPE)


def _in_proj_body(*refs, segs, rope, sgu, n_out):
    x_ref, sh_ref, sc_ref, g_ref, w_ref = refs[:5]
    pos = 5
    if rope:
        cos_ref, sin_ref = refs[pos:pos + 2]
        pos += 2
    if sgu:
        sg_ref, sw_ref, sb_ref = refs[pos:pos + 3]
        pos += 3
    out_refs = refs[pos:pos + n_out]
    x = x_ref[...]
    ms = jnp.mean(x * x, axis=-1, keepdims=True)
    y = x * lax.rsqrt(ms + NORM_EPS) * g_ref[...]
    h = (y * (1.0 + sc_ref[...]) + sh_ref[...]).astype(MXU_DTYPE)
    for (c0, width, kind, oi) in segs:
        if kind == "sgu":
            u = _gelu(_dot(h, w_ref[:, c0:c0 + BRANCH_DIM]))
            v = _gelu(_dot(h, w_ref[:, c0 + BRANCH_DIM:c0 + 2 * BRANCH_DIM]))
            _spatial_gating(u, v, sg_ref, sw_ref, sb_ref, out_refs[oi])
            continue
        step = min(width, 512)
        for cc in range(0, width, step):
            acc = _dot(h, w_ref[:, c0 + cc:c0 + cc + step])
            if kind == "rope" and rope:
                for k, piece in enumerate(_rope(acc, cos_ref[...], sin_ref[...])):
                    out_refs[oi][:, cc + k * LANES:cc + (k + 1) * LANES] = piece.astype(ACT_DTYPE)
            else:
                out_refs[oi][:, cc:cc + step] = acc.astype(ACT_DTYPE)


def _in_proj(x, mod, mod_row, norm_g, w, layer, segs, rope_tabs, sgu, name):
    bx, l, d = x.shape
    tm = min(IN_PROJ_TM, l)
    assert all(s[0] + s[1] <= OFF_GATE for s in segs)
    n_out = max(s[3] for s in segs) + 1
    out_w = lambda s: BRANCH_DIM if s[2] == "sgu" else s[1]
    widths = [sum(out_w(s) for s in segs if s[3] == oi) for oi in range(n_out)]
    rope = rope_tabs is not None
    use_sgu = any(s[2] == "sgu" for s in segs)
    in_specs = [
        pl.BlockSpec((None, tm, d), lambda b, i: (b, i, 0)),
        _mod_spec(lambda b, i: mod_row(b), 0),
        _mod_spec(lambda b, i: mod_row(b), 1),
        _layer_spec(norm_g, layer),
        pl.BlockSpec((None, d, OFF_GATE), lambda *_: (layer, 0, 0), pipeline_mode=pl.Buffered(1)),
    ]
    args = [x, mod, mod, norm_g, w]
    if rope:
        in_specs += [pl.BlockSpec((tm, LANES), lambda b, i: (i, 0))] * 2
        args += list(rope_tabs)
    if use_sgu:
        in_specs += [_layer_spec(p, layer) for p in sgu]
        args += list(sgu)
    return pl.pallas_call(
        functools.partial(_in_proj_body, segs=segs, rope=rope, sgu=use_sgu, n_out=n_out),
        grid=(bx, l // tm),
        in_specs=in_specs,
        out_specs=[pl.BlockSpec((None, tm, wd), lambda b, i: (b, i, 0)) for wd in widths],
        out_shape=[jax.ShapeDtypeStruct((bx, l, wd), ACT_DTYPE) for wd in widths],
        compiler_params=_cparams(("arbitrary", "arbitrary")),
        name=name,
    )(*args)


IN_PROJ_TM = 1024
FULL_SEGS = (
    (OFF_SGU, 2 * BRANCH_DIM, "sgu", 0),
    (OFF_QKV, BRANCH_DIM, "rope", 1),
    (OFF_QKV + BRANCH_DIM, BRANCH_DIM, "rope", 2),
    (OFF_QKV + 2 * BRANCH_DIM, BRANCH_DIM, "plain", 3),
    (OFF_HY, 3 * HY_DIM, "plain", 4),
)
KV_SEGS = ((OFF_QKV + BRANCH_DIM, BRANCH_DIM, "plain", 0), (OFF_QKV + 2 * BRANCH_DIM, BRANCH_DIM, "plain", 1))


def _att_plan(rows):
    kr = min(NA_WIN_ROWS, rows)
    assert rows % Q_ROWS_PER_BLOCK == 0 and rows >= K_ROWS_PER_BLOCK and kr == NA_WIN_ROWS
    patterns, type_of, key_start = [], [], []
    for rb in range(0, rows, Q_ROWS_PER_BLOCK):
        ks = min(max(rb - NA_WIN_ROWS // 2, 0), rows - K_ROWS_PER_BLOCK)
        assert ks % Q_ROWS_PER_BLOCK == 0
        pat = []
        for qi in range(Q_ROWS_PER_BLOCK):
            rq = rb + qi
            r0 = min(max(rq - kr // 2, 0), rows - kr)
            for j in range(K_ROWS_PER_BLOCK):
                rk = ks + j
                pat.append(rk - rq + NA_WIN_ROWS - 1 if r0 <= rk < r0 + kr else None)
        pat = tuple(pat)
        if pat not in patterns:
            patterns.append(pat)
        type_of.append(patterns.index(pat))
        key_start.append(ks // Q_ROWS_PER_BLOCK)
    return patterns, type_of, key_start


def _bias_body(rpb_ref, o_ref, *, patterns):
    n_dr, n_dc = 2 * NA_WIN_ROWS - 1, 2 * NA_WIN_COLS - 1
    h = pl.program_id(0)
    qc = lax.broadcasted_iota(jnp.int32, (GRID_W, GRID_W), 0)
    kc = lax.broadcasted_iota(jnp.int32, (GRID_W, GRID_W), 1)
    cstart = jnp.clip(qc - NA_WIN_COLS // 2, 0, GRID_W - NA_WIN_COLS)
    col_ok = (kc >= cstart) & (kc < cstart + NA_WIN_COLS)
    dc = jnp.clip(kc - qc + NA_WIN_COLS - 1, 0, n_dc - 1)
    neg = jnp.full((GRID_W, GRID_W), NEG_INF, F32)
    tiles = []
    for dr in range(n_dr):
        t = jnp.zeros((GRID_W, GRID_W), F32)
        for d in range(n_dc):
            t = jnp.where(dc == d, rpb_ref[(h * n_dr + dr) * n_dc + d], t)
        tiles.append(jnp.where(col_ok, t * LOG2E, neg))
    for ty, pat in enumerate(patterns):
        for qi in range(Q_ROWS_PER_BLOCK):
            for j in range(K_ROWS_PER_BLOCK):
                dr = pat[qi * K_ROWS_PER_BLOCK + j]
                o_ref[ty, qi * GRID_W:(qi + 1) * GRID_W, j * GRID_W:(j + 1) * GRID_W] = neg if dr is None else tiles[dr]


def _bias_tables(rpb, patterns):
    nt = len(patterns)
    return pl.pallas_call(
        functools.partial(_bias_body, patterns=patterns),
        grid=(NA_HEADS,),
        in_specs=[pl.BlockSpec(memory_space=pltpu.SMEM)],
        out_specs=pl.BlockSpec((nt, None, ATT_TOK, K_ROWS_PER_BLOCK * GRID_W), lambda h: (0, h, 0, 0)),
        out_shape=jax.ShapeDtypeStruct((nt, NA_HEADS, ATT_TOK, K_ROWS_PER_BLOCK * GRID_W), F32),
        compiler_params=_cparams(("arbitrary",)),
        name="attention_bias_tables",
    )(rpb.reshape(-1))


def _head_mask(shape, hh):
    lane = lax.broadcasted_iota(jnp.int32, shape, 1)
    return (lane >= hh * NA_HEAD_DIM) & (lane < (hh + 1) * NA_HEAD_DIM)


LOG2E = math.log2(math.e)
QK_SCALE = NA_HEAD_DIM ** -0.5 * LOG2E


def _pair_attention(qp, keys, biases, values):
    m_rows = qp.shape[0]
    q2 = jnp.concatenate([jnp.where(_head_mask(qp.shape, hh), qp, jnp.zeros_like(qp)) for hh in range(2)], axis=0)
    s_all = _dot_nt(q2, jnp.concatenate(keys, axis=0))
    scores, c0 = [], 0
    for j, kj in enumerate(keys):
        s = s_all[:, c0:c0 + kj.shape[0]]
        c0 += kj.shape[0]
        if biases[0][j] is not None:
            s = s + jnp.concatenate([biases[0][j], biases[1][j]], axis=0)
        scores.append(s)
    m = functools.reduce(jnp.maximum, [jnp.max(s, axis=-1, keepdims=True) for s in scores])
    e = jnp.concatenate([jnp.exp2(s - m).astype(MXU_DTYPE) for s in scores], axis=1)
    vcat = jnp.concatenate(values, axis=0)
    acc = _dot(e, jnp.concatenate([vcat, jnp.ones_like(vcat)], axis=1))
    out = acc[:, :LANES] / acc[:, LANES:]
    return jnp.where(_head_mask((m_rows, LANES), 0), out[:m_rows], out[m_rows:])


ATT_BATCH = 4


def _attn_body(ty_ref, kb_ref, q_ref, k0, k1, k2, v0, v1, v2, kc_ref, vc_ref, bias_ref, o_ref):
    del ty_ref, kb_ref
    for p in range(NA_HEADS // 2):
        cs = slice(p * LANES, (p + 1) * LANES)
        biases = [[bias_ref[2 * p + hh, :, j * ATT_TOK:(j + 1) * ATT_TOK] for j in range(3)] + [None]
                  for hh in range(2)]
        for e in range(q_ref.shape[0]):
            qp = q_ref[e, :, cs] * QK_SCALE
            keys = [k0[e, :, cs], k1[e, :, cs], k2[e, :, cs], kc_ref[e, :, cs]]
            values = [v0[e, :, cs], v1[e, :, cs], v2[e, :, cs], vc_ref[e, :, cs]]
            o_ref[e, :, cs] = _pair_attention(qp, keys, biases, values).astype(ACT_DTYPE)


def _neighborhood_attention(q, k, v, kc, vc, bias, type_of, key_start):
    b, l, _ = q.shape
    lc = kc.shape[1]
    n_blk = l // ATT_TOK
    nb = math.gcd(b, ATT_BATCH)
    tok = lambda off: pl.BlockSpec((nb, ATT_TOK, BRANCH_DIM), lambda r, bb, ty, kb: (bb, kb[r] + off, 0))
    ctx = pl.BlockSpec((nb, lc, BRANCH_DIM), lambda r, bb, ty, kb: (bb, 0, 0))
    grid_spec = pltpu.PrefetchScalarGridSpec(
        num_scalar_prefetch=2,
        grid=(n_blk, b // nb),
        in_specs=[
            pl.BlockSpec((nb, ATT_TOK, BRANCH_DIM), lambda r, bb, ty, kb: (bb, r, 0)),
            tok(0), tok(1), tok(2), tok(0), tok(1), tok(2), ctx, ctx,
            pl.BlockSpec((None, NA_HEADS, ATT_TOK, K_ROWS_PER_BLOCK * GRID_W), lambda r, bb, ty, kb: (ty[r], 0, 0, 0)),
        ],
        out_specs=pl.BlockSpec((nb, ATT_TOK, BRANCH_DIM), lambda r, bb, ty, kb: (bb, r, 0)),
    )
    return pl.pallas_call(
        _attn_body,
        grid_spec=grid_spec,
        out_shape=jax.ShapeDtypeStruct((b, l, BRANCH_DIM), ACT_DTYPE),
        compiler_params=_cparams(("arbitrary", "arbitrary")),
        name="neighborhood_attention",
    )(jnp.asarray(type_of, jnp.int32), jnp.asarray(key_start, jnp.int32), q, k, k, k, v, v, v, kc, vc, bias)


def _ctx_attn_body(q_ref, k_ref, v_ref, o_ref):
    for p in range(NA_HEADS // 2):
        cs = slice(p * LANES, (p + 1) * LANES)
        qp = q_ref[:, cs] * QK_SCALE
        o_ref[:, cs] = _pair_attention(qp, [k_ref[:, cs]], [[None], [None]], [v_ref[:, cs]]).astype(ACT_DTYPE)


def _context_attention(q, k, v):
    b, lc, _ = q.shape
    spec = pl.BlockSpec((None, lc, BRANCH_DIM), lambda bb: (bb, 0, 0))
    return pl.pallas_call(
        _ctx_attn_body,
        grid=(b,),
        in_specs=[spec, spec, spec],
        out_specs=spec,
        out_shape=jax.ShapeDtypeStruct((b, lc, BRANCH_DIM), ACT_DTYPE),
        compiler_params=_cparams(("arbitrary",)),
        name="context_attention",
    )(q, k, v)


HY_CW = 256


@functools.lru_cache(maxsize=None)
def _dft_consts_np(l):
    n = 2 * l
    idx = np.arange(l, dtype=np.int64)
    ang = ((idx[:, None] * idx[None, :]) % n).astype(np.float64) * (2.0 * math.pi / n)
    cs = np.concatenate([np.cos(ang), -np.sin(ang)], axis=1).astype(np.float32)
    sign = np.where(idx % 2 == 0, 1.0, -1.0).astype(np.float32).reshape(l, 1)
    wf = np.where(idx == 0, 1.0 / n, 2.0 / n).astype(np.float32).reshape(l, 1)
    return cs, sign, wf


def _dft_consts(l):
    cs, sign, wf = _dft_consts_np(l)
    return jnp.asarray(cs.astype(MXU_DTYPE)), jnp.asarray(sign), jnp.asarray(wf)


def _filter_consts(l):
    t = np.linspace(0.0, 1.0, l, dtype=np.float32)[:, None]
    n_bands = (HY_EMB - 1) // 2
    bands = np.linspace(1e-4, n_bands - 1, n_bands, dtype=np.float32)[None, :]
    ang = np.float32(2.0 * math.pi) * np.arange(l, dtype=np.float32)[:, None] / np.float32(l) * bands
    z = np.concatenate([t, np.cos(ang), -np.sin(ang)], axis=-1).astype(np.float32)
    z = np.pad(z, ((0, 0), (0, LANES - HY_EMB)))
    max_decay = math.log(HY_DECAY_TARGET) / HY_FAST_DECAY_PCT
    min_decay = math.log(HY_DECAY_TARGET) / HY_SLOW_DECAY_PCT
    deltas = np.abs(np.linspace(min_decay, max_decay, HY_DIM, dtype=np.float32)).reshape(1, HY_DIM)
    return jnp.asarray(z), jnp.asarray(t), jnp.asarray(deltas)


def _filter_body(z_ref, w1_ref, b1_ref, w2_ref, b2_ref, fr_ref, w3a_ref, w3b_ref, t_ref, dl_ref,
                 sg_ref, wf_ref, cs_ref, kc_ref, ks_ref, kn_ref):
    l = z_ref.shape[0]
    fr = fr_ref[...]
    h = jnp.sin(fr * (_dot(z_ref[...], w1_ref[...], precision=HIGHEST) + b1_ref[...]))
    h = jnp.sin(fr * (_dot(h, w2_ref[...], precision=HIGHEST) + b2_ref[...]))
    decay = jnp.exp(-t_ref[...] * dl_ref[...])
    h0 = _dot(h, w3a_ref[...], precision=HIGHEST) * decay
    h1 = _dot(h, w3b_ref[...], precision=HIGHEST) * decay
    row = lax.broadcasted_iota(jnp.int32, h1.shape, 0)
    h1 = jnp.where(row == 0, 0.0, h1)
    den = jnp.sum(jnp.abs(h0), axis=0, keepdims=True) + jnp.sum(jnp.abs(h1), axis=0, keepdims=True)
    hs = (h0 + h1) / den
    hd = (h0 - h1) / den
    kn_ref[...] = jnp.sum(sg_ref[...] * hs, axis=0, keepdims=True) * (1.0 / (2 * l))
    kc_ref[...] = (wf_ref[...] * _dot(cs_ref[:, :l], hs.astype(MXU_DTYPE))).astype(kc_ref.dtype)
    ks_ref[...] = (wf_ref[...] * _dot(cs_ref[:, l:], hd.astype(MXU_DTYPE))).astype(ks_ref.dtype)


def _hyena_filters(l, dft, fconsts, w1, b1, w2, b2, w3, freq):
    cs, sign, wf = dft
    z, t, deltas = fconsts
    cw = HY_CW
    nb = HY_DIM // cw
    w1p = jnp.pad(w1, ((0, LANES - HY_EMB), (0, 0)))
    hid = HY_FILTER_HIDDEN
    small = lambda shape: pl.BlockSpec(shape, lambda n, cb: (0,) * len(shape))
    spec_out = pl.BlockSpec((None, l, cw), lambda n, cb: (n, 0, cb))
    return pl.pallas_call(
        _filter_body,
        grid=(HY_ORDER, nb),
        in_specs=[
            small((l, LANES)), small((LANES, hid)), small((1, hid)), small((hid, hid)), small((1, hid)), small((1, hid)),
            pl.BlockSpec((hid, cw), lambda n, cb: (0, n * 2 * nb + cb)),
            pl.BlockSpec((hid, cw), lambda n, cb: (0, n * 2 * nb + nb + cb)),
            small((l, 1)),
            pl.BlockSpec((1, cw), lambda n, cb: (0, cb)),
            small((l, 1)), small((l, 1)),
            _const_spec((l, 2 * l)),
        ],
        out_specs=[spec_out, spec_out, pl.BlockSpec((None, 1, cw), lambda n, cb: (n, 0, cb))],
        out_shape=[jax.ShapeDtypeStruct((HY_ORDER, l, HY_DIM), ACT_DTYPE)] * 2
        + [jax.ShapeDtypeStruct((HY_ORDER, 1, HY_DIM), F32)],
        compiler_params=_cparams(("arbitrary", "arbitrary")),
        name="hyena_filter_spectrum",
    )(z, w1p, b1.reshape(1, hid), w2, b2.reshape(1, hid), freq.reshape(1, hid), w3, w3, t, deltas, sign, wf, cs)


HY_TM = 512
CONV_HALO = 16


def _conv3_rows(p_ref, w_ref, b_ref, r0, rows):
    l = p_ref.shape[0]
    lo, hi = max(r0 - CONV_HALO, 0), min(r0 + rows + CONV_HALO, l)
    x = p_ref[lo:hi, :].astype(F32)
    n = hi - lo
    prev, nxt = pltpu.roll(x, 1, 0), pltpu.roll(x, n - 1, 0)
    row = lax.broadcasted_iota(jnp.int32, x.shape, 0)
    if lo == 0:
        prev = jnp.where(row == 0, 0.0, prev)
    if hi == l:
        nxt = jnp.where(row == n - 1, 0.0, nxt)
    y = prev * w_ref[0:1, :] + x * w_ref[1:2, :] + nxt * w_ref[2:3, :] + b_ref[...]
    return y[r0 - lo:r0 - lo + rows]


def _hyena_body(pv_ref, p1_ref, p2_ref, wv_ref, w1_ref, w2_ref, bv_ref, b1_ref, b2_ref, skip_ref,
                kc_ref, ks_ref, kn_ref, sg_ref, cs_ref, o_ref, zf_ref, zb_ref, y_ref):
    l = pv_ref.shape[0]
    tm = min(HY_TM, l)
    chunks = [slice(r0, r0 + tm) for r0 in range(0, l, tm)]
    zn = jnp.zeros((1, pv_ref.shape[1]), F32)
    for rs in chunks:
        z = _conv3_rows(pv_ref, wv_ref, bv_ref, rs.start, tm)
        zf_ref[rs] = z
        zb_ref[rs] = z.astype(MXU_DTYPE)
        zn = zn + jnp.sum(sg_ref[rs] * z, axis=0, keepdims=True)
    gates = ((p1_ref, w1_ref, b1_ref), (p2_ref, w2_ref, b2_ref))
    for n, (p_ref, w_ref, b_ref) in enumerate(gates):
        for rs in chunks:
            zc = _dot(cs_ref[rs, :l], zb_ref[...])
            zs = _dot(cs_ref[rs, l:], zb_ref[...])
            kc, ks = kc_ref[n, rs].astype(F32), ks_ref[n, rs].astype(F32)
            y_ref[rs] = (zc * kc - zs * ks).astype(MXU_DTYPE)
            y_ref[l + rs.start:l + rs.stop] = (zc * ks + zs * kc).astype(MXU_DTYPE)
        nyq = zn * kn_ref[n]
        zn = jnp.zeros_like(zn)
        for rs in chunks:
            y = _dot(cs_ref[rs, :], y_ref[...]) + sg_ref[rs] * nyq
            gate = _conv3_rows(p_ref, w_ref, b_ref, rs.start, tm)
            z = gate * (y + zf_ref[rs] * skip_ref[n:n + 1, :])
            if n + 1 < HY_ORDER:
                zf_ref[rs] = z
                zb_ref[rs] = z.astype(MXU_DTYPE)
                zn = zn + jnp.sum(sg_ref[rs] * z, axis=0, keepdims=True)
            else:
                o_ref[rs] = z.astype(ACT_DTYPE)


def _hyena(p, conv_w, conv_b, skip, filters, dft):
    bx, l, _ = p.shape
    cs, sign, _ = dft
    kcs, kss, kns = filters
    cw = HY_CW
    nb = HY_DIM // cw
    pspec = lambda part: pl.BlockSpec((None, l, cw), lambda cb, b: (b, 0, part * nb + cb))
    wspec = lambda part: pl.BlockSpec((3, cw), lambda cb, b: (0, part * nb + cb))
    bspec = lambda part: pl.BlockSpec((1, cw), lambda cb, b: (0, part * nb + cb))
    fspec = pl.BlockSpec((HY_ORDER, l, cw), lambda cb, b: (0, 0, cb), pipeline_mode=pl.Buffered(1))
    return pl.pallas_call(
        _hyena_body,
        grid=(nb, bx),
        in_specs=[
            pspec(0), pspec(1), pspec(2), wspec(0), wspec(1), wspec(2), bspec(0), bspec(1), bspec(2),
            pl.BlockSpec((HY_ORDER, cw), lambda cb, b: (0, cb)),
            fspec, fspec,
            pl.BlockSpec((HY_ORDER, 1, cw), lambda cb, b: (0, 0, cb)),
            _const_spec((l, 1)), _const_spec((l, 2 * l)),
        ],
        out_specs=pl.BlockSpec((None, l, cw), lambda cb, b: (b, 0, cb)),
        out_shape=jax.ShapeDtypeStruct((bx, l, HY_DIM), ACT_DTYPE),
        scratch_shapes=[pltpu.VMEM((l, cw), F32), pltpu.VMEM((l, cw), MXU_DTYPE), pltpu.VMEM((2 * l, cw), MXU_DTYPE)],
        compiler_params=_cparams(("arbitrary", "arbitrary")),
        name="hyena_long_conv",
    )(p, p, p, conv_w, conv_w, conv_w, conv_b.reshape(1, -1), conv_b.reshape(1, -1), conv_b.reshape(1, -1),
      skip, kcs, kss, kns, sign, cs)


MERGE_TM = 1024


def _merge_body(x_ref, a_ref, at_ref, hy_ref, sh_ref, sc_ref, g1_ref, ng_ref, wg0_ref, wg1_ref, wg2_ref,
                wb_ref, wo_ref, o_ref):
    x = x_ref[...]
    ms = jnp.mean(x * x, axis=-1, keepdims=True)
    y = x * lax.rsqrt(ms + NORM_EPS) * ng_ref[...]
    h = (y * (1.0 + sc_ref[...]) + sh_ref[...]).astype(MXU_DTYPE)
    m = None
    for j, (br_ref, wg_ref) in enumerate(((a_ref, wg0_ref), (at_ref, wg1_ref), (hy_ref, wg2_ref))):
        term = jax.nn.sigmoid(_dot(h, wg_ref[...])) * _dot(br_ref[...], wb_ref[j])
        m = term if m is None else m + term
    o_ref[...] = x + g1_ref[...] * _dot(m.astype(MXU_DTYPE), wo_ref[...])


def _merge(x, a, attn, hy, mod, mod_row, norm_g, w_in, wb, wo, layer):
    bx, l, d = x.shape
    tm = min(MERGE_TM, l)
    tile = lambda w: pl.BlockSpec((None, tm, w), lambda b, i: (b, i, 0))
    gate_w = lambda j: pl.BlockSpec((None, d, d), lambda *_: (layer, 0, OFF_GATE // d + j), pipeline_mode=pl.Buffered(1))
    return pl.pallas_call(
        _merge_body,
        grid=(bx, l // tm),
        in_specs=[tile(d), tile(BRANCH_DIM), tile(BRANCH_DIM), tile(BRANCH_DIM),
                  _mod_spec(lambda b, i: mod_row(b), 0), _mod_spec(lambda b, i: mod_row(b), 1),
                  _mod_spec(lambda b, i: mod_row(b), 2), _layer_spec(norm_g, layer),
                  gate_w(0), gate_w(1), gate_w(2), _layer_spec(wb, layer), _layer_spec(wo, layer)],
        out_specs=tile(d),
        out_shape=jax.ShapeDtypeStruct((bx, l, d), F32),
        compiler_params=_cparams(("arbitrary", "arbitrary")),
        name="branch_merge",
    )(x, a, attn, hy, mod, mod, mod, norm_g, w_in, w_in, w_in, wb, wo)


FFN_TM = 1024
FFN_TN = 256


def _ffn_body(xp_ref, x_ref, xn_ref, sh_ref, sc_ref, g2_ref, ng_ref, wu_ref, cw_ref, cb_ref, wd_ref, fg_ref,
              o_ref, h_ref, gt_ref, *, final, seq_len):
    tm = x_ref.shape[0]
    halo = CONV_HALO
    n_ext = tm + 2 * halo

    def norm_mod(x):
        ms = jnp.mean(x * x, axis=-1, keepdims=True)
        y = x * lax.rsqrt(ms + NORM_EPS) * ng_ref[...]
        return (y * (1.0 + sc_ref[...]) + sh_ref[...]).astype(MXU_DTYPE)

    h_ref[0:halo] = norm_mod(xp_ref[...])
    h_ref[halo:halo + tm] = norm_mod(x_ref[...])
    h_ref[halo + tm:n_ext] = norm_mod(xn_ref[...])
    row = lax.broadcasted_iota(jnp.int32, (n_ext, 1), 0)
    pos = (pl.program_id(1) * tm + row + (seq_len - halo)) % seq_len
    first, last = pos == 0, pos == seq_len - 1
    for j in range(D_FF // FFN_TN):
        cs = slice(j * FFN_TN, (j + 1) * FFN_TN)
        a = _dot(h_ref[...], wu_ref[:, cs])
        prev = jnp.where(first, 0.0, pltpu.roll(a, 1, 0))
        nxt = jnp.where(last, 0.0, pltpu.roll(a, n_ext - 1, 0))
        a = prev * cw_ref[0:1, cs] + a * cw_ref[1:2, cs] + nxt * cw_ref[2:3, cs] + cb_ref[:, cs]
        gate = _dot(h_ref[halo:halo + tm], wu_ref[:, D_FF + j * FFN_TN:D_FF + (j + 1) * FFN_TN])
        gt_ref[:, cs] = (_gelu(a[halo:halo + tm]) * gate).astype(MXU_DTYPE)
    xn = x_ref[...] + g2_ref[...] * _dot(gt_ref[...], wd_ref[...])
    if final:
        ms = jnp.mean(xn * xn, axis=-1, keepdims=True)
        xn = xn * lax.rsqrt(ms + NORM_EPS) * fg_ref[...]
    o_ref[...] = xn


def _ffn(x, mod, mod_row, norm_g, w_up, conv_w, conv_b, w_down, layer, final_g, final, seq_len):
    bx, l, d = x.shape
    tm = min(FFN_TM, l)
    halo = CONV_HALO
    per, last_blk = tm // halo, l // halo - 1
    tile = pl.BlockSpec((None, tm, d), lambda b, i: (b, i, 0))
    return pl.pallas_call(
        functools.partial(_ffn_body, final=final, seq_len=seq_len),
        grid=(bx, l // tm),
        in_specs=[
            pl.BlockSpec((None, halo, d), lambda b, i: (b, jnp.maximum(i * per - 1, 0), 0)),
            tile,
            pl.BlockSpec((None, halo, d), lambda b, i: (b, jnp.minimum((i + 1) * per, last_blk), 0)),
            _mod_spec(lambda b, i: mod_row(b), 3), _mod_spec(lambda b, i: mod_row(b), 4),
            _mod_spec(lambda b, i: mod_row(b), 5),
            _layer_spec(norm_g, layer), _layer_spec(w_up, layer), _layer_spec(conv_w, layer),
            _layer_spec(conv_b, layer), _layer_spec(w_down, layer), _const_spec((1, d)),
        ],
        out_specs=tile,
        out_shape=jax.ShapeDtypeStruct((bx, l, d), F32),
        scratch_shapes=[pltpu.VMEM((tm + 2 * halo, d), MXU_DTYPE), pltpu.VMEM((tm, D_FF), MXU_DTYPE)],
        compiler_params=_cparams(("arbitrary", "arbitrary")),
        name="conv_ffn",
    )(x, x, x, mod, mod, mod, norm_g, w_up, conv_w, conv_b, w_down, final_g.reshape(1, d))


def _rope_tables(l):
    half = NA_HEAD_DIM // 2
    quarter = half // 2
    inv_freq = np.float32(ROPE_THETA) ** (-np.arange(quarter, dtype=np.float32) * np.float32(2.0) / np.float32(half))
    pos = np.arange(l)
    ang_r = (pos // GRID_W).astype(np.float32)[:, None] * inv_freq[None, :]
    ang_c = (pos % GRID_W).astype(np.float32)[:, None] * inv_freq[None, :]
    cos_h = np.concatenate([np.cos(ang_r), np.cos(ang_r), np.cos(ang_c), np.cos(ang_c)], axis=-1)
    sin_h = np.concatenate([-np.sin(ang_r), np.sin(ang_r), -np.sin(ang_c), np.sin(ang_c)], axis=-1)
    reps = LANES // NA_HEAD_DIM
    return (jnp.asarray(np.tile(cos_h, (1, reps)).astype(np.float32)),
            jnp.asarray(np.tile(sin_h, (1, reps)).astype(np.float32)))


def kernel(x, c, ctx, c_ctx, ada_w, ada_b, norm1_g, norm2_g, w_in, sgu_norm_g, sgu_w, sgu_b, na_rpb, hy_conv_w, hy_conv_b, hy_filt_w1, hy_filt_b1, hy_filt_w2, hy_filt_b2, hy_filt_w3, hy_sin_freq, hy_skip, w_branch, w_out, ffn_w_up, ffn_conv_w, ffn_conv_b, ffn_w_down, final_norm_g):
    b, l, d = x.shape
    lc = ctx.shape[1]
    assert d == D_MODEL and l % ATT_TOK == 0 and l % GRID_W == 0

    n_rows = -(-(b + 1) // 8) * 8
    cc = jnp.concatenate([c, c_ctx[None, :], jnp.zeros((n_rows - b - 1, d), F32)], axis=0)
    mod = _modulation(cc, ada_w, ada_b).reshape(DEPTH * n_rows, 1, N_MOD * d)

    rope_tabs = _rope_tables(l)
    dft_l, fc_l = _dft_consts(l), _filter_consts(l)
    dft_c, fc_c = _dft_consts(lc), _filter_consts(lc)
    patterns, type_of, key_start = _att_plan(l // GRID_W)

    w_in_b, wb_b, wo_b = w_in.astype(MXU_DTYPE), w_branch.astype(MXU_DTYPE), w_out.astype(MXU_DTYPE)
    wup_b, wdn_b = ffn_w_up.astype(MXU_DTYPE), ffn_w_down.astype(MXU_DTYPE)
    n1g, n2g = norm1_g.reshape(DEPTH, 1, d), norm2_g.reshape(DEPTH, 1, d)
    ffn_cb = ffn_conv_b.reshape(DEPTH, 1, D_FF)
    sgu = (sgu_norm_g.reshape(DEPTH, 1, BRANCH_DIM), sgu_w.astype(MXU_DTYPE),
           sgu_b.reshape(DEPTH, SGU_GROUPS, SGU_CHUNK, 1))

    xc = ctx
    for i in range(DEPTH):
        update_ctx = i < DEPTH - 1
        lat_row = lambda bb, i=i: i * n_rows + bb
        ctx_row = lambda bb, i=i: i * n_rows + b
        filt_args = (hy_filt_w1[i], hy_filt_b1[i], hy_filt_w2[i], hy_filt_b2[i], hy_filt_w3[i], hy_sin_freq[i])
        bias = _bias_tables(na_rpb[i], patterns)

        if update_ctx:
            a_c, q_c, k_c, v_c, hy_c = _in_proj(xc, mod, ctx_row, n1g, w_in_b, i, FULL_SEGS, None, sgu,
                                                "context_in_proj")
        else:
            k_c, v_c = _in_proj(xc, mod, ctx_row, n1g, w_in_b, i, KV_SEGS, None, None, "context_kv_proj")

        a, q, k, v, hy = _in_proj(x, mod, lat_row, n1g, w_in_b, i, FULL_SEGS, rope_tabs, sgu, "latent_in_proj")
        attn = _neighborhood_attention(q, k, v, k_c, v_c, bias, type_of, key_start)
        filt = _hyena_filters(l, dft_l, fc_l, *filt_args)
        hyo = _hyena(hy, hy_conv_w[i], hy_conv_b[i], hy_skip[i], filt, dft_l)
        x = _merge(x, a, attn, hyo, mod, lat_row, n1g, w_in_b, wb_b, wo_b, i)
        x = _ffn(x, mod, lat_row, n2g, wup_b, ffn_conv_w, ffn_cb, wdn_b, i, final_norm_g,
                 final=not update_ctx, seq_len=l)

        if update_ctx:
            attn_c = _context_attention(q_c, k_c, v_c)
            filt_c = _hyena_filters(lc, dft_c, fc_c, *filt_args)
            hyo_c = _hyena(hy_c, hy_conv_w[i], hy_conv_b[i], hy_skip[i], filt_c, dft_c)
            xc = _merge(xc, a_c, attn_c, hyo_c, mod, ctx_row, n1g, w_in_b, wb_b, wo_b, i)
            grp = math.gcd(b, max(1, l // lc))
            xc = _ffn(xc.reshape(b // grp, grp * lc, d), mod, ctx_row, n2g, wup_b, ffn_conv_w, ffn_cb, wdn_b, i,
                      final_norm_g, final=False, seq_len=lc).reshape(b, lc, d)
    return x
```

```python
import functools
import math

import jax
import jax.numpy as jnp
import numpy as np
from jax import lax
from jax.experimental import pallas as pl
from jax.experimental.pallas import tpu as pltpu

D_MODEL = 1024
DEPTH = 2
GRID_W = 64
NORM_EPS = 1e-6
N_MOD = 6
BRANCH_DIM = D_MODEL // 2
SGU_GROUP_DIM = 128
SGU_GROUPS = BRANCH_DIM // SGU_GROUP_DIM
SGU_CHUNK = 128
NA_HEAD_DIM = 64
NA_HEADS = BRANCH_DIM // NA_HEAD_DIM
NA_WIN_ROWS = 8
NA_WIN_COLS = 16
ROPE_THETA = 10000.0
NEG_INF = -1e30
HY_DIM = BRANCH_DIM
HY_ORDER = 2
HY_EMB = 33
HY_FILTER_HIDDEN = 64
HY_FAST_DECAY_PCT = 0.3
HY_SLOW_DECAY_PCT = 1.5
HY_DECAY_TARGET = 1e-2
D_FF = 128 * ((8 * D_MODEL + 3 * 128 - 1) // (3 * 128))
OFF_SGU = 0
OFF_QKV = OFF_SGU + 2 * BRANCH_DIM
OFF_HY = OFF_QKV + 3 * BRANCH_DIM
OFF_GATE = OFF_HY + (HY_ORDER + 1) * HY_DIM
IN_COLS = OFF_GATE + 3 * D_MODEL

LANES = 128
MXU_WIDTH = 256
VMEM_LIMIT_BYTES = 56 * 1024 * 1024

MXU_DTYPE = jnp.bfloat16
ACT_DTYPE = jnp.bfloat16
F32 = jnp.float32
HIGHEST = lax.Precision.HIGHEST

Q_ROWS_PER_BLOCK = 4
K_ROWS_PER_BLOCK = 12
ATT_TOK = Q_ROWS_PER_BLOCK * GRID_W


def _cparams(sem):
    return pltpu.CompilerParams(dimension_semantics=sem, vmem_limit_bytes=VMEM_LIMIT_BYTES)


def _const_spec(shape):
    nd = len(shape)
    return pl.BlockSpec(shape, lambda *_: (0,) * nd, pipeline_mode=pl.Buffered(1))


def _layer_spec(arr, layer):
    nd = arr.ndim - 1
    return pl.BlockSpec((None,) + arr.shape[1:], lambda *_: (layer,) + (0,) * nd, pipeline_mode=pl.Buffered(1))


def _dot(a, b, **kw):
    return jnp.dot(a, b, preferred_element_type=F32, **kw)


def _dot_nt(a, b):
    return lax.dot_general(a, b, (((1,), (1,)), ((), ())), preferred_element_type=F32)


def _gelu(x):
    return 0.5 * x * (1.0 + lax.erf(x * (1.0 / math.sqrt(2.0))))


def _mod_body(c_ref, w_ref, b_ref, o_ref):
    c = c_ref[...]
    s = c * jax.nn.sigmoid(c)
    o_ref[...] = _dot(s, w_ref[...], precision=HIGHEST) + b_ref[...]


def _modulation(cc, ada_w, ada_b):
    depth, d, n = ada_w.shape
    r = cc.shape[0]
    tn = 1024
    return pl.pallas_call(
        _mod_body,
        grid=(depth, n // tn),
        in_specs=[
            pl.BlockSpec((r, d), lambda i, j: (0, 0)),
            pl.BlockSpec((None, d, tn), lambda i, j: (i, 0, j)),
            pl.BlockSpec((None, 1, tn), lambda i, j: (i, 0, j)),
        ],
        out_specs=pl.BlockSpec((None, r, tn), lambda i, j: (i, 0, j)),
        out_shape=jax.ShapeDtypeStruct((depth, r, n), F32),
        compiler_params=_cparams(("arbitrary", "arbitrary")),
        name="adaln_modulation",
    )(cc, ada_w, ada_b.reshape(depth, 1, n))


def _mod_spec(row_fn, chunk):
    return pl.BlockSpec((None, 1, D_MODEL), lambda *g: (row_fn(*g), 0, chunk))


def _rope(acc, cos, sin):
    lane = lax.broadcasted_iota(jnp.int32, cos.shape, 1)
    first_half = (lane % 32) < 16
    outs = []
    for c0 in range(0, acc.shape[1], LANES):
        xc = acc[:, c0:c0 + LANES]
        partner = jnp.where(first_half, pltpu.roll(xc, LANES - 16, 1), pltpu.roll(xc, 16, 1))
        outs.append(xc * cos + partner * sin)
    return outs


def _spatial_gating(u, v, g_ref, w_ref, b_ref, o_ref):
    ms = jnp.mean(v * v, axis=-1, keepdims=True)
    vb = (v * lax.rsqrt(ms + NORM_EPS) * g_ref[...]).astype(MXU_DTYPE)
    for n in range(u.shape[0] // SGU_CHUNK):
        rs = slice(n * SGU_CHUNK, (n + 1) * SGU_CHUNK)
        for g in range(SGU_GROUPS):
            cs = slice(g * SGU_GROUP_DIM, (g + 1) * SGU_GROUP_DIM)
            mixed = _dot(w_ref[g], vb[rs, cs]) + b_ref[g]
            o_ref[rs, cs] = (u[rs, cs] * mixed).astype(ACT_DTYPE)


def _in_proj_body(*refs, segs, rope, sgu, n_out):
    x_ref, sh_ref, sc_ref, g_ref, w_ref = refs[:5]
    pos = 5
    if rope:
        cos_ref, sin_ref = refs[pos:pos + 2]
        pos += 2
    if sgu:
        sg_ref, sw_ref, sb_ref = refs[pos:pos + 3]
        pos += 3
    out_refs = refs[pos:pos + n_out]
    x = x_ref[...]
    ms = jnp.mean(x * x, axis=-1, keepdims=True)
    y = x * lax.rsqrt(ms + NORM_EPS) * g_ref[...]
    h = (y * (1.0 + sc_ref[...]) + sh_ref[...]).astype(MXU_DTYPE)
    for (c0, width, kind, oi) in segs:
        if kind == "sgu":
            u = _gelu(_dot(h, w_ref[:, c0:c0 + BRANCH_DIM]))
            v = _gelu(_dot(h, w_ref[:, c0 + BRANCH_DIM:c0 + 2 * BRANCH_DIM]))
            _spatial_gating(u, v, sg_ref, sw_ref, sb_ref, out_refs[oi])
            continue
        step = min(width, 512)
        for cc in range(0, width, step):
            acc = _dot(h, w_ref[:, c0 + cc:c0 + cc + step])
            if kind == "rope" and rope:
                for k, piece in enumerate(_rope(acc, cos_ref[...], sin_ref[...])):
                    out_refs[oi][:, cc + k * LANES:cc + (k + 1) * LANES] = piece.astype(ACT_DTYPE)
            else:
                out_refs[oi][:, cc:cc + step] = acc.astype(ACT_DTYPE)


def _in_proj(x, mod, mod_row, norm_g, w, layer, segs, rope_tabs, sgu, name):
    bx, l, d = x.shape
    tm = min(IN_PROJ_TM, l)
    assert all(s[0] + s[1] <= OFF_GATE for s in segs)
    n_out = max(s[3] for s in segs) + 1
    out_w = lambda s: BRANCH_DIM if s[2] == "sgu" else s[1]
    widths = [sum(out_w(s) for s in segs if s[3] == oi) for oi in range(n_out)]
    rope = rope_tabs is not None
    use_sgu = any(s[2] == "sgu" for s in segs)
    in_specs = [
        pl.BlockSpec((None, tm, d), lambda b, i: (b, i, 0)),
        _mod_spec(lambda b, i: mod_row(b), 0),
        _mod_spec(lambda b, i: mod_row(b), 1),
        _layer_spec(norm_g, layer),
        pl.BlockSpec((None, d, OFF_GATE), lambda *_: (layer, 0, 0), pipeline_mode=pl.Buffered(1)),
    ]
    args = [x, mod, mod, norm_g, w]
    if rope:
        in_specs += [pl.BlockSpec((tm, LANES), lambda b, i: (i, 0))] * 2
        args += list(rope_tabs)
    if use_sgu:
        in_specs += [_layer_spec(p, layer) for p in sgu]
        args += list(sgu)
    return pl.pallas_call(
        functools.partial(_in_proj_body, segs=segs, rope=rope, sgu=use_sgu, n_out=n_out),
        grid=(bx, l // tm),
        in_specs=in_specs,
        out_specs=[pl.BlockSpec((None, tm, wd), lambda b, i: (b, i, 0)) for wd in widths],
        out_shape=[jax.ShapeDtypeStruct((bx, l, wd), ACT_DTYPE) for wd in widths],
        compiler_params=_cparams(("arbitrary", "arbitrary")),
        name=name,
    )(*args)


IN_PROJ_TM = 1024
FULL_SEGS = (
    (OFF_SGU, 2 * BRANCH_DIM, "sgu", 0),
    (OFF_QKV, BRANCH_DIM, "rope", 1),
    (OFF_QKV + BRANCH_DIM, BRANCH_DIM, "rope", 2),
    (OFF_QKV + 2 * BRANCH_DIM, BRANCH_DIM, "plain", 3),
    (OFF_HY, 3 * HY_DIM, "plain", 4),
)
KV_SEGS = ((OFF_QKV + BRANCH_DIM, BRANCH_DIM, "plain", 0), (OFF_QKV + 2 * BRANCH_DIM, BRANCH_DIM, "plain", 1))


def _att_plan(rows):
    kr = min(NA_WIN_ROWS, rows)
    assert rows % Q_ROWS_PER_BLOCK == 0 and rows >= K_ROWS_PER_BLOCK and kr == NA_WIN_ROWS
    patterns, type_of, key_start = [], [], []
    for rb in range(0, rows, Q_ROWS_PER_BLOCK):
        ks = min(max(rb - NA_WIN_ROWS // 2, 0), rows - K_ROWS_PER_BLOCK)
        assert ks % Q_ROWS_PER_BLOCK == 0
        pat = []
        for qi in range(Q_ROWS_PER_BLOCK):
            rq = rb + qi
            r0 = min(max(rq - kr // 2, 0), rows - kr)
            for j in range(K_ROWS_PER_BLOCK):
                rk = ks + j
                pat.append(rk - rq + NA_WIN_ROWS - 1 if r0 <= rk < r0 + kr else None)
        pat = tuple(pat)
        if pat not in patterns:
            patterns.append(pat)
        type_of.append(patterns.index(pat))
        key_start.append(ks // Q_ROWS_PER_BLOCK)
    return patterns, type_of, key_start


def _bias_body(rpb_ref, o_ref, *, patterns):
    n_dr, n_dc = 2 * NA_WIN_ROWS - 1, 2 * NA_WIN_COLS - 1
    h = pl.program_id(0)
    qc = lax.broadcasted_iota(jnp.int32, (GRID_W, GRID_W), 0)
    kc = lax.broadcasted_iota(jnp.int32, (GRID_W, GRID_W), 1)
    cstart = jnp.clip(qc - NA_WIN_COLS // 2, 0, GRID_W - NA_WIN_COLS)
    col_ok = (kc >= cstart) & (kc < cstart + NA_WIN_COLS)
    dc = jnp.clip(kc - qc + NA_WIN_COLS - 1, 0, n_dc - 1)
    neg = jnp.full((GRID_W, GRID_W), NEG_INF, F32)
    tiles = []
    for dr in range(n_dr):
        t = jnp.zeros((GRID_W, GRID_W), F32)
        for d in range(n_dc):
            t = jnp.where(dc == d, rpb_ref[(h * n_dr + dr) * n_dc + d], t)
        tiles.append(jnp.where(col_ok, t * LOG2E, neg))
    for ty, pat in enumerate(patterns):
        for qi in range(Q_ROWS_PER_BLOCK):
            for j in range(K_ROWS_PER_BLOCK):
                dr = pat[qi * K_ROWS_PER_BLOCK + j]
                o_ref[ty, qi * GRID_W:(qi + 1) * GRID_W, j * GRID_W:(j + 1) * GRID_W] = neg if dr is None else tiles[dr]


def _bias_tables(rpb, patterns):
    nt = len(patterns)
    return pl.pallas_call(
        functools.partial(_bias_body, patterns=patterns),
        grid=(NA_HEADS,),
        in_specs=[pl.BlockSpec(memory_space=pltpu.SMEM)],
        out_specs=pl.BlockSpec((nt, None, ATT_TOK, K_ROWS_PER_BLOCK * GRID_W), lambda h: (0, h, 0, 0)),
        out_shape=jax.ShapeDtypeStruct((nt, NA_HEADS, ATT_TOK, K_ROWS_PER_BLOCK * GRID_W), F32),
        compiler_params=_cparams(("arbitrary",)),
        name="attention_bias_tables",
    )(rpb.reshape(-1))


def _head_mask(shape, hh):
    lane = lax.broadcasted_iota(jnp.int32, shape, 1)
    return (lane >= hh * NA_HEAD_DIM) & (lane < (hh + 1) * NA_HEAD_DIM)


LOG2E = math.log2(math.e)
QK_SCALE = NA_HEAD_DIM ** -0.5 * LOG2E


def _pair_attention(qp, keys, biases, values):
    m_rows = qp.shape[0]
    q2 = jnp.concatenate([jnp.where(_head_mask(qp.shape, hh), qp, jnp.zeros_like(qp)) for hh in range(2)], axis=0)
    s_all = _dot_nt(q2, jnp.concatenate(keys, axis=0))
    scores, c0 = [], 0
    for j, kj in enumerate(keys):
        s = s_all[:, c0:c0 + kj.shape[0]]
        c0 += kj.shape[0]
        if biases[0][j] is not None:
            s = s + jnp.concatenate([biases[0][j], biases[1][j]], axis=0)
        scores.append(s)
    m = functools.reduce(jnp.maximum, [jnp.max(s, axis=-1, keepdims=True) for s in scores])
    e = jnp.concatenate([jnp.exp2(s - m).astype(MXU_DTYPE) for s in scores], axis=1)
    vcat = jnp.concatenate(values, axis=0)
    acc = _dot(e, jnp.concatenate([vcat, jnp.ones_like(vcat)], axis=1))
    out = acc[:, :LANES] / acc[:, LANES:]
    return jnp.where(_head_mask((m_rows, LANES), 0), out[:m_rows], out[m_rows:])


ATT_BATCH = 4


def _attn_body(ty_ref, kb_ref, q_ref, k0, k1, k2, v0, v1, v2, kc_ref, vc_ref, bias_ref, o_ref):
    del ty_ref, kb_ref
    for p in range(NA_HEADS // 2):
        cs = slice(p * LANES, (p + 1) * LANES)
        biases = [[bias_ref[2 * p + hh, :, j * ATT_TOK:(j + 1) * ATT_TOK] for j in range(3)] + [None]
                  for hh in range(2)]
        for e in range(q_ref.shape[0]):
            qp = q_ref[e, :, cs] * QK_SCALE
            keys = [k0[e, :, cs], k1[e, :, cs], k2[e, :, cs], kc_ref[e, :, cs]]
            values = [v0[e, :, cs], v1[e, :, cs], v2[e, :, cs], vc_ref[e, :, cs]]
            o_ref[e, :, cs] = _pair_attention(qp, keys, biases, values).astype(ACT_DTYPE)


def _neighborhood_attention(q, k, v, kc, vc, bias, type_of, key_start):
    b, l, _ = q.shape
    lc = kc.shape[1]
    n_blk = l // ATT_TOK
    nb = math.gcd(b, ATT_BATCH)
    tok = lambda off: pl.BlockSpec((nb, ATT_TOK, BRANCH_DIM), lambda r, bb, ty, kb: (bb, kb[r] + off, 0))
    ctx = pl.BlockSpec((nb, lc, BRANCH_DIM), lambda r, bb, ty, kb: (bb, 0, 0))
    grid_spec = pltpu.PrefetchScalarGridSpec(
        num_scalar_prefetch=2,
        grid=(n_blk, b // nb),
        in_specs=[
            pl.BlockSpec((nb, ATT_TOK, BRANCH_DIM), lambda r, bb, ty, kb: (bb, r, 0)),
            tok(0), tok(1), tok(2), tok(0), tok(1), tok(2), ctx, ctx,
            pl.BlockSpec((None, NA_HEADS, ATT_TOK, K_ROWS_PER_BLOCK * GRID_W), lambda r, bb, ty, kb: (ty[r], 0, 0, 0)),
        ],
        out_specs=pl.BlockSpec((nb, ATT_TOK, BRANCH_DIM), lambda r, bb, ty, kb: (bb, r, 0)),
    )
    return pl.pallas_call(
        _attn_body,
        grid_spec=grid_spec,
        out_shape=jax.ShapeDtypeStruct((b, l, BRANCH_DIM), ACT_DTYPE),
        compiler_params=_cparams(("arbitrary", "arbitrary")),
        name="neighborhood_attention",
    )(jnp.asarray(type_of, jnp.int32), jnp.asarray(key_start, jnp.int32), q, k, k, k, v, v, v, kc, vc, bias)


def _ctx_attn_body(q_ref, k_ref, v_ref, o_ref):
    for p in range(NA_HEADS // 2):
        cs = slice(p * LANES, (p + 1) * LANES)
        qp = q_ref[:, cs] * QK_SCALE
        o_ref[:, cs] = _pair_attention(qp, [k_ref[:, cs]], [[None], [None]], [v_ref[:, cs]]).astype(ACT_DTYPE)


def _context_attention(q, k, v):
    b, lc, _ = q.shape
    spec = pl.BlockSpec((None, lc, BRANCH_DIM), lambda bb: (bb, 0, 0))
    return pl.pallas_call(
        _ctx_attn_body,
        grid=(b,),
        in_specs=[spec, spec, spec],
        out_specs=spec,
        out_shape=jax.ShapeDtypeStruct((b, lc, BRANCH_DIM), ACT_DTYPE),
        compiler_params=_cparams(("arbitrary",)),
        name="context_attention",
    )(q, k, v)


HY_CW = 256


@functools.lru_cache(maxsize=None)
def _dft_consts_np(l):
    n = 2 * l
    idx = np.arange(l, dtype=np.int64)
    ang = ((idx[:, None] * idx[None, :]) % n).astype(np.float64) * (2.0 * math.pi / n)
    cs = np.concatenate([np.cos(ang), -np.sin(ang)], axis=1).astype(np.float32)
    sign = np.where(idx % 2 == 0, 1.0, -1.0).astype(np.float32).reshape(l, 1)
    wf = np.where(idx == 0, 1.0 / n, 2.0 / n).astype(np.float32).reshape(l, 1)
    return cs, sign, wf


def _dft_consts(l):
    cs, sign, wf = _dft_consts_np(l)
    return jnp.asarray(cs.astype(MXU_DTYPE)), jnp.asarray(sign), jnp.asarray(wf)


def _filter_consts(l):
    t = np.linspace(0.0, 1.0, l, dtype=np.float32)[:, None]
    n_bands = (HY_EMB - 1) // 2
    bands = np.linspace(1e-4, n_bands - 1, n_bands, dtype=np.float32)[None, :]
    ang = np.float32(2.0 * math.pi) * np.arange(l, dtype=np.float32)[:, None] / np.float32(l) * bands
    z = np.concatenate([t, np.cos(ang), -np.sin(ang)], axis=-1).astype(np.float32)
    z = np.pad(z, ((0, 0), (0, LANES - HY_EMB)))
    max_decay = math.log(HY_DECAY_TARGET) / HY_FAST_DECAY_PCT
    min_decay = math.log(HY_DECAY_TARGET) / HY_SLOW_DECAY_PCT
    deltas = np.abs(np.linspace(min_decay, max_decay, HY_DIM, dtype=np.float32)).reshape(1, HY_DIM)
    return jnp.asarray(z), jnp.asarray(t), jnp.asarray(deltas)


def _filter_body(z_ref, w1_ref, b1_ref, w2_ref, b2_ref, fr_ref, w3a_ref, w3b_ref, t_ref, dl_ref,
                 sg_ref, wf_ref, cs_ref, kc_ref, ks_ref, kn_ref, h_ref):
    l = z_ref.shape[0]

    @pl.when((pl.program_id(0) == 0) & (pl.program_id(1) == 0))
    def _():
        fr = fr_ref[...]
        h1 = jnp.sin(fr * (_dot(z_ref[...], w1_ref[...], precision=HIGHEST) + b1_ref[...]))
        h_ref[...] = jnp.sin(fr * (_dot(h1, w2_ref[...], precision=HIGHEST) + b2_ref[...]))

    h = h_ref[...]
    decay = jnp.exp(-t_ref[...] * dl_ref[...])
    h0 = _dot(h, w3a_ref[...], precision=HIGHEST) * decay
    h1 = _dot(h, w3b_ref[...], precision=HIGHEST) * decay
    row = lax.broadcasted_iota(jnp.int32, h1.shape, 0)
    h1 = jnp.where(row == 0, 0.0, h1)
    den = jnp.sum(jnp.abs(h0), axis=0, keepdims=True) + jnp.sum(jnp.abs(h1), axis=0, keepdims=True)
    hs = (h0 + h1) / den
    hd = (h0 - h1) / den
    kn_ref[...] = jnp.sum(sg_ref[...] * hs, axis=0, keepdims=True) * (1.0 / (2 * l))
    kc_ref[...] = (wf_ref[...] * _dot(cs_ref[:, :l], hs.astype(MXU_DTYPE))).astype(kc_ref.dtype)
    ks_ref[...] = (wf_ref[...] * _dot(cs_ref[:, l:], hd.astype(MXU_DTYPE))).astype(ks_ref.dtype)


def _hyena_filters(l, dft, fconsts, w1, b1, w2, b2, w3, freq):
    cs, sign, wf = dft
    z, t, deltas = fconsts
    cw = HY_CW
    nb = HY_DIM // cw
    w1p = jnp.pad(w1, ((0, LANES - HY_EMB), (0, 0)))
    hid = HY_FILTER_HIDDEN
    small = lambda shape: pl.BlockSpec(shape, lambda n, cb: (0,) * len(shape))
    spec_out = pl.BlockSpec((None, l, cw), lambda n, cb: (n, 0, cb))
    return pl.pallas_call(
        _filter_body,
        grid=(HY_ORDER, nb),
        in_specs=[
            small((l, LANES)), small((LANES, hid)), small((1, hid)), small((hid, hid)), small((1, hid)), small((1, hid)),
            pl.BlockSpec((hid, cw), lambda n, cb: (0, n * 2 * nb + cb)),
            pl.BlockSpec((hid, cw), lambda n, cb: (0, n * 2 * nb + nb + cb)),
            small((l, 1)),
            pl.BlockSpec((1, cw), lambda n, cb: (0, cb)),
            small((l, 1)), small((l, 1)),
            _const_spec((l, 2 * l)),
        ],
        out_specs=[spec_out, spec_out, pl.BlockSpec((None, 1, cw), lambda n, cb: (n, 0, cb))],
        out_shape=[jax.ShapeDtypeStruct((HY_ORDER, l, HY_DIM), ACT_DTYPE)] * 2
        + [jax.ShapeDtypeStruct((HY_ORDER, 1, HY_DIM), F32)],
        scratch_shapes=[pltpu.VMEM((l, hid), F32)],
        compiler_params=_cparams(("arbitrary", "arbitrary")),
        name="hyena_filter_spectrum",
    )(z, w1p, b1.reshape(1, hid), w2, b2.reshape(1, hid), freq.reshape(1, hid), w3, w3, t, deltas, sign, wf, cs)


HY_TM = 512
CONV_HALO = 16


def _conv3_rows(p_ref, w_ref, b_ref, r0, rows):
    l = p_ref.shape[0]
    lo, hi = max(r0 - CONV_HALO, 0), min(r0 + rows + CONV_HALO, l)
    x = p_ref[lo:hi, :].astype(F32)
    n = hi - lo
    prev, nxt = pltpu.roll(x, 1, 0), pltpu.roll(x, n - 1, 0)
    row = lax.broadcasted_iota(jnp.int32, x.shape, 0)
    if lo == 0:
        prev = jnp.where(row == 0, 0.0, prev)
    if hi == l:
        nxt = jnp.where(row == n - 1, 0.0, nxt)
    y = prev * w_ref[0:1, :] + x * w_ref[1:2, :] + nxt * w_ref[2:3, :] + b_ref[...]
    return y[r0 - lo:r0 - lo + rows]


HY_BATCH = 2


def _hyena_body(pv_ref, p1_ref, p2_ref, wv_ref, w1_ref, w2_ref, bv_ref, b1_ref, b2_ref, skip_ref,
                kc_ref, ks_ref, kn_ref, sg_ref, cs_ref, o_ref, zf_ref, zb_ref, y_ref):
    nb, l, cw = pv_ref.shape
    tm = min(HY_TM, l)
    chunks = [slice(r0, r0 + tm) for r0 in range(0, l, tm)]
    zn = [jnp.zeros((1, cw), F32) for _ in range(nb)]
    for e in range(nb):
        for rs in chunks:
            z = _conv3_rows(pv_ref.at[e], wv_ref, bv_ref, rs.start, tm)
            zf_ref[e, rs] = z
            zb_ref[e, rs] = z.astype(MXU_DTYPE)
            zn[e] = zn[e] + jnp.sum(sg_ref[rs] * z, axis=0, keepdims=True)
    gates = ((p1_ref, w1_ref, b1_ref), (p2_ref, w2_ref, b2_ref))
    for n, (p_ref, w_ref, b_ref) in enumerate(gates):
        for e in range(nb):
            for rs in chunks:
                zc = _dot(cs_ref[rs, :l], zb_ref[e])
                zs = _dot(cs_ref[rs, l:], zb_ref[e])
                kc, ks = kc_ref[n, rs].astype(F32), ks_ref[n, rs].astype(F32)
                y_ref[e, rs] = (zc * kc - zs * ks).astype(MXU_DTYPE)
                y_ref[e, l + rs.start:l + rs.stop] = (zc * ks + zs * kc).astype(MXU_DTYPE)
        nyq = [zn[e] * kn_ref[n] for e in range(nb)]
        zn = [jnp.zeros((1, cw), F32) for _ in range(nb)]
        for e in range(nb):
            for rs in chunks:
                y = _dot(cs_ref[rs, :], y_ref[e]) + sg_ref[rs] * nyq[e]
                gate = _conv3_rows(p_ref.at[e], w_ref, b_ref, rs.start, tm)
                z = gate * (y + zf_ref[e, rs] * skip_ref[n:n + 1, :])
                if n + 1 < HY_ORDER:
                    zf_ref[e, rs] = z
                    zb_ref[e, rs] = z.astype(MXU_DTYPE)
                    zn[e] = zn[e] + jnp.sum(sg_ref[rs] * z, axis=0, keepdims=True)
                else:
                    o_ref[e, rs] = z.astype(ACT_DTYPE)


def _hyena(p, conv_w, conv_b, skip, filters, dft):
    bx, l, _ = p.shape
    cs, sign, _ = dft
    kcs, kss, kns = filters
    cw = HY_CW
    nb = HY_DIM // cw
    ne = math.gcd(bx, HY_BATCH)
    pspec = lambda part: pl.BlockSpec((ne, l, cw), lambda cb, b: (b, 0, part * nb + cb))
    wspec = lambda part: pl.BlockSpec((3, cw), lambda cb, b: (0, part * nb + cb))
    bspec = lambda part: pl.BlockSpec((1, cw), lambda cb, b: (0, part * nb + cb))
    fspec = pl.BlockSpec((HY_ORDER, l, cw), lambda cb, b: (0, 0, cb), pipeline_mode=pl.Buffered(1))
    return pl.pallas_call(
        _hyena_body,
        grid=(nb, bx // ne),
        in_specs=[
            pspec(0), pspec(1), pspec(2), wspec(0), wspec(1), wspec(2), bspec(0), bspec(1), bspec(2),
            pl.BlockSpec((HY_ORDER, cw), lambda cb, b: (0, cb)),
            fspec, fspec,
            pl.BlockSpec((HY_ORDER, 1, cw), lambda cb, b: (0, 0, cb)),
            _const_spec((l, 1)), _const_spec((l, 2 * l)),
        ],
        out_specs=pl.BlockSpec((ne, l, cw), lambda cb, b: (b, 0, cb)),
        out_shape=jax.ShapeDtypeStruct((bx, l, HY_DIM), ACT_DTYPE),
        scratch_shapes=[pltpu.VMEM((ne, l, cw), F32), pltpu.VMEM((ne, l, cw), MXU_DTYPE),
                        pltpu.VMEM((ne, 2 * l, cw), MXU_DTYPE)],
        compiler_params=_cparams(("arbitrary", "arbitrary")),
        name="hyena_long_conv",
    )(p, p, p, conv_w, conv_w, conv_w, conv_b.reshape(1, -1), conv_b.reshape(1, -1), conv_b.reshape(1, -1),
      skip, kcs, kss, kns, sign, cs)


MERGE_TM = 1024


def _merge_body(x_ref, a_ref, at_ref, hy_ref, sh_ref, sc_ref, g1_ref, ng_ref, wg0_ref, wg1_ref, wg2_ref,
                wb_ref, wo_ref, o_ref):
    x = x_ref[...]
    ms = jnp.mean(x * x, axis=-1, keepdims=True)
    y = x * lax.rsqrt(ms + NORM_EPS) * ng_ref[...]
    h = (y * (1.0 + sc_ref[...]) + sh_ref[...]).astype(MXU_DTYPE)
    m = None
    for j, (br_ref, wg_ref) in enumerate(((a_ref, wg0_ref), (at_ref, wg1_ref), (hy_ref, wg2_ref))):
        term = jax.nn.sigmoid(_dot(h, wg_ref[...])) * _dot(br_ref[...], wb_ref[j])
        m = term if m is None else m + term
    o_ref[...] = x + g1_ref[...] * _dot(m.astype(MXU_DTYPE), wo_ref[...])


def _merge(x, a, attn, hy, mod, mod_row, norm_g, w_in, wb, wo, layer):
    bx, l, d = x.shape
    tm = min(MERGE_TM, l)
    tile = lambda w: pl.BlockSpec((None, tm, w), lambda b, i: (b, i, 0))
    gate_w = lambda j: pl.BlockSpec((None, d, d), lambda *_: (layer, 0, OFF_GATE // d + j), pipeline_mode=pl.Buffered(1))
    return pl.pallas_call(
        _merge_body,
        grid=(bx, l // tm),
        in_specs=[tile(d), tile(BRANCH_DIM), tile(BRANCH_DIM), tile(BRANCH_DIM),
                  _mod_spec(lambda b, i: mod_row(b), 0), _mod_spec(lambda b, i: mod_row(b), 1),
                  _mod_spec(lambda b, i: mod_row(b), 2), _layer_spec(norm_g, layer),
                  gate_w(0), gate_w(1), gate_w(2), _layer_spec(wb, layer), _layer_spec(wo, layer)],
        out_specs=tile(d),
        out_shape=jax.ShapeDtypeStruct((bx, l, d), F32),
        compiler_params=_cparams(("arbitrary", "arbitrary")),
        name="branch_merge",
    )(x, a, attn, hy, mod, mod, mod, norm_g, w_in, w_in, w_in, wb, wo)


FFN_TM = 1024
FFN_TN = 256


def _ffn_body(xp_ref, x_ref, xn_ref, sh_ref, sc_ref, g2_ref, ng_ref, wu_ref, cw_ref, cb_ref, wd_ref, fg_ref,
              o_ref, h_ref, gt_ref, *, final, seq_len):
    tm = x_ref.shape[0]
    halo = CONV_HALO
    n_ext = tm + 2 * halo

    def norm_mod(x):
        ms = jnp.mean(x * x, axis=-1, keepdims=True)
        y = x * lax.rsqrt(ms + NORM_EPS) * ng_ref[...]
        return (y * (1.0 + sc_ref[...]) + sh_ref[...]).astype(MXU_DTYPE)

    h_ref[0:halo] = norm_mod(xp_ref[...])
    h_ref[halo:halo + tm] = norm_mod(x_ref[...])
    h_ref[halo + tm:n_ext] = norm_mod(xn_ref[...])
    row = lax.broadcasted_iota(jnp.int32, (n_ext, 1), 0)
    pos = (pl.program_id(1) * tm + row + (seq_len - halo)) % seq_len
    first, last = pos == 0, pos == seq_len - 1
    for j in range(D_FF // FFN_TN):
        cs = slice(j * FFN_TN, (j + 1) * FFN_TN)
        a = _dot(h_ref[...], wu_ref[:, cs])
        prev = jnp.where(first, 0.0, pltpu.roll(a, 1, 0))
        nxt = jnp.where(last, 0.0, pltpu.roll(a, n_ext - 1, 0))
        a = prev * cw_ref[0:1, cs] + a * cw_ref[1:2, cs] + nxt * cw_ref[2:3, cs] + cb_ref[:, cs]
        gate = _dot(h_ref[halo:halo + tm], wu_ref[:, D_FF + j * FFN_TN:D_FF + (j + 1) * FFN_TN])
        gt_ref[:, cs] = (_gelu(a[halo:halo + tm]) * gate).astype(MXU_DTYPE)
    xn = x_ref[...] + g2_ref[...] * _dot(gt_ref[...], wd_ref[...])
    if final:
        ms = jnp.mean(xn * xn, axis=-1, keepdims=True)
        xn = xn * lax.rsqrt(ms + NORM_EPS) * fg_ref[...]
    o_ref[...] = xn


def _ffn(x, mod, mod_row, norm_g, w_up, conv_w, conv_b, w_down, layer, final_g, final, seq_len):
    bx, l, d = x.shape
    tm = min(FFN_TM, l)
    halo = CONV_HALO
    per, last_blk = tm // halo, l // halo - 1
    tile = pl.BlockSpec((None, tm, d), lambda b, i: (b, i, 0))
    return pl.pallas_call(
        functools.partial(_ffn_body, final=final, seq_len=seq_len),
        grid=(bx, l // tm),
        in_specs=[
            pl.BlockSpec((None, halo, d), lambda b, i: (b, jnp.maximum(i * per - 1, 0), 0)),
            tile,
            pl.BlockSpec((None, halo, d), lambda b, i: (b, jnp.minimum((i + 1) * per, last_blk), 0)),
            _mod_spec(lambda b, i: mod_row(b), 3), _mod_spec(lambda b, i: mod_row(b), 4),
            _mod_spec(lambda b, i: mod_row(b), 5),
            _layer_spec(norm_g, layer), _layer_spec(w_up, layer), _layer_spec(conv_w, layer),
            _layer_spec(conv_b, layer), _layer_spec(w_down, layer), _const_spec((1, d)),
        ],
        out_specs=tile,
        out_shape=jax.ShapeDtypeStruct((bx, l, d), F32),
        scratch_shapes=[pltpu.VMEM((tm + 2 * halo, d), MXU_DTYPE), pltpu.VMEM((tm, D_FF), MXU_DTYPE)],
        compiler_params=_cparams(("arbitrary", "arbitrary")),
        name="conv_ffn",
    )(x, x, x, mod, mod, mod, norm_g, w_up, conv_w, conv_b, w_down, final_g.reshape(1, d))


def _rope_tables(l):
    half = NA_HEAD_DIM // 2
    quarter = half // 2
    inv_freq = np.float32(ROPE_THETA) ** (-np.arange(quarter, dtype=np.float32) * np.float32(2.0) / np.float32(half))
    pos = np.arange(l)
    ang_r = (pos // GRID_W).astype(np.float32)[:, None] * inv_freq[None, :]
    ang_c = (pos % GRID_W).astype(np.float32)[:, None] * inv_freq[None, :]
    cos_h = np.concatenate([np.cos(ang_r), np.cos(ang_r), np.cos(ang_c), np.cos(ang_c)], axis=-1)
    sin_h = np.concatenate([-np.sin(ang_r), np.sin(ang_r), -np.sin(ang_c), np.sin(ang_c)], axis=-1)
    reps = LANES // NA_HEAD_DIM
    return (jnp.asarray(np.tile(cos_h, (1, reps)).astype(np.float32)),
            jnp.asarray(np.tile(sin_h, (1, reps)).astype(np.float32)))


def kernel(x, c, ctx, c_ctx, ada_w, ada_b, norm1_g, norm2_g, w_in, sgu_norm_g, sgu_w, sgu_b, na_rpb, hy_conv_w, hy_conv_b, hy_filt_w1, hy_filt_b1, hy_filt_w2, hy_filt_b2, hy_filt_w3, hy_sin_freq, hy_skip, w_branch, w_out, ffn_w_up, ffn_conv_w, ffn_conv_b, ffn_w_down, final_norm_g):
    b, l, d = x.shape
    lc = ctx.shape[1]
    assert d == D_MODEL and l % ATT_TOK == 0 and l % GRID_W == 0

    n_rows = -(-(b + 1) // 8) * 8
    cc = jnp.concatenate([c, c_ctx[None, :], jnp.zeros((n_rows - b - 1, d), F32)], axis=0)
    mod = _modulation(cc, ada_w, ada_b).reshape(DEPTH * n_rows, 1, N_MOD * d)

    rope_tabs = _rope_tables(l)
    dft_l, fc_l = _dft_consts(l), _filter_consts(l)
    dft_c, fc_c = _dft_consts(lc), _filter_consts(lc)
    patterns, type_of, key_start = _att_plan(l // GRID_W)

    w_in_b, wb_b, wo_b = w_in.astype(MXU_DTYPE), w_branch.astype(MXU_DTYPE), w_out.astype(MXU_DTYPE)
    wup_b, wdn_b = ffn_w_up.astype(MXU_DTYPE), ffn_w_down.astype(MXU_DTYPE)
    n1g, n2g = norm1_g.reshape(DEPTH, 1, d), norm2_g.reshape(DEPTH, 1, d)
    ffn_cb = ffn_conv_b.reshape(DEPTH, 1, D_FF)
    sgu = (sgu_norm_g.reshape(DEPTH, 1, BRANCH_DIM), sgu_w.astype(MXU_DTYPE),
           sgu_b.reshape(DEPTH, SGU_GROUPS, SGU_CHUNK, 1))

    xc = ctx
    for i in range(DEPTH):
        update_ctx = i < DEPTH - 1
        lat_row = lambda bb, i=i: i * n_rows + bb
        ctx_row = lambda bb, i=i: i * n_rows + b
        filt_args = (hy_filt_w1[i], hy_filt_b1[i], hy_filt_w2[i], hy_filt_b2[i], hy_filt_w3[i], hy_sin_freq[i])
        bias = _bias_tables(na_rpb[i], patterns)

        if update_ctx:
            a_c, q_c, k_c, v_c, hy_c = _in_proj(xc, mod, ctx_row, n1g, w_in_b, i, FULL_SEGS, None, sgu,
                                                "context_in_proj")
        else:
            k_c, v_c = _in_proj(xc, mod, ctx_row, n1g, w_in_b, i, KV_SEGS, None, None, "context_kv_proj")

        a, q, k, v, hy = _in_proj(x, mod, lat_row, n1g, w_in_b, i, FULL_SEGS, rope_tabs, sgu, "latent_in_proj")
        attn = _neighborhood_attention(q, k, v, k_c, v_c, bias, type_of, key_start)
        filt = _hyena_filters(l, dft_l, fc_l, *filt_args)
        hyo = _hyena(hy, hy_conv_w[i], hy_conv_b[i], hy_skip[i], filt, dft_l)
        x = _merge(x, a, attn, hyo, mod, lat_row, n1g, w_in_b, wb_b, wo_b, i)
        x = _ffn(x, mod, lat_row, n2g, wup_b, ffn_conv_w, ffn_cb, wdn_b, i, final_norm_g,
                 final=not update_ctx, seq_len=l)

        if update_ctx:
            attn_c = _context_attention(q_c, k_c, v_c)
            filt_c = _hyena_filters(lc, dft_c, fc_c, *filt_args)
            hyo_c = _hyena(hy_c, hy_conv_w[i], hy_conv_b[i], hy_skip[i], filt_c, dft_c)
            xc = _merge(xc, a_c, attn_c, hyo_c, mod, ctx_row, n1g, w_in_b, wb_b, wo_b, i)
            grp = math.gcd(b, max(1, l // lc))
            xc = _ffn(xc.reshape(b // grp, grp * lc, d), mod, ctx_row, n2g, wup_b, ffn_conv_w, ffn_cb, wdn_b, i,
                      final_norm_g, final=False, seq_len=lc).reshape(b, lc, d)
    return x
```

```python
import functools
import math

import jax
import jax.numpy as jnp
import numpy as np
from jax import lax
from jax.experimental import pallas as pl
from jax.experimental.pallas import tpu as pltpu

D_MODEL = 1024
DEPTH = 2
GRID_W = 64
NORM_EPS = 1e-6
N_MOD = 6
BRANCH_DIM = D_MODEL // 2
SGU_GROUP_DIM = 128
SGU_GROUPS = BRANCH_DIM // SGU_GROUP_DIM
SGU_CHUNK = 128
NA_HEAD_DIM = 64
NA_HEADS = BRANCH_DIM // NA_HEAD_DIM
NA_WIN_ROWS = 8
NA_WIN_COLS = 16
ROPE_THETA = 10000.0
NEG_INF = -1e30
HY_DIM = BRANCH_DIM
HY_ORDER = 2
HY_EMB = 33
HY_FILTER_HIDDEN = 64
HY_FAST_DECAY_PCT = 0.3
HY_SLOW_DECAY_PCT = 1.5
HY_DECAY_TARGET = 1e-2
D_FF = 128 * ((8 * D_MODEL + 3 * 128 - 1) // (3 * 128))
OFF_SGU = 0
OFF_QKV = OFF_SGU + 2 * BRANCH_DIM
OFF_HY = OFF_QKV + 3 * BRANCH_DIM
OFF_GATE = OFF_HY + (HY_ORDER + 1) * HY_DIM
IN_COLS = OFF_GATE + 3 * D_MODEL

LANES = 128
MXU_WIDTH = 256
VMEM_LIMIT_BYTES = 56 * 1024 * 1024

MXU_DTYPE = jnp.bfloat16
ACT_DTYPE = jnp.bfloat16
F32 = jnp.float32
HIGHEST = lax.Precision.HIGHEST

Q_ROWS_PER_BLOCK = 4
K_ROWS_PER_BLOCK = 12
ATT_TOK = Q_ROWS_PER_BLOCK * GRID_W


def _cparams(sem):
    return pltpu.CompilerParams(dimension_semantics=sem, vmem_limit_bytes=VMEM_LIMIT_BYTES)


def _const_spec(shape):
    nd = len(shape)
    return pl.BlockSpec(shape, lambda *_: (0,) * nd, pipeline_mode=pl.Buffered(1))


def _layer_spec(arr, layer):
    nd = arr.ndim - 1
    return pl.BlockSpec((None,) + arr.shape[1:], lambda *_: (layer,) + (0,) * nd, pipeline_mode=pl.Buffered(1))


def _dot(a, b, **kw):
    return jnp.dot(a, b, preferred_element_type=F32, **kw)


def _dot_nt(a, b):
    return lax.dot_general(a, b, (((1,), (1,)), ((), ())), preferred_element_type=F32)


def _gelu(x):
    return 0.5 * x * (1.0 + lax.erf(x * (1.0 / math.sqrt(2.0))))


def _mod_body(c_ref, w_ref, b_ref, o_ref):
    c = c_ref[...]
    s = c * jax.nn.sigmoid(c)
    o_ref[...] = _dot(s, w_ref[...], precision=HIGHEST) + b_ref[...]


def _modulation(cc, ada_w, ada_b):
    depth, d, n = ada_w.shape
    r = cc.shape[0]
    tn = 1024
    return pl.pallas_call(
        _mod_body,
        grid=(depth, n // tn),
        in_specs=[
            pl.BlockSpec((r, d), lambda i, j: (0, 0)),
            pl.BlockSpec((None, d, tn), lambda i, j: (i, 0, j)),
            pl.BlockSpec((None, 1, tn), lambda i, j: (i, 0, j)),
        ],
        out_specs=pl.BlockSpec((None, r, tn), lambda i, j: (i, 0, j)),
        out_shape=jax.ShapeDtypeStruct((depth, r, n), F32),
        compiler_params=_cparams(("arbitrary", "arbitrary")),
        name="adaln_modulation",
    )(cc, ada_w, ada_b.reshape(depth, 1, n))


def _mod_spec(row_fn, chunk):
    return pl.BlockSpec((None, 1, D_MODEL), lambda *g: (row_fn(*g), 0, chunk))


def _rope(acc, cos, sin):
    lane = lax.broadcasted_iota(jnp.int32, cos.shape, 1)
    first_half = (lane % 32) < 16
    outs = []
    for c0 in range(0, acc.shape[1], LANES):
        xc = acc[:, c0:c0 + LANES]
        partner = jnp.where(first_half, pltpu.roll(xc, LANES - 16, 1), pltpu.roll(xc, 16, 1))
        outs.append(xc * cos + partner * sin)
    return outs


def _spatial_gating(u, v, g_ref, w_ref, b_ref, o_ref):
    ms = jnp.mean(v * v, axis=-1, keepdims=True)
    vb = (v * lax.rsqrt(ms + NORM_EPS) * g_ref[...]).astype(MXU_DTYPE)
    for n in range(u.shape[0] // SGU_CHUNK):
        rs = slice(n * SGU_CHUNK, (n + 1) * SGU_CHUNK)
        for g in range(SGU_GROUPS):
            cs = slice(g * SGU_GROUP_DIM, (g + 1) * SGU_GROUP_DIM)
            mixed = _dot(w_ref[g], vb[rs, cs]) + b_ref[g]
            o_ref[rs, cs] = (u[rs, cs] * mixed).astype(ACT_DTYPE)


def _in_proj_body(*refs, segs, rope, sgu, n_out):
    x_ref, sh_ref, sc_ref, g_ref, w_ref = refs[:5]
    pos = 5
    if rope:
        cos_ref, sin_ref = refs[pos:pos + 2]
        pos += 2
    if sgu:
        sg_ref, sw_ref, sb_ref = refs[pos:pos + 3]
        pos += 3
    out_refs = refs[pos:pos + n_out]
    x = x_ref[...]
    ms = jnp.mean(x * x, axis=-1, keepdims=True)
    y = x * lax.rsqrt(ms + NORM_EPS) * g_ref[...]
    h = (y * (1.0 + sc_ref[...]) + sh_ref[...]).astype(MXU_DTYPE)
    for (c0, width, kind, oi) in segs:
        if kind == "sgu":
            u = _gelu(_dot(h, w_ref[:, c0:c0 + BRANCH_DIM]))
            v = _gelu(_dot(h, w_ref[:, c0 + BRANCH_DIM:c0 + 2 * BRANCH_DIM]))
            _spatial_gating(u, v, sg_ref, sw_ref, sb_ref, out_refs[oi])
            continue
        step = min(width, 512)
        for cc in range(0, width, step):
            acc = _dot(h, w_ref[:, c0 + cc:c0 + cc + step])
            if kind == "rope" and rope:
                for k, piece in enumerate(_rope(acc, cos_ref[...], sin_ref[...])):
                    out_refs[oi][:, cc + k * LANES:cc + (k + 1) * LANES] = piece.astype(ACT_DTYPE)
            else:
                out_refs[oi][:, cc:cc + step] = acc.astype(ACT_DTYPE)


def _in_proj(x, mod, mod_row, norm_g, w, layer, segs, rope_tabs, sgu, name):
    bx, l, d = x.shape
    tm = min(IN_PROJ_TM, l)
    assert all(s[0] + s[1] <= OFF_GATE for s in segs)
    n_out = max(s[3] for s in segs) + 1
    out_w = lambda s: BRANCH_DIM if s[2] == "sgu" else s[1]
    widths = [sum(out_w(s) for s in segs if s[3] == oi) for oi in range(n_out)]
    rope = rope_tabs is not None
    use_sgu = any(s[2] == "sgu" for s in segs)
    in_specs = [
        pl.BlockSpec((None, tm, d), lambda b, i: (b, i, 0)),
        _mod_spec(lambda b, i: mod_row(b), 0),
        _mod_spec(lambda b, i: mod_row(b), 1),
        _layer_spec(norm_g, layer),
        pl.BlockSpec((None, d, OFF_GATE), lambda *_: (layer, 0, 0), pipeline_mode=pl.Buffered(1)),
    ]
    args = [x, mod, mod, norm_g, w]
    if rope:
        in_specs += [pl.BlockSpec((tm, LANES), lambda b, i: (i, 0))] * 2
        args += list(rope_tabs)
    if use_sgu:
        in_specs += [_layer_spec(p, layer) for p in sgu]
        args += list(sgu)
    return pl.pallas_call(
        functools.partial(_in_proj_body, segs=segs, rope=rope, sgu=use_sgu, n_out=n_out),
        grid=(bx, l // tm),
        in_specs=in_specs,
        out_specs=[pl.BlockSpec((None, tm, wd), lambda b, i: (b, i, 0)) for wd in widths],
        out_shape=[jax.ShapeDtypeStruct((bx, l, wd), ACT_DTYPE) for wd in widths],
        compiler_params=_cparams(("arbitrary", "arbitrary")),
        name=name,
    )(*args)


IN_PROJ_TM = 1024
FULL_SEGS = (
    (OFF_SGU, 2 * BRANCH_DIM, "sgu", 0),
    (OFF_QKV, BRANCH_DIM, "rope", 1),
    (OFF_QKV + BRANCH_DIM, BRANCH_DIM, "rope", 2),
    (OFF_QKV + 2 * BRANCH_DIM, BRANCH_DIM, "plain", 3),
    (OFF_HY, 3 * HY_DIM, "plain", 4),
)
KV_SEGS = ((OFF_QKV + BRANCH_DIM, BRANCH_DIM, "plain", 0), (OFF_QKV + 2 * BRANCH_DIM, BRANCH_DIM, "plain", 1))


def _att_plan(rows):
    kr = min(NA_WIN_ROWS, rows)
    assert rows % Q_ROWS_PER_BLOCK == 0 and rows >= K_ROWS_PER_BLOCK and kr == NA_WIN_ROWS
    patterns, type_of, key_start = [], [], []
    for rb in range(0, rows, Q_ROWS_PER_BLOCK):
        ks = min(max(rb - NA_WIN_ROWS // 2, 0), rows - K_ROWS_PER_BLOCK)
        assert ks % Q_ROWS_PER_BLOCK == 0
        pat = []
        for qi in range(Q_ROWS_PER_BLOCK):
            rq = rb + qi
            r0 = min(max(rq - kr // 2, 0), rows - kr)
            for j in range(K_ROWS_PER_BLOCK):
                rk = ks + j
                pat.append(rk - rq + NA_WIN_ROWS - 1 if r0 <= rk < r0 + kr else None)
        pat = tuple(pat)
        if pat not in patterns:
            patterns.append(pat)
        type_of.append(patterns.index(pat))
        key_start.append(ks // Q_ROWS_PER_BLOCK)
    return patterns, type_of, key_start


def _bias_body(rpb_ref, o_ref, *, patterns):
    n_dr, n_dc = 2 * NA_WIN_ROWS - 1, 2 * NA_WIN_COLS - 1
    h = pl.program_id(0)
    qc = lax.broadcasted_iota(jnp.int32, (GRID_W, GRID_W), 0)
    kc = lax.broadcasted_iota(jnp.int32, (GRID_W, GRID_W), 1)
    cstart = jnp.clip(qc - NA_WIN_COLS // 2, 0, GRID_W - NA_WIN_COLS)
    col_ok = (kc >= cstart) & (kc < cstart + NA_WIN_COLS)
    dc = jnp.clip(kc - qc + NA_WIN_COLS - 1, 0, n_dc - 1)
    neg = jnp.full((GRID_W, GRID_W), NEG_INF, F32)
    tiles = []
    for dr in range(n_dr):
        t = jnp.zeros((GRID_W, GRID_W), F32)
        for d in range(n_dc):
            t = jnp.where(dc == d, rpb_ref[(h * n_dr + dr) * n_dc + d], t)
        tiles.append(jnp.where(col_ok, t * LOG2E, neg))
    for ty, pat in enumerate(patterns):
        for qi in range(Q_ROWS_PER_BLOCK):
            for j in range(K_ROWS_PER_BLOCK):
                dr = pat[qi * K_ROWS_PER_BLOCK + j]
                o_ref[ty, qi * GRID_W:(qi + 1) * GRID_W, j * GRID_W:(j + 1) * GRID_W] = neg if dr is None else tiles[dr]


def _bias_tables(rpb, patterns):
    nt = len(patterns)
    return pl.pallas_call(
        functools.partial(_bias_body, patterns=patterns),
        grid=(NA_HEADS,),
        in_specs=[pl.BlockSpec(memory_space=pltpu.SMEM)],
        out_specs=pl.BlockSpec((nt, None, ATT_TOK, K_ROWS_PER_BLOCK * GRID_W), lambda h: (0, h, 0, 0)),
        out_shape=jax.ShapeDtypeStruct((nt, NA_HEADS, ATT_TOK, K_ROWS_PER_BLOCK * GRID_W), F32),
        compiler_params=_cparams(("arbitrary",)),
        name="attention_bias_tables",
    )(rpb.reshape(-1))


def _head_mask(shape, hh):
    lane = lax.broadcasted_iota(jnp.int32, shape, 1)
    return (lane >= hh * NA_HEAD_DIM) & (lane < (hh + 1) * NA_HEAD_DIM)


LOG2E = math.log2(math.e)
QK_SCALE = NA_HEAD_DIM ** -0.5 * LOG2E


def _pair_attention(qp, keys, biases, values):
    m_rows = qp.shape[0]
    q2 = jnp.concatenate([jnp.where(_head_mask(qp.shape, hh), qp, jnp.zeros_like(qp)) for hh in range(2)], axis=0)
    s_all = _dot_nt(q2, jnp.concatenate(keys, axis=0))
    scores, c0 = [], 0
    for j, kj in enumerate(keys):
        s = s_all[:, c0:c0 + kj.shape[0]]
        c0 += kj.shape[0]
        if biases[0][j] is not None:
            s = s + jnp.concatenate([biases[0][j], biases[1][j]], axis=0)
        scores.append(s)
    m = functools.reduce(jnp.maximum, [jnp.max(s, axis=-1, keepdims=True) for s in scores])
    e = jnp.concatenate([jnp.exp2(s - m).astype(MXU_DTYPE) for s in scores], axis=1)
    vcat = jnp.concatenate(values, axis=0)
    acc = _dot(e, jnp.concatenate([vcat, jnp.ones_like(vcat)], axis=1))
    out = acc[:, :LANES] / acc[:, LANES:]
    return jnp.where(_head_mask((m_rows, LANES), 0), out[:m_rows], out[m_rows:])


ATT_BATCH = 4


def _attn_body(ty_ref, kb_ref, q_ref, k0, k1, k2, v0, v1, v2, kc_ref, vc_ref, bias_ref, o_ref):
    del ty_ref, kb_ref
    for p in range(NA_HEADS // 2):
        cs = slice(p * LANES, (p + 1) * LANES)
        biases = [[bias_ref[2 * p + hh, :, j * ATT_TOK:(j + 1) * ATT_TOK] for j in range(3)] + [None]
                  for hh in range(2)]
        for e in range(q_ref.shape[0]):
            qp = q_ref[e, :, cs] * QK_SCALE
            keys = [k0[e, :, cs], k1[e, :, cs], k2[e, :, cs], kc_ref[e, :, cs]]
            values = [v0[e, :, cs], v1[e, :, cs], v2[e, :, cs], vc_ref[e, :, cs]]
            o_ref[e, :, cs] = _pair_attention(qp, keys, biases, values).astype(ACT_DTYPE)


def _neighborhood_attention(q, k, v, kc, vc, bias, type_of, key_start):
    b, l, _ = q.shape
    lc = kc.shape[1]
    n_blk = l // ATT_TOK
    nb = math.gcd(b, ATT_BATCH)
    tok = lambda off: pl.BlockSpec((nb, ATT_TOK, BRANCH_DIM), lambda r, bb, ty, kb: (bb, kb[r] + off, 0))
    ctx = pl.BlockSpec((nb, lc, BRANCH_DIM), lambda r, bb, ty, kb: (bb, 0, 0))
    grid_spec = pltpu.PrefetchScalarGridSpec(
        num_scalar_prefetch=2,
        grid=(n_blk, b // nb),
        in_specs=[
            pl.BlockSpec((nb, ATT_TOK, BRANCH_DIM), lambda r, bb, ty, kb: (bb, r, 0)),
            tok(0), tok(1), tok(2), tok(0), tok(1), tok(2), ctx, ctx,
            pl.BlockSpec((None, NA_HEADS, ATT_TOK, K_ROWS_PER_BLOCK * GRID_W), lambda r, bb, ty, kb: (ty[r], 0, 0, 0)),
        ],
        out_specs=pl.BlockSpec((nb, ATT_TOK, BRANCH_DIM), lambda r, bb, ty, kb: (bb, r, 0)),
    )
    return pl.pallas_call(
        _attn_body,
        grid_spec=grid_spec,
        out_shape=jax.ShapeDtypeStruct((b, l, BRANCH_DIM), ACT_DTYPE),
        compiler_params=_cparams(("arbitrary", "arbitrary")),
        name="neighborhood_attention",
    )(jnp.asarray(type_of, jnp.int32), jnp.asarray(key_start, jnp.int32), q, k, k, k, v, v, v, kc, vc, bias)


def _ctx_attn_body(q_ref, k_ref, v_ref, o_ref):
    for p in range(NA_HEADS // 2):
        cs = slice(p * LANES, (p + 1) * LANES)
        qp = q_ref[:, cs] * QK_SCALE
        o_ref[:, cs] = _pair_attention(qp, [k_ref[:, cs]], [[None], [None]], [v_ref[:, cs]]).astype(ACT_DTYPE)


def _context_attention(q, k, v):
    b, lc, _ = q.shape
    spec = pl.BlockSpec((None, lc, BRANCH_DIM), lambda bb: (bb, 0, 0))
    return pl.pallas_call(
        _ctx_attn_body,
        grid=(b,),
        in_specs=[spec, spec, spec],
        out_specs=spec,
        out_shape=jax.ShapeDtypeStruct((b, lc, BRANCH_DIM), ACT_DTYPE),
        compiler_params=_cparams(("arbitrary",)),
        name="context_attention",
    )(q, k, v)


HY_CW = 256
HY_TM = 512
HY_BLOCKS = 4
HY_MIN_BLOCK = 256
CONV_HALO = 16


def _hy_blocks(l):
    return HY_BLOCKS if l % (HY_BLOCKS * HY_MIN_BLOCK) == 0 else 1


@functools.lru_cache(maxsize=None)
def _dft_consts_np(m):
    n = 2 * m
    idx = np.arange(m, dtype=np.int64)
    ang = ((idx[:, None] * idx[None, :]) % n).astype(np.float64) * (2.0 * math.pi / n)
    cs = np.concatenate([np.cos(ang), -np.sin(ang)], axis=1).astype(np.float32)
    sign = np.where(idx % 2 == 0, 1.0, -1.0).astype(np.float32).reshape(m, 1)
    wf = np.where(idx == 0, 1.0 / n, 2.0 / n).astype(np.float32).reshape(m, 1)
    return cs, sign, wf


def _dft_consts(m):
    cs, sign, wf = _dft_consts_np(m)
    return jnp.asarray(cs.astype(MXU_DTYPE)), jnp.asarray(sign), jnp.asarray(wf)


def _filter_consts(l):
    t = np.linspace(0.0, 1.0, l, dtype=np.float32)[:, None]
    n_bands = (HY_EMB - 1) // 2
    bands = np.linspace(1e-4, n_bands - 1, n_bands, dtype=np.float32)[None, :]
    ang = np.float32(2.0 * math.pi) * np.arange(l, dtype=np.float32)[:, None] / np.float32(l) * bands
    z = np.concatenate([t, np.cos(ang), -np.sin(ang)], axis=-1).astype(np.float32)
    z = np.pad(z, ((0, 0), (0, LANES - HY_EMB)))
    max_decay = math.log(HY_DECAY_TARGET) / HY_FAST_DECAY_PCT
    min_decay = math.log(HY_DECAY_TARGET) / HY_SLOW_DECAY_PCT
    deltas = np.abs(np.linspace(min_decay, max_decay, HY_DIM, dtype=np.float32)).reshape(1, HY_DIM)
    return jnp.asarray(z), jnp.asarray(t), jnp.asarray(deltas)


def _filter_body(z_ref, w1_ref, b1_ref, w2_ref, b2_ref, fr_ref, w3a_ref, w3b_ref, t_ref, dl_ref,
                 sg_ref, wf_ref, cs_ref, gre_ref, gim_ref, gny_ref, h_ref, *, p):
    l = z_ref.shape[0]
    m = l // p

    @pl.when((pl.program_id(0) == 0) & (pl.program_id(1) == 0))
    def _():
        fr = fr_ref[...]
        h1 = jnp.sin(fr * (_dot(z_ref[...], w1_ref[...], precision=HIGHEST) + b1_ref[...]))
        h_ref[...] = jnp.sin(fr * (_dot(h1, w2_ref[...], precision=HIGHEST) + b2_ref[...]))

    h = h_ref[...]
    decay = jnp.exp(-t_ref[...] * dl_ref[...])
    h0 = _dot(h, w3a_ref[...], precision=HIGHEST) * decay
    h1 = _dot(h, w3b_ref[...], precision=HIGHEST) * decay
    row = lax.broadcasted_iota(jnp.int32, h1.shape, 0)
    den = (jnp.sum(jnp.abs(h0), axis=0, keepdims=True)
           + jnp.sum(jnp.where(row == 0, 0.0, jnp.abs(h1)), axis=0, keepdims=True))
    h0, h1 = h0 / den, h1 / den
    sg, wf = sg_ref[...], wf_ref[...]
    u, v, pc, q, un, pn, r0, r1 = [], [], [], [], [], [], [], []
    for b in range(p):
        x0, x1 = h0[b * m:(b + 1) * m], h1[b * m:(b + 1) * m]
        x0b, x1b = x0.astype(MXU_DTYPE), x1.astype(MXU_DTYPE)
        u.append(_dot(cs_ref[:, :m], x0b))
        v.append(_dot(cs_ref[:, m:], x0b))
        pc.append(_dot(cs_ref[:, :m], x1b))
        q.append(_dot(cs_ref[:, m:], x1b))
        un.append(jnp.sum(sg * x0, axis=0, keepdims=True))
        pn.append(jnp.sum(sg * x1, axis=0, keepdims=True))
        r0.append(x0[0:1])
        r1.append(x1[0:1])
    for li, d in enumerate(range(-(p - 1), p)):
        if d > 0:
            gre = u[d] + sg * (u[d - 1] - r0[d - 1])
            gim = v[d] + sg * v[d - 1]
            gny = un[d] + un[d - 1] - r0[d - 1]
        elif d == 0:
            gre = u[0] + pc[0] - r1[0]
            gim = v[0] - q[0]
            gny = un[0] + pn[0] - r1[0]
        else:
            gre = pc[-d] + sg * (pc[-d - 1] - r1[-d - 1])
            gim = -q[-d] - sg * q[-d - 1]
            gny = pn[-d] + pn[-d - 1] - r1[-d - 1]
        gre_ref[li] = (wf * gre).astype(gre_ref.dtype)
        gim_ref[li] = (wf * gim).astype(gim_ref.dtype)
        gny_ref[li] = gny * (1.0 / (2 * m))


def _hyena_filters(l, dft, fconsts, w1, b1, w2, b2, w3, freq):
    cs, sign, wf = dft
    z, t, deltas = fconsts
    p = _hy_blocks(l)
    m, nl = l // p, 2 * p - 1
    cw = HY_CW
    nb = HY_DIM // cw
    w1p = jnp.pad(w1, ((0, LANES - HY_EMB), (0, 0)))
    hid = HY_FILTER_HIDDEN
    small = lambda shape: pl.BlockSpec(shape, lambda n, cb: (0,) * len(shape))
    spec_out = pl.BlockSpec((None, nl, m, cw), lambda n, cb: (n, 0, 0, cb))
    return pl.pallas_call(
        functools.partial(_filter_body, p=p),
        grid=(HY_ORDER, nb),
        in_specs=[
            small((l, LANES)), small((LANES, hid)), small((1, hid)), small((hid, hid)), small((1, hid)), small((1, hid)),
            pl.BlockSpec((hid, cw), lambda n, cb: (0, n * 2 * nb + cb)),
            pl.BlockSpec((hid, cw), lambda n, cb: (0, n * 2 * nb + nb + cb)),
            small((l, 1)),
            pl.BlockSpec((1, cw), lambda n, cb: (0, cb)),
            small((m, 1)), small((m, 1)),
            _const_spec((m, 2 * m)),
        ],
        out_specs=[spec_out, spec_out, pl.BlockSpec((None, nl, 1, cw), lambda n, cb: (n, 0, 0, cb))],
        out_shape=[jax.ShapeDtypeStruct((HY_ORDER, nl, m, HY_DIM), ACT_DTYPE)] * 2
        + [jax.ShapeDtypeStruct((HY_ORDER, nl, 1, HY_DIM), F32)],
        scratch_shapes=[pltpu.VMEM((l, hid), F32)],
        compiler_params=_cparams(("arbitrary", "arbitrary")),
        name="hyena_filter_spectrum",
    )(z, w1p, b1.reshape(1, hid), w2, b2.reshape(1, hid), freq.reshape(1, hid), w3, w3, t, deltas, sign, wf, cs)


def _conv3_rows(p_ref, w_ref, b_ref, r0, rows):
    l = p_ref.shape[0]
    lo, hi = max(r0 - CONV_HALO, 0), min(r0 + rows + CONV_HALO, l)
    x = p_ref[lo:hi, :].astype(F32)
    n = hi - lo
    prev, nxt = pltpu.roll(x, 1, 0), pltpu.roll(x, n - 1, 0)
    row = lax.broadcasted_iota(jnp.int32, x.shape, 0)
    if lo == 0:
        prev = jnp.where(row == 0, 0.0, prev)
    if hi == l:
        nxt = jnp.where(row == n - 1, 0.0, nxt)
    y = prev * w_ref[0:1, :] + x * w_ref[1:2, :] + nxt * w_ref[2:3, :] + b_ref[...]
    return y[r0 - lo:r0 - lo + rows]


def _hyena_body(pv_ref, p1_ref, p2_ref, wv_ref, w1_ref, w2_ref, bv_ref, b1_ref, b2_ref, skip_ref,
                gre_ref, gim_ref, gny_ref, sg_ref, cs_ref, o_ref, zf_ref, zb_ref, zz_ref, y_ref, *, p):
    l, cw = pv_ref.shape
    m = l // p
    tm = min(HY_TM, m)
    offs = range(0, m, tm)
    zn = [jnp.zeros((1, cw), F32) for _ in range(p)]

    def put_z(a, c0, z):
        rs = slice(a * m + c0, a * m + c0 + tm)
        zf_ref[rs] = z
        zb_ref[rs] = z.astype(MXU_DTYPE)
        zn[a] = zn[a] + jnp.sum(sg_ref[c0:c0 + tm] * z, axis=0, keepdims=True)

    for a in range(p):
        for c0 in offs:
            put_z(a, c0, _conv3_rows(pv_ref, wv_ref, bv_ref, a * m + c0, tm))
    gates = ((p1_ref, w1_ref, b1_ref), (p2_ref, w2_ref, b2_ref))
    for n, (p_ref, w_ref, b_ref) in enumerate(gates):
        for a in range(p):
            zblk = zb_ref[a * m:(a + 1) * m]
            for c0 in offs:
                rs = slice(a * m + c0, a * m + c0 + tm)
                zz_ref[0, rs] = _dot(cs_ref[c0:c0 + tm, :m], zblk).astype(zz_ref.dtype)
                zz_ref[1, rs] = _dot(cs_ref[c0:c0 + tm, m:], zblk).astype(zz_ref.dtype)
        for a2 in range(p):
            for c0 in offs:
                yc = ys = None
                for a in range(p):
                    li = a2 - a + p - 1
                    gre, gim = gre_ref[n, li, c0:c0 + tm], gim_ref[n, li, c0:c0 + tm]
                    zc, zs = zz_ref[0, a * m + c0:a * m + c0 + tm], zz_ref[1, a * m + c0:a * m + c0 + tm]
                    tc, ts = zc * gre - zs * gim, zc * gim + zs * gre
                    yc, ys = (tc, ts) if yc is None else (yc + tc, ys + ts)
                y_ref[a2, c0:c0 + tm] = yc.astype(MXU_DTYPE)
                y_ref[a2, m + c0:m + c0 + tm] = ys.astype(MXU_DTYPE)
        nyq = [functools.reduce(lambda s, t: s + t, [zn[a] * gny_ref[n, a2 - a + p - 1] for a in range(p)])
               for a2 in range(p)]
        zn = [jnp.zeros((1, cw), F32) for _ in range(p)]
        for a2 in range(p):
            for c0 in offs:
                rs = slice(a2 * m + c0, a2 * m + c0 + tm)
                y = _dot(cs_ref[c0:c0 + tm, :], y_ref[a2]) + sg_ref[c0:c0 + tm] * nyq[a2]
                gate = _conv3_rows(p_ref, w_ref, b_ref, rs.start, tm)
                z = gate * (y + zf_ref[rs] * skip_ref[n:n + 1, :])
                if n + 1 < HY_ORDER:
                    put_z(a2, c0, z)
                else:
                    o_ref[rs] = z.astype(ACT_DTYPE)


def _hyena(x, conv_w, conv_b, skip, filters, dft):
    bx, l, _ = x.shape
    cs, sign, _ = dft
    gre, gim, gny = filters
    p = _hy_blocks(l)
    m, nl = l // p, 2 * p - 1
    cw = HY_CW
    nb = HY_DIM // cw
    pspec = lambda part: pl.BlockSpec((None, l, cw), lambda cb, b: (b, 0, part * nb + cb))
    wspec = lambda part: pl.BlockSpec((3, cw), lambda cb, b: (0, part * nb + cb))
    bspec = lambda part: pl.BlockSpec((1, cw), lambda cb, b: (0, part * nb + cb))
    fspec = pl.BlockSpec((HY_ORDER, nl, m, cw), lambda cb, b: (0, 0, 0, cb), pipeline_mode=pl.Buffered(1))
    return pl.pallas_call(
        functools.partial(_hyena_body, p=p),
        grid=(nb, bx),
        in_specs=[
            pspec(0), pspec(1), pspec(2), wspec(0), wspec(1), wspec(2), bspec(0), bspec(1), bspec(2),
            pl.BlockSpec((HY_ORDER, cw), lambda cb, b: (0, cb)),
            fspec, fspec,
            pl.BlockSpec((HY_ORDER, nl, 1, cw), lambda cb, b: (0, 0, 0, cb)),
            _const_spec((m, 1)), _const_spec((m, 2 * m)),
        ],
        out_specs=pl.BlockSpec((None, l, cw), lambda cb, b: (b, 0, cb)),
        out_shape=jax.ShapeDtypeStruct((bx, l, HY_DIM), ACT_DTYPE),
        scratch_shapes=[pltpu.VMEM((l, cw), F32), pltpu.VMEM((l, cw), MXU_DTYPE), pltpu.VMEM((2, l, cw), ACT_DTYPE),
                        pltpu.VMEM((p, 2 * m, cw), MXU_DTYPE)],
        compiler_params=_cparams(("arbitrary", "arbitrary")),
        name="hyena_long_conv",
    )(x, x, x, conv_w, conv_w, conv_w, conv_b.reshape(1, -1), conv_b.reshape(1, -1), conv_b.reshape(1, -1),
      skip, gre, gim, gny, sign, cs)


MERGE_TM = 1024


def _merge_body(x_ref, a_ref, at_ref, hy_ref, sh_ref, sc_ref, g1_ref, ng_ref, wg0_ref, wg1_ref, wg2_ref,
                wb_ref, wo_ref, o_ref):
    x = x_ref[...]
    ms = jnp.mean(x * x, axis=-1, keepdims=True)
    y = x * lax.rsqrt(ms + NORM_EPS) * ng_ref[...]
    h = (y * (1.0 + sc_ref[...]) + sh_ref[...]).astype(MXU_DTYPE)
    m = None
    for j, (br_ref, wg_ref) in enumerate(((a_ref, wg0_ref), (at_ref, wg1_ref), (hy_ref, wg2_ref))):
        term = jax.nn.sigmoid(_dot(h, wg_ref[...])) * _dot(br_ref[...], wb_ref[j])
        m = term if m is None else m + term
    o_ref[...] = x + g1_ref[...] * _dot(m.astype(MXU_DTYPE), wo_ref[...])


def _merge(x, a, attn, hy, mod, mod_row, norm_g, w_in, wb, wo, layer):
    bx, l, d = x.shape
    tm = min(MERGE_TM, l)
    tile = lambda w: pl.BlockSpec((None, tm, w), lambda b, i: (b, i, 0))
    gate_w = lambda j: pl.BlockSpec((None, d, d), lambda *_: (layer, 0, OFF_GATE // d + j), pipeline_mode=pl.Buffered(1))
    return pl.pallas_call(
        _merge_body,
        grid=(bx, l // tm),
        in_specs=[tile(d), tile(BRANCH_DIM), tile(BRANCH_DIM), tile(BRANCH_DIM),
                  _mod_spec(lambda b, i: mod_row(b), 0), _mod_spec(lambda b, i: mod_row(b), 1),
                  _mod_spec(lambda b, i: mod_row(b), 2), _layer_spec(norm_g, layer),
                  gate_w(0), gate_w(1), gate_w(2), _layer_spec(wb, layer), _layer_spec(wo, layer)],
        out_specs=tile(d),
        out_shape=jax.ShapeDtypeStruct((bx, l, d), F32),
        compiler_params=_cparams(("arbitrary", "arbitrary")),
        name="branch_merge",
    )(x, a, attn, hy, mod, mod, mod, norm_g, w_in, w_in, w_in, wb, wo)


FFN_TM = 1024
FFN_TN = 256


def _ffn_body(xp_ref, x_ref, xn_ref, sh_ref, sc_ref, g2_ref, ng_ref, wu_ref, cw_ref, cb_ref, wd_ref, fg_ref,
              o_ref, h_ref, gt_ref, *, final, seq_len):
    tm = x_ref.shape[0]
    halo = CONV_HALO
    n_ext = tm + 2 * halo

    def norm_mod(x):
        ms = jnp.mean(x * x, axis=-1, keepdims=True)
        y = x * lax.rsqrt(ms + NORM_EPS) * ng_ref[...]
        return (y * (1.0 + sc_ref[...]) + sh_ref[...]).astype(MXU_DTYPE)

    h_ref[0:halo] = norm_mod(xp_ref[...])
    h_ref[halo:halo + tm] = norm_mod(x_ref[...])
    h_ref[halo + tm:n_ext] = norm_mod(xn_ref[...])
    row = lax.broadcasted_iota(jnp.int32, (n_ext, 1), 0)
    pos = (pl.program_id(1) * tm + row + (seq_len - halo)) % seq_len
    first, last = pos == 0, pos == seq_len - 1
    for j in range(D_FF // FFN_TN):
        cs = slice(j * FFN_TN, (j + 1) * FFN_TN)
        a = _dot(h_ref[...], wu_ref[:, cs])
        prev = jnp.where(first, 0.0, pltpu.roll(a, 1, 0))
        nxt = jnp.where(last, 0.0, pltpu.roll(a, n_ext - 1, 0))
        a = prev * cw_ref[0:1, cs] + a * cw_ref[1:2, cs] + nxt * cw_ref[2:3, cs] + cb_ref[:, cs]
        gate = _dot(h_ref[halo:halo + tm], wu_ref[:, D_FF + j * FFN_TN:D_FF + (j + 1) * FFN_TN])
        gt_ref[:, cs] = (_gelu(a[halo:halo + tm]) * gate).astype(MXU_DTYPE)
    xn = x_ref[...] + g2_ref[...] * _dot(gt_ref[...], wd_ref[...])
    if final:
        ms = jnp.mean(xn * xn, axis=-1, keepdims=True)
        xn = xn * lax.rsqrt(ms + NORM_EPS) * fg_ref[...]
    o_ref[...] = xn


def _ffn(x, mod, mod_row, norm_g, w_up, conv_w, conv_b, w_down, layer, final_g, final, seq_len):
    bx, l, d = x.shape
    tm = min(FFN_TM, l)
    halo = CONV_HALO
    per, last_blk = tm // halo, l // halo - 1
    tile = pl.BlockSpec((None, tm, d), lambda b, i: (b, i, 0))
    return pl.pallas_call(
        functools.partial(_ffn_body, final=final, seq_len=seq_len),
        grid=(bx, l // tm),
        in_specs=[
            pl.BlockSpec((None, halo, d), lambda b, i: (b, jnp.maximum(i * per - 1, 0), 0)),
            tile,
            pl.BlockSpec((None, halo, d), lambda b, i: (b, jnp.minimum((i + 1) * per, last_blk), 0)),
            _mod_spec(lambda b, i: mod_row(b), 3), _mod_spec(lambda b, i: mod_row(b), 4),
            _mod_spec(lambda b, i: mod_row(b), 5),
            _layer_spec(norm_g, layer), _layer_spec(w_up, layer), _layer_spec(conv_w, layer),
            _layer_spec(conv_b, layer), _layer_spec(w_down, layer), _const_spec((1, d)),
        ],
        out_specs=tile,
        out_shape=jax.ShapeDtypeStruct((bx, l, d), F32),
        scratch_shapes=[pltpu.VMEM((tm + 2 * halo, d), MXU_DTYPE), pltpu.VMEM((tm, D_FF), MXU_DTYPE)],
        compiler_params=_cparams(("arbitrary", "arbitrary")),
        name="conv_ffn",
    )(x, x, x, mod, mod, mod, norm_g, w_up, conv_w, conv_b, w_down, final_g.reshape(1, d))


def _rope_tables(l):
    half = NA_HEAD_DIM // 2
    quarter = half // 2
    inv_freq = np.float32(ROPE_THETA) ** (-np.arange(quarter, dtype=np.float32) * np.float32(2.0) / np.float32(half))
    pos = np.arange(l)
    ang_r = (pos // GRID_W).astype(np.float32)[:, None] * inv_freq[None, :]
    ang_c = (pos % GRID_W).astype(np.float32)[:, None] * inv_freq[None, :]
    cos_h = np.concatenate([np.cos(ang_r), np.cos(ang_r), np.cos(ang_c), np.cos(ang_c)], axis=-1)
    sin_h = np.concatenate([-np.sin(ang_r), np.sin(ang_r), -np.sin(ang_c), np.sin(ang_c)], axis=-1)
    reps = LANES // NA_HEAD_DIM
    return (jnp.asarray(np.tile(cos_h, (1, reps)).astype(np.float32)),
            jnp.asarray(np.tile(sin_h, (1, reps)).astype(np.float32)))


def kernel(x, c, ctx, c_ctx, ada_w, ada_b, norm1_g, norm2_g, w_in, sgu_norm_g, sgu_w, sgu_b, na_rpb, hy_conv_w, hy_conv_b, hy_filt_w1, hy_filt_b1, hy_filt_w2, hy_filt_b2, hy_filt_w3, hy_sin_freq, hy_skip, w_branch, w_out, ffn_w_up, ffn_conv_w, ffn_conv_b, ffn_w_down, final_norm_g):
    b, l, d = x.shape
    lc = ctx.shape[1]
    assert d == D_MODEL and l % ATT_TOK == 0 and l % GRID_W == 0

    n_rows = -(-(b + 1) // 8) * 8
    cc = jnp.concatenate([c, c_ctx[None, :], jnp.zeros((n_rows - b - 1, d), F32)], axis=0)
    mod = _modulation(cc, ada_w, ada_b).reshape(DEPTH * n_rows, 1, N_MOD * d)

    rope_tabs = _rope_tables(l)
    dft_l, fc_l = _dft_consts(l // _hy_blocks(l)), _filter_consts(l)
    dft_c, fc_c = _dft_consts(lc // _hy_blocks(lc)), _filter_consts(lc)
    patterns, type_of, key_start = _att_plan(l // GRID_W)

    w_in_b, wb_b, wo_b = w_in.astype(MXU_DTYPE), w_branch.astype(MXU_DTYPE), w_out.astype(MXU_DTYPE)
    wup_b, wdn_b = ffn_w_up.astype(MXU_DTYPE), ffn_w_down.astype(MXU_DTYPE)
    n1g, n2g = norm1_g.reshape(DEPTH, 1, d), norm2_g.reshape(DEPTH, 1, d)
    ffn_cb = ffn_conv_b.reshape(DEPTH, 1, D_FF)
    sgu = (sgu_norm_g.reshape(DEPTH, 1, BRANCH_DIM), sgu_w.astype(MXU_DTYPE),
           sgu_b.reshape(DEPTH, SGU_GROUPS, SGU_CHUNK, 1))

    grp = math.gcd(b, max(1, l // lc))
    stack = lambda t: t.reshape(b // grp, grp * lc, t.shape[-1])
    unstack = lambda t: t.reshape(b, lc, t.shape[-1])

    xc = ctx
    for i in range(DEPTH):
        update_ctx = i < DEPTH - 1
        lat_row = lambda bb, i=i: i * n_rows + bb
        ctx_row = lambda bb, i=i: i * n_rows + b
        filt_args = (hy_filt_w1[i], hy_filt_b1[i], hy_filt_w2[i], hy_filt_b2[i], hy_filt_w3[i], hy_sin_freq[i])
        bias = _bias_tables(na_rpb[i], patterns)

        if update_ctx:
            a_c, q_c, k_c, v_c, hy_c = map(unstack, _in_proj(stack(xc), mod, ctx_row, n1g, w_in_b, i, FULL_SEGS, None,
                                                             sgu, "context_in_proj"))
        else:
            k_c, v_c = map(unstack, _in_proj(stack(xc), mod, ctx_row, n1g, w_in_b, i, KV_SEGS, None, None,
                                             "context_kv_proj"))

        a, q, k, v, hy = _in_proj(x, mod, lat_row, n1g, w_in_b, i, FULL_SEGS, rope_tabs, sgu, "latent_in_proj")
        attn = _neighborhood_attention(q, k, v, k_c, v_c, bias, type_of, key_start)
        filt = _hyena_filters(l, dft_l, fc_l, *filt_args)
        hyo = _hyena(hy, hy_conv_w[i], hy_conv_b[i], hy_skip[i], filt, dft_l)
        x = _merge(x, a, attn, hyo, mod, lat_row, n1g, w_in_b, wb_b, wo_b, i)
        x = _ffn(x, mod, lat_row, n2g, wup_b, ffn_conv_w, ffn_cb, wdn_b, i, final_norm_g,
                 final=not update_ctx, seq_len=l)

        if update_ctx:
            attn_c = _context_attention(q_c, k_c, v_c)
            filt_c = _hyena_filters(lc, dft_c, fc_c, *filt_args)
            hyo_c = _hyena(hy_c, hy_conv_w[i], hy_conv_b[i], hy_skip[i], filt_c, dft_c)
            xs = _merge(stack(xc), stack(a_c), stack(attn_c), stack(hyo_c), mod, ctx_row, n1g, w_in_b, wb_b, wo_b, i)
            xc = unstack(_ffn(xs, mod, ctx_row, n2g, wup_b, ffn_conv_w, ffn_cb, wdn_b, i, final_norm_g,
                              final=False, seq_len=lc))
    return x
```

```python
import functools
import math

import jax
import jax.numpy as jnp
import numpy as np
from jax import lax
from jax.experimental import pallas as pl
from jax.experimental.pallas import tpu as pltpu

D_MODEL = 1024
DEPTH = 2
GRID_W = 64
NORM_EPS = 1e-6
N_MOD = 6
BRANCH_DIM = D_MODEL // 2
SGU_GROUP_DIM = 128
SGU_GROUPS = BRANCH_DIM // SGU_GROUP_DIM
SGU_CHUNK = 128
NA_HEAD_DIM = 64
NA_HEADS = BRANCH_DIM // NA_HEAD_DIM
NA_WIN_ROWS = 8
NA_WIN_COLS = 16
ROPE_THETA = 10000.0
NEG_INF = -1e30
HY_DIM = BRANCH_DIM
HY_ORDER = 2
HY_EMB = 33
HY_FILTER_HIDDEN = 64
HY_FAST_DECAY_PCT = 0.3
HY_SLOW_DECAY_PCT = 1.5
HY_DECAY_TARGET = 1e-2
D_FF = 128 * ((8 * D_MODEL + 3 * 128 - 1) // (3 * 128))
OFF_SGU = 0
OFF_QKV = OFF_SGU + 2 * BRANCH_DIM
OFF_HY = OFF_QKV + 3 * BRANCH_DIM
OFF_GATE = OFF_HY + (HY_ORDER + 1) * HY_DIM
IN_COLS = OFF_GATE + 3 * D_MODEL

LANES = 128
MXU_WIDTH = 256
VMEM_LIMIT_BYTES = 56 * 1024 * 1024

MXU_DTYPE = jnp.bfloat16
ACT_DTYPE = jnp.bfloat16
F32 = jnp.float32
HIGHEST = lax.Precision.HIGHEST

Q_ROWS_PER_BLOCK = 4
K_ROWS_PER_BLOCK = 12
ATT_TOK = Q_ROWS_PER_BLOCK * GRID_W


def _cparams(sem):
    return pltpu.CompilerParams(dimension_semantics=sem, vmem_limit_bytes=VMEM_LIMIT_BYTES)


def _const_spec(shape):
    nd = len(shape)
    return pl.BlockSpec(shape, lambda *_: (0,) * nd, pipeline_mode=pl.Buffered(1))


def _layer_spec(arr, layer):
    nd = arr.ndim - 1
    return pl.BlockSpec((None,) + arr.shape[1:], lambda *_: (layer,) + (0,) * nd, pipeline_mode=pl.Buffered(1))


def _dot(a, b, **kw):
    return jnp.dot(a, b, preferred_element_type=F32, **kw)


def _dot_nt(a, b):
    return lax.dot_general(a, b, (((1,), (1,)), ((), ())), preferred_element_type=F32)


def _gelu(x):
    return 0.5 * x * (1.0 + lax.erf(x * (1.0 / math.sqrt(2.0))))


def _mod_body(c_ref, w_ref, b_ref, o_ref):
    c = c_ref[...]
    s = c * jax.nn.sigmoid(c)
    o_ref[...] = _dot(s, w_ref[...], precision=HIGHEST) + b_ref[...]


def _modulation(cc, ada_w, ada_b):
    depth, d, n = ada_w.shape
    r = cc.shape[0]
    tn = 1024
    return pl.pallas_call(
        _mod_body,
        grid=(depth, n // tn),
        in_specs=[
            pl.BlockSpec((r, d), lambda i, j: (0, 0)),
            pl.BlockSpec((None, d, tn), lambda i, j: (i, 0, j)),
            pl.BlockSpec((None, 1, tn), lambda i, j: (i, 0, j)),
        ],
        out_specs=pl.BlockSpec((None, r, tn), lambda i, j: (i, 0, j)),
        out_shape=jax.ShapeDtypeStruct((depth, r, n), F32),
        compiler_params=_cparams(("arbitrary", "arbitrary")),
        name="adaln_modulation",
    )(cc, ada_w, ada_b.reshape(depth, 1, n))


def _mod_spec(row_fn, chunk):
    return pl.BlockSpec((None, 1, D_MODEL), lambda *g: (row_fn(*g), 0, chunk))


def _rope(acc, cos, sin):
    lane = lax.broadcasted_iota(jnp.int32, cos.shape, 1)
    first_half = (lane % 32) < 16
    outs = []
    for c0 in range(0, acc.shape[1], LANES):
        xc = acc[:, c0:c0 + LANES]
        partner = jnp.where(first_half, pltpu.roll(xc, LANES - 16, 1), pltpu.roll(xc, 16, 1))
        outs.append(xc * cos + partner * sin)
    return outs


def _spatial_gating(u, v, g_ref, w_ref, b_ref, o_ref):
    ms = jnp.mean(v * v, axis=-1, keepdims=True)
    vb = (v * lax.rsqrt(ms + NORM_EPS) * g_ref[...]).astype(MXU_DTYPE)
    for n in range(u.shape[0] // SGU_CHUNK):
        rs = slice(n * SGU_CHUNK, (n + 1) * SGU_CHUNK)
        for g in range(SGU_GROUPS):
            cs = slice(g * SGU_GROUP_DIM, (g + 1) * SGU_GROUP_DIM)
            mixed = _dot(w_ref[g], vb[rs, cs]) + b_ref[g]
            o_ref[rs, cs] = (u[rs, cs] * mixed).astype(ACT_DTYPE)


def _in_proj_body(*refs, segs, rope, sgu, n_out):
    x_ref, sh_ref, sc_ref, g_ref, w_ref = refs[:5]
    pos = 5
    if rope:
        cos_ref, sin_ref = refs[pos:pos + 2]
        pos += 2
    if sgu:
        sg_ref, sw_ref, sb_ref = refs[pos:pos + 3]
        pos += 3
    out_refs = refs[pos:pos + n_out]
    x = x_ref[...]
    ms = jnp.mean(x * x, axis=-1, keepdims=True)
    y = x * lax.rsqrt(ms + NORM_EPS) * g_ref[...]
    h = (y * (1.0 + sc_ref[...]) + sh_ref[...]).astype(MXU_DTYPE)
    for (c0, width, kind, oi) in segs:
        if kind == "sgu":
            u = _gelu(_dot(h, w_ref[:, c0:c0 + BRANCH_DIM]))
            v = _gelu(_dot(h, w_ref[:, c0 + BRANCH_DIM:c0 + 2 * BRANCH_DIM]))
            _spatial_gating(u, v, sg_ref, sw_ref, sb_ref, out_refs[oi])
            continue
        step = min(width, 512)
        for cc in range(0, width, step):
            acc = _dot(h, w_ref[:, c0 + cc:c0 + cc + step])
            if kind == "rope" and rope:
                for k, piece in enumerate(_rope(acc, cos_ref[...], sin_ref[...])):
                    out_refs[oi][:, cc + k * LANES:cc + (k + 1) * LANES] = piece.astype(ACT_DTYPE)
            else:
                out_refs[oi][:, cc:cc + step] = acc.astype(ACT_DTYPE)


def _in_proj(x, mod, mod_row, norm_g, w, layer, segs, rope_tabs, sgu, name):
    bx, l, d = x.shape
    tm = min(IN_PROJ_TM, l)
    assert all(s[0] + s[1] <= OFF_GATE for s in segs)
    n_out = max(s[3] for s in segs) + 1
    out_w = lambda s: BRANCH_DIM if s[2] == "sgu" else s[1]
    widths = [sum(out_w(s) for s in segs if s[3] == oi) for oi in range(n_out)]
    rope = rope_tabs is not None
    use_sgu = any(s[2] == "sgu" for s in segs)
    in_specs = [
        pl.BlockSpec((None, tm, d), lambda b, i: (b, i, 0)),
        _mod_spec(lambda b, i: mod_row(b), 0),
        _mod_spec(lambda b, i: mod_row(b), 1),
        _layer_spec(norm_g, layer),
        pl.BlockSpec((None, d, OFF_GATE), lambda *_: (layer, 0, 0), pipeline_mode=pl.Buffered(1)),
    ]
    args = [x, mod, mod, norm_g, w]
    if rope:
        in_specs += [pl.BlockSpec((tm, LANES), lambda b, i: (i, 0))] * 2
        args += list(rope_tabs)
    if use_sgu:
        in_specs += [_layer_spec(p, layer) for p in sgu]
        args += list(sgu)
    return pl.pallas_call(
        functools.partial(_in_proj_body, segs=segs, rope=rope, sgu=use_sgu, n_out=n_out),
        grid=(bx, l // tm),
        in_specs=in_specs,
        out_specs=[pl.BlockSpec((None, tm, wd), lambda b, i: (b, i, 0)) for wd in widths],
        out_shape=[jax.ShapeDtypeStruct((bx, l, wd), ACT_DTYPE) for wd in widths],
        compiler_params=_cparams(("arbitrary", "arbitrary")),
        name=name,
    )(*args)


IN_PROJ_TM = 1024
FULL_SEGS = (
    (OFF_SGU, 2 * BRANCH_DIM, "sgu", 0),
    (OFF_QKV, BRANCH_DIM, "rope", 1),
    (OFF_QKV + BRANCH_DIM, BRANCH_DIM, "rope", 2),
    (OFF_QKV + 2 * BRANCH_DIM, BRANCH_DIM, "plain", 3),
    (OFF_HY, 3 * HY_DIM, "plain", 4),
)
KV_SEGS = ((OFF_QKV + BRANCH_DIM, BRANCH_DIM, "plain", 0), (OFF_QKV + 2 * BRANCH_DIM, BRANCH_DIM, "plain", 1))


def _att_plan(rows):
    kr = min(NA_WIN_ROWS, rows)
    assert rows % Q_ROWS_PER_BLOCK == 0 and rows >= K_ROWS_PER_BLOCK and kr == NA_WIN_ROWS
    patterns, type_of, key_start = [], [], []
    for rb in range(0, rows, Q_ROWS_PER_BLOCK):
        ks = min(max(rb - NA_WIN_ROWS // 2, 0), rows - K_ROWS_PER_BLOCK)
        assert ks % Q_ROWS_PER_BLOCK == 0
        pat = []
        for qi in range(Q_ROWS_PER_BLOCK):
            rq = rb + qi
            r0 = min(max(rq - kr // 2, 0), rows - kr)
            for j in range(K_ROWS_PER_BLOCK):
                rk = ks + j
                pat.append(rk - rq + NA_WIN_ROWS - 1 if r0 <= rk < r0 + kr else None)
        pat = tuple(pat)
        if pat not in patterns:
            patterns.append(pat)
        type_of.append(patterns.index(pat))
        key_start.append(ks // Q_ROWS_PER_BLOCK)
    return patterns, type_of, key_start


def _bias_body(rpb_ref, o_ref, *, patterns):
    n_dr, n_dc = 2 * NA_WIN_ROWS - 1, 2 * NA_WIN_COLS - 1
    h = pl.program_id(0)
    qc = lax.broadcasted_iota(jnp.int32, (GRID_W, GRID_W), 0)
    kc = lax.broadcasted_iota(jnp.int32, (GRID_W, GRID_W), 1)
    cstart = jnp.clip(qc - NA_WIN_COLS // 2, 0, GRID_W - NA_WIN_COLS)
    col_ok = (kc >= cstart) & (kc < cstart + NA_WIN_COLS)
    dc = jnp.clip(kc - qc + NA_WIN_COLS - 1, 0, n_dc - 1)
    neg = jnp.full((GRID_W, GRID_W), NEG_INF, F32)
    tiles = []
    for dr in range(n_dr):
        t = jnp.zeros((GRID_W, GRID_W), F32)
        for d in range(n_dc):
            t = jnp.where(dc == d, rpb_ref[(h * n_dr + dr) * n_dc + d], t)
        tiles.append(jnp.where(col_ok, t * LOG2E, neg))
    for ty, pat in enumerate(patterns):
        for qi in range(Q_ROWS_PER_BLOCK):
            for j in range(K_ROWS_PER_BLOCK):
                dr = pat[qi * K_ROWS_PER_BLOCK + j]
                o_ref[ty, qi * GRID_W:(qi + 1) * GRID_W, j * GRID_W:(j + 1) * GRID_W] = neg if dr is None else tiles[dr]


def _bias_tables(rpb, patterns):
    nt = len(patterns)
    return pl.pallas_call(
        functools.partial(_bias_body, patterns=patterns),
        grid=(NA_HEADS,),
        in_specs=[pl.BlockSpec(memory_space=pltpu.SMEM)],
        out_specs=pl.BlockSpec((nt, None, ATT_TOK, K_ROWS_PER_BLOCK * GRID_W), lambda h: (0, h, 0, 0)),
        out_shape=jax.ShapeDtypeStruct((nt, NA_HEADS, ATT_TOK, K_ROWS_PER_BLOCK * GRID_W), F32),
        compiler_params=_cparams(("arbitrary",)),
        name="attention_bias_tables",
    )(rpb.reshape(-1))


def _head_mask(shape, hh):
    lane = lax.broadcasted_iota(jnp.int32, shape, 1)
    return (lane >= hh * NA_HEAD_DIM) & (lane < (hh + 1) * NA_HEAD_DIM)


LOG2E = math.log2(math.e)
QK_SCALE = NA_HEAD_DIM ** -0.5 * LOG2E


def _pair_attention(qp, keys, biases, values):
    m_rows = qp.shape[0]
    q2 = jnp.concatenate([jnp.where(_head_mask(qp.shape, hh), qp, jnp.zeros_like(qp)) for hh in range(2)], axis=0)
    s_all = _dot_nt(q2, jnp.concatenate(keys, axis=0))
    scores, c0 = [], 0
    for j, kj in enumerate(keys):
        s = s_all[:, c0:c0 + kj.shape[0]]
        c0 += kj.shape[0]
        if biases[0][j] is not None:
            s = s + jnp.concatenate([biases[0][j], biases[1][j]], axis=0)
        scores.append(s)
    m = functools.reduce(jnp.maximum, [jnp.max(s, axis=-1, keepdims=True) for s in scores])
    e = jnp.concatenate([jnp.exp2(s - m).astype(MXU_DTYPE) for s in scores], axis=1)
    vcat = jnp.concatenate(values, axis=0)
    acc = _dot(e, jnp.concatenate([vcat, jnp.ones_like(vcat)], axis=1))
    out = acc[:, :LANES] / acc[:, LANES:]
    return jnp.where(_head_mask((m_rows, LANES), 0), out[:m_rows], out[m_rows:])


ATT_BATCH = 8


def _attn_body(ty_ref, kb_ref, q_ref, k0, k1, k2, v0, v1, v2, kc_ref, vc_ref, bias_ref, o_ref):
    del ty_ref, kb_ref
    for p in range(NA_HEADS // 2):
        cs = slice(p * LANES, (p + 1) * LANES)
        biases = [[bias_ref[2 * p + hh, :, j * ATT_TOK:(j + 1) * ATT_TOK] for j in range(3)] + [None]
                  for hh in range(2)]
        for e in range(q_ref.shape[0]):
            qp = q_ref[e, :, cs] * QK_SCALE
            keys = [k0[e, :, cs], k1[e, :, cs], k2[e, :, cs], kc_ref[e, :, cs]]
            values = [v0[e, :, cs], v1[e, :, cs], v2[e, :, cs], vc_ref[e, :, cs]]
            o_ref[e, :, cs] = _pair_attention(qp, keys, biases, values).astype(ACT_DTYPE)


def _neighborhood_attention(q, k, v, kc, vc, bias, type_of, key_start):
    b, l, _ = q.shape
    lc = kc.shape[1]
    n_blk = l // ATT_TOK
    nb = math.gcd(b, ATT_BATCH)
    tok = lambda off: pl.BlockSpec((nb, ATT_TOK, BRANCH_DIM), lambda r, bb, ty, kb: (bb, kb[r] + off, 0))
    ctx = pl.BlockSpec((nb, lc, BRANCH_DIM), lambda r, bb, ty, kb: (bb, 0, 0))
    grid_spec = pltpu.PrefetchScalarGridSpec(
        num_scalar_prefetch=2,
        grid=(n_blk, b // nb),
        in_specs=[
            pl.BlockSpec((nb, ATT_TOK, BRANCH_DIM), lambda r, bb, ty, kb: (bb, r, 0)),
            tok(0), tok(1), tok(2), tok(0), tok(1), tok(2), ctx, ctx,
            pl.BlockSpec((None, NA_HEADS, ATT_TOK, K_ROWS_PER_BLOCK * GRID_W), lambda r, bb, ty, kb: (ty[r], 0, 0, 0)),
        ],
        out_specs=pl.BlockSpec((nb, ATT_TOK, BRANCH_DIM), lambda r, bb, ty, kb: (bb, r, 0)),
    )
    return pl.pallas_call(
        _attn_body,
        grid_spec=grid_spec,
        out_shape=jax.ShapeDtypeStruct((b, l, BRANCH_DIM), ACT_DTYPE),
        compiler_params=_cparams(("arbitrary", "arbitrary")),
        name="neighborhood_attention",
    )(jnp.asarray(type_of, jnp.int32), jnp.asarray(key_start, jnp.int32), q, k, k, k, v, v, v, kc, vc, bias)


def _ctx_attn_body(q_ref, k_ref, v_ref, o_ref):
    for p in range(NA_HEADS // 2):
        cs = slice(p * LANES, (p + 1) * LANES)
        qp = q_ref[:, cs] * QK_SCALE
        o_ref[:, cs] = _pair_attention(qp, [k_ref[:, cs]], [[None], [None]], [v_ref[:, cs]]).astype(ACT_DTYPE)


def _context_attention(q, k, v):
    b, lc, _ = q.shape
    spec = pl.BlockSpec((None, lc, BRANCH_DIM), lambda bb: (bb, 0, 0))
    return pl.pallas_call(
        _ctx_attn_body,
        grid=(b,),
        in_specs=[spec, spec, spec],
        out_specs=spec,
        out_shape=jax.ShapeDtypeStruct((b, lc, BRANCH_DIM), ACT_DTYPE),
        compiler_params=_cparams(("arbitrary",)),
        name="context_attention",
    )(q, k, v)


HY_CW = 256
HY_TM = 512
HY_BLOCKS = 4
HY_MIN_BLOCK = 256
CONV_HALO = 16


def _hy_blocks(l):
    return HY_BLOCKS if l % (HY_BLOCKS * HY_MIN_BLOCK) == 0 else 1


@functools.lru_cache(maxsize=None)
def _dft_consts_np(m):
    n = 2 * m
    idx = np.arange(m, dtype=np.int64)
    ang = ((idx[:, None] * idx[None, :]) % n).astype(np.float64) * (2.0 * math.pi / n)
    cs = np.concatenate([np.cos(ang), -np.sin(ang)], axis=1).astype(np.float32)
    sign = np.where(idx % 2 == 0, 1.0, -1.0).astype(np.float32).reshape(m, 1)
    wf = np.where(idx == 0, 1.0 / n, 2.0 / n).astype(np.float32).reshape(m, 1)
    return cs, sign, wf


def _dft_consts(m):
    cs, sign, wf = _dft_consts_np(m)
    return jnp.asarray(cs.astype(MXU_DTYPE)), jnp.asarray(sign), jnp.asarray(wf)


def _filter_consts(l):
    t = np.linspace(0.0, 1.0, l, dtype=np.float32)[:, None]
    n_bands = (HY_EMB - 1) // 2
    bands = np.linspace(1e-4, n_bands - 1, n_bands, dtype=np.float32)[None, :]
    ang = np.float32(2.0 * math.pi) * np.arange(l, dtype=np.float32)[:, None] / np.float32(l) * bands
    z = np.concatenate([t, np.cos(ang), -np.sin(ang)], axis=-1).astype(np.float32)
    z = np.pad(z, ((0, 0), (0, LANES - HY_EMB)))
    max_decay = math.log(HY_DECAY_TARGET) / HY_FAST_DECAY_PCT
    min_decay = math.log(HY_DECAY_TARGET) / HY_SLOW_DECAY_PCT
    deltas = np.abs(np.linspace(min_decay, max_decay, HY_DIM, dtype=np.float32)).reshape(1, HY_DIM)
    return jnp.asarray(z), jnp.asarray(t), jnp.asarray(deltas)


def _filter_body(z_ref, w1_ref, b1_ref, w2_ref, b2_ref, fr_ref, w3a_ref, w3b_ref, t_ref, dl_ref,
                 sg_ref, wf_ref, cs_ref, gre_ref, gim_ref, gny_ref, h_ref, *, p):
    l = z_ref.shape[0]
    m = l // p

    @pl.when((pl.program_id(0) == 0) & (pl.program_id(1) == 0))
    def _():
        fr = fr_ref[...]
        h1 = jnp.sin(fr * (_dot(z_ref[...], w1_ref[...], precision=HIGHEST) + b1_ref[...]))
        h_ref[...] = jnp.sin(fr * (_dot(h1, w2_ref[...], precision=HIGHEST) + b2_ref[...]))

    h = h_ref[...]
    decay = jnp.exp(-t_ref[...] * dl_ref[...])
    h0 = _dot(h, w3a_ref[...], precision=HIGHEST) * decay
    h1 = _dot(h, w3b_ref[...], precision=HIGHEST) * decay
    row = lax.broadcasted_iota(jnp.int32, h1.shape, 0)
    den = (jnp.sum(jnp.abs(h0), axis=0, keepdims=True)
           + jnp.sum(jnp.where(row == 0, 0.0, jnp.abs(h1)), axis=0, keepdims=True))
    h0, h1 = h0 / den, h1 / den
    sg, wf = sg_ref[...], wf_ref[...]
    u, v, pc, q, un, pn, r0, r1 = [], [], [], [], [], [], [], []
    for b in range(p):
        x0, x1 = h0[b * m:(b + 1) * m], h1[b * m:(b + 1) * m]
        x0b, x1b = x0.astype(MXU_DTYPE), x1.astype(MXU_DTYPE)
        u.append(_dot(cs_ref[:, :m], x0b))
        v.append(_dot(cs_ref[:, m:], x0b))
        pc.append(_dot(cs_ref[:, :m], x1b))
        q.append(_dot(cs_ref[:, m:], x1b))
        un.append(jnp.sum(sg * x0, axis=0, keepdims=True))
        pn.append(jnp.sum(sg * x1, axis=0, keepdims=True))
        r0.append(x0[0:1])
        r1.append(x1[0:1])
    for li, d in enumerate(range(-(p - 1), p)):
        if d > 0:
            gre = u[d] + sg * (u[d - 1] - r0[d - 1])
            gim = v[d] + sg * v[d - 1]
            gny = un[d] + un[d - 1] - r0[d - 1]
        elif d == 0:
            gre = u[0] + pc[0] - r1[0]
            gim = v[0] - q[0]
            gny = un[0] + pn[0] - r1[0]
        else:
            gre = pc[-d] + sg * (pc[-d - 1] - r1[-d - 1])
            gim = -q[-d] - sg * q[-d - 1]
            gny = pn[-d] + pn[-d - 1] - r1[-d - 1]
        gre_ref[li] = (wf * gre).astype(gre_ref.dtype)
        gim_ref[li] = (wf * gim).astype(gim_ref.dtype)
        gny_ref[li] = gny * (1.0 / (2 * m))


def _hyena_filters(l, dft, fconsts, w1, b1, w2, b2, w3, freq):
    cs, sign, wf = dft
    z, t, deltas = fconsts
    p = _hy_blocks(l)
    m, nl = l // p, 2 * p - 1
    cw = HY_CW
    nb = HY_DIM // cw
    w1p = jnp.pad(w1, ((0, LANES - HY_EMB), (0, 0)))
    hid = HY_FILTER_HIDDEN
    small = lambda shape: pl.BlockSpec(shape, lambda n, cb: (0,) * len(shape))
    spec_out = pl.BlockSpec((None, nl, m, cw), lambda n, cb: (n, 0, 0, cb))
    return pl.pallas_call(
        functools.partial(_filter_body, p=p),
        grid=(HY_ORDER, nb),
        in_specs=[
            small((l, LANES)), small((LANES, hid)), small((1, hid)), small((hid, hid)), small((1, hid)), small((1, hid)),
            pl.BlockSpec((hid, cw), lambda n, cb: (0, n * 2 * nb + cb)),
            pl.BlockSpec((hid, cw), lambda n, cb: (0, n * 2 * nb + nb + cb)),
            small((l, 1)),
            pl.BlockSpec((1, cw), lambda n, cb: (0, cb)),
            small((m, 1)), small((m, 1)),
            _const_spec((m, 2 * m)),
        ],
        out_specs=[spec_out, spec_out, pl.BlockSpec((None, nl, 1, cw), lambda n, cb: (n, 0, 0, cb))],
        out_shape=[jax.ShapeDtypeStruct((HY_ORDER, nl, m, HY_DIM), ACT_DTYPE)] * 2
        + [jax.ShapeDtypeStruct((HY_ORDER, nl, 1, HY_DIM), F32)],
        scratch_shapes=[pltpu.VMEM((l, hid), F32)],
        compiler_params=_cparams(("arbitrary", "arbitrary")),
        name="hyena_filter_spectrum",
    )(z, w1p, b1.reshape(1, hid), w2, b2.reshape(1, hid), freq.reshape(1, hid), w3, w3, t, deltas, sign, wf, cs)


def _conv3_rows(p_ref, w_ref, b_ref, r0, rows):
    l = p_ref.shape[0]
    lo, hi = max(r0 - CONV_HALO, 0), min(r0 + rows + CONV_HALO, l)
    x = p_ref[lo:hi, :].astype(F32)
    n = hi - lo
    prev, nxt = pltpu.roll(x, 1, 0), pltpu.roll(x, n - 1, 0)
    row = lax.broadcasted_iota(jnp.int32, x.shape, 0)
    if lo == 0:
        prev = jnp.where(row == 0, 0.0, prev)
    if hi == l:
        nxt = jnp.where(row == n - 1, 0.0, nxt)
    y = prev * w_ref[0:1, :] + x * w_ref[1:2, :] + nxt * w_ref[2:3, :] + b_ref[...]
    return y[r0 - lo:r0 - lo + rows]


def _hyena_body(pv_ref, p1_ref, p2_ref, wv_ref, w1_ref, w2_ref, bv_ref, b1_ref, b2_ref, skip_ref,
                gre_ref, gim_ref, gny_ref, sg_ref, cs_ref, o_ref, zf_ref, zb_ref, zz_ref, y_ref, *, p):
    l, cw = pv_ref.shape
    m = l // p
    tm = min(HY_TM, m)
    offs = range(0, m, tm)
    zn = [jnp.zeros((1, cw), F32) for _ in range(p)]

    def put_z(a, c0, z):
        rs = slice(a * m + c0, a * m + c0 + tm)
        zf_ref[rs] = z
        zb_ref[rs] = z.astype(MXU_DTYPE)
        zn[a] = zn[a] + jnp.sum(sg_ref[c0:c0 + tm] * z, axis=0, keepdims=True)

    for a in range(p):
        for c0 in offs:
            put_z(a, c0, _conv3_rows(pv_ref, wv_ref, bv_ref, a * m + c0, tm))
    gates = ((p1_ref, w1_ref, b1_ref), (p2_ref, w2_ref, b2_ref))
    for n, (p_ref, w_ref, b_ref) in enumerate(gates):
        for a in range(p):
            zblk = zb_ref[a * m:(a + 1) * m]
            for c0 in offs:
                rs = slice(a * m + c0, a * m + c0 + tm)
                zz_ref[0, rs] = _dot(cs_ref[c0:c0 + tm, :m], zblk).astype(zz_ref.dtype)
                zz_ref[1, rs] = _dot(cs_ref[c0:c0 + tm, m:], zblk).astype(zz_ref.dtype)
        for a2 in range(p):
            for c0 in offs:
                yc = ys = None
                for a in range(p):
                    li = a2 - a + p - 1
                    gre, gim = gre_ref[n, li, c0:c0 + tm], gim_ref[n, li, c0:c0 + tm]
                    zc, zs = zz_ref[0, a * m + c0:a * m + c0 + tm], zz_ref[1, a * m + c0:a * m + c0 + tm]
                    tc, ts = zc * gre - zs * gim, zc * gim + zs * gre
                    yc, ys = (tc, ts) if yc is None else (yc + tc, ys + ts)
                y_ref[a2, c0:c0 + tm] = yc.astype(MXU_DTYPE)
                y_ref[a2, m + c0:m + c0 + tm] = ys.astype(MXU_DTYPE)
        nyq = [functools.reduce(lambda s, t: s + t, [zn[a] * gny_ref[n, a2 - a + p - 1] for a in range(p)])
               for a2 in range(p)]
        zn = [jnp.zeros((1, cw), F32) for _ in range(p)]
        for a2 in range(p):
            for c0 in offs:
                rs = slice(a2 * m + c0, a2 * m + c0 + tm)
                y = _dot(cs_ref[c0:c0 + tm, :], y_ref[a2]) + sg_ref[c0:c0 + tm] * nyq[a2]
                gate = _conv3_rows(p_ref, w_ref, b_ref, rs.start, tm)
                z = gate * (y + zf_ref[rs] * skip_ref[n:n + 1, :])
                if n + 1 < HY_ORDER:
                    put_z(a2, c0, z)
                else:
                    o_ref[rs] = z.astype(ACT_DTYPE)


def _hyena(x, conv_w, conv_b, skip, filters, dft):
    bx, l, _ = x.shape
    cs, sign, _ = dft
    gre, gim, gny = filters
    p = _hy_blocks(l)
    m, nl = l // p, 2 * p - 1
    cw = HY_CW
    nb = HY_DIM // cw
    pspec = lambda part: pl.BlockSpec((None, l, cw), lambda cb, b: (b, 0, part * nb + cb))
    wspec = lambda part: pl.BlockSpec((3, cw), lambda cb, b: (0, part * nb + cb))
    bspec = lambda part: pl.BlockSpec((1, cw), lambda cb, b: (0, part * nb + cb))
    fspec = pl.BlockSpec((HY_ORDER, nl, m, cw), lambda cb, b: (0, 0, 0, cb), pipeline_mode=pl.Buffered(1))
    return pl.pallas_call(
        functools.partial(_hyena_body, p=p),
        grid=(nb, bx),
        in_specs=[
            pspec(0), pspec(1), pspec(2), wspec(0), wspec(1), wspec(2), bspec(0), bspec(1), bspec(2),
            pl.BlockSpec((HY_ORDER, cw), lambda cb, b: (0, cb)),
            fspec, fspec,
            pl.BlockSpec((HY_ORDER, nl, 1, cw), lambda cb, b: (0, 0, 0, cb)),
            _const_spec((m, 1)), _const_spec((m, 2 * m)),
        ],
        out_specs=pl.BlockSpec((None, l, cw), lambda cb, b: (b, 0, cb)),
        out_shape=jax.ShapeDtypeStruct((bx, l, HY_DIM), ACT_DTYPE),
        scratch_shapes=[pltpu.VMEM((l, cw), F32), pltpu.VMEM((l, cw), MXU_DTYPE), pltpu.VMEM((2, l, cw), ACT_DTYPE),
                        pltpu.VMEM((p, 2 * m, cw), MXU_DTYPE)],
        compiler_params=_cparams(("arbitrary", "arbitrary")),
        name="hyena_long_conv",
    )(x, x, x, conv_w, conv_w, conv_w, conv_b.reshape(1, -1), conv_b.reshape(1, -1), conv_b.reshape(1, -1),
      skip, gre, gim, gny, sign, cs)


MERGE_TM = 1024


def _merge_body(x_ref, a_ref, at_ref, hy_ref, sh_ref, sc_ref, g1_ref, ng_ref, wg0_ref, wg1_ref, wg2_ref,
                wb_ref, wo_ref, o_ref):
    x = x_ref[...]
    ms = jnp.mean(x * x, axis=-1, keepdims=True)
    y = x * lax.rsqrt(ms + NORM_EPS) * ng_ref[...]
    h = (y * (1.0 + sc_ref[...]) + sh_ref[...]).astype(MXU_DTYPE)
    m = None
    for j, (br_ref, wg_ref) in enumerate(((a_ref, wg0_ref), (at_ref, wg1_ref), (hy_ref, wg2_ref))):
        term = jax.nn.sigmoid(_dot(h, wg_ref[...])) * _dot(br_ref[...], wb_ref[j])
        m = term if m is None else m + term
    o_ref[...] = x + g1_ref[...] * _dot(m.astype(MXU_DTYPE), wo_ref[...])


def _merge(x, a, attn, hy, mod, mod_row, norm_g, w_in, wb, wo, layer):
    bx, l, d = x.shape
    tm = min(MERGE_TM, l)
    tile = lambda w: pl.BlockSpec((None, tm, w), lambda b, i: (b, i, 0))
    gate_w = lambda j: pl.BlockSpec((None, d, d), lambda *_: (layer, 0, OFF_GATE // d + j), pipeline_mode=pl.Buffered(1))
    return pl.pallas_call(
        _merge_body,
        grid=(bx, l // tm),
        in_specs=[tile(d), tile(BRANCH_DIM), tile(BRANCH_DIM), tile(BRANCH_DIM),
                  _mod_spec(lambda b, i: mod_row(b), 0), _mod_spec(lambda b, i: mod_row(b), 1),
                  _mod_spec(lambda b, i: mod_row(b), 2), _layer_spec(norm_g, layer),
                  gate_w(0), gate_w(1), gate_w(2), _layer_spec(wb, layer), _layer_spec(wo, layer)],
        out_specs=tile(d),
        out_shape=jax.ShapeDtypeStruct((bx, l, d), F32),
        compiler_params=_cparams(("arbitrary", "arbitrary")),
        name="branch_merge",
    )(x, a, attn, hy, mod, mod, mod, norm_g, w_in, w_in, w_in, wb, wo)


FFN_TM = 1024
FFN_TN = 256


def _ffn_body(xp_ref, x_ref, xn_ref, sh_ref, sc_ref, g2_ref, ng_ref, wu_ref, cw_ref, cb_ref, wd_ref, fg_ref,
              o_ref, h_ref, gt_ref, *, final, seq_len):
    tm = x_ref.shape[0]
    halo = CONV_HALO
    n_ext = tm + 2 * halo

    def norm_mod(x):
        ms = jnp.mean(x * x, axis=-1, keepdims=True)
        y = x * lax.rsqrt(ms + NORM_EPS) * ng_ref[...]
        return (y * (1.0 + sc_ref[...]) + sh_ref[...]).astype(MXU_DTYPE)

    h_ref[0:halo] = norm_mod(xp_ref[...])
    h_ref[halo:halo + tm] = norm_mod(x_ref[...])
    h_ref[halo + tm:n_ext] = norm_mod(xn_ref[...])
    row = lax.broadcasted_iota(jnp.int32, (n_ext, 1), 0)
    pos = (pl.program_id(1) * tm + row + (seq_len - halo)) % seq_len
    first, last = pos == 0, pos == seq_len - 1
    for j in range(D_FF // FFN_TN):
        cs = slice(j * FFN_TN, (j + 1) * FFN_TN)
        a = _dot(h_ref[...], wu_ref[:, cs])
        prev = jnp.where(first, 0.0, pltpu.roll(a, 1, 0))
        nxt = jnp.where(last, 0.0, pltpu.roll(a, n_ext - 1, 0))
        a = prev * cw_ref[0:1, cs] + a * cw_ref[1:2, cs] + nxt * cw_ref[2:3, cs] + cb_ref[:, cs]
        gate = _dot(h_ref[halo:halo + tm], wu_ref[:, D_FF + j * FFN_TN:D_FF + (j + 1) * FFN_TN])
        gt_ref[:, cs] = (_gelu(a[halo:halo + tm]) * gate).astype(MXU_DTYPE)
    xn = x_ref[...] + g2_ref[...] * _dot(gt_ref[...], wd_ref[...])
    if final:
        ms = jnp.mean(xn * xn, axis=-1, keepdims=True)
        xn = xn * lax.rsqrt(ms + NORM_EPS) * fg_ref[...]
    o_ref[...] = xn


def _ffn(x, mod, mod_row, norm_g, w_up, conv_w, conv_b, w_down, layer, final_g, final, seq_len):
    bx, l, d = x.shape
    tm = min(FFN_TM, l)
    halo = CONV_HALO
    per, last_blk = tm // halo, l // halo - 1
    tile = pl.BlockSpec((None, tm, d), lambda b, i: (b, i, 0))
    return pl.pallas_call(
        functools.partial(_ffn_body, final=final, seq_len=seq_len),
        grid=(bx, l // tm),
        in_specs=[
            pl.BlockSpec((None, halo, d), lambda b, i: (b, jnp.maximum(i * per - 1, 0), 0)),
            tile,
            pl.BlockSpec((None, halo, d), lambda b, i: (b, jnp.minimum((i + 1) * per, last_blk), 0)),
            _mod_spec(lambda b, i: mod_row(b), 3), _mod_spec(lambda b, i: mod_row(b), 4),
            _mod_spec(lambda b, i: mod_row(b), 5),
            _layer_spec(norm_g, layer), _layer_spec(w_up, layer), _layer_spec(conv_w, layer),
            _layer_spec(conv_b, layer), _layer_spec(w_down, layer), _const_spec((1, d)),
        ],
        out_specs=tile,
        out_shape=jax.ShapeDtypeStruct((bx, l, d), F32),
        scratch_shapes=[pltpu.VMEM((tm + 2 * halo, d), MXU_DTYPE), pltpu.VMEM((tm, D_FF), MXU_DTYPE)],
        compiler_params=_cparams(("arbitrary", "arbitrary")),
        name="conv_ffn",
    )(x, x, x, mod, mod, mod, norm_g, w_up, conv_w, conv_b, w_down, final_g.reshape(1, d))


def _rope_tables(l):
    half = NA_HEAD_DIM // 2
    quarter = half // 2
    inv_freq = np.float32(ROPE_THETA) ** (-np.arange(quarter, dtype=np.float32) * np.float32(2.0) / np.float32(half))
    pos = np.arange(l)
    ang_r = (pos // GRID_W).astype(np.float32)[:, None] * inv_freq[None, :]
    ang_c = (pos % GRID_W).astype(np.float32)[:, None] * inv_freq[None, :]
    cos_h = np.concatenate([np.cos(ang_r), np.cos(ang_r), np.cos(ang_c), np.cos(ang_c)], axis=-1)
    sin_h = np.concatenate([-np.sin(ang_r), np.sin(ang_r), -np.sin(ang_c), np.sin(ang_c)], axis=-1)
    reps = LANES // NA_HEAD_DIM
    return (jnp.asarray(np.tile(cos_h, (1, reps)).astype(np.float32)),
            jnp.asarray(np.tile(sin_h, (1, reps)).astype(np.float32)))


def kernel(x, c, ctx, c_ctx, ada_w, ada_b, norm1_g, norm2_g, w_in, sgu_norm_g, sgu_w, sgu_b, na_rpb, hy_conv_w, hy_conv_b, hy_filt_w1, hy_filt_b1, hy_filt_w2, hy_filt_b2, hy_filt_w3, hy_sin_freq, hy_skip, w_branch, w_out, ffn_w_up, ffn_conv_w, ffn_conv_b, ffn_w_down, final_norm_g):
    b, l, d = x.shape
    lc = ctx.shape[1]
    assert d == D_MODEL and l % ATT_TOK == 0 and l % GRID_W == 0

    n_rows = -(-(b + 1) // 8) * 8
    cc = jnp.concatenate([c, c_ctx[None, :], jnp.zeros((n_rows - b - 1, d), F32)], axis=0)
    mod = _modulation(cc, ada_w, ada_b).reshape(DEPTH * n_rows, 1, N_MOD * d)

    rope_tabs = _rope_tables(l)
    dft_l, fc_l = _dft_consts(l // _hy_blocks(l)), _filter_consts(l)
    dft_c, fc_c = _dft_consts(lc // _hy_blocks(lc)), _filter_consts(lc)
    patterns, type_of, key_start = _att_plan(l // GRID_W)

    w_in_b, wb_b, wo_b = w_in.astype(MXU_DTYPE), w_branch.astype(MXU_DTYPE), w_out.astype(MXU_DTYPE)
    wup_b, wdn_b = ffn_w_up.astype(MXU_DTYPE), ffn_w_down.astype(MXU_DTYPE)
    n1g, n2g = norm1_g.reshape(DEPTH, 1, d), norm2_g.reshape(DEPTH, 1, d)
    ffn_cb = ffn_conv_b.reshape(DEPTH, 1, D_FF)
    sgu = (sgu_norm_g.reshape(DEPTH, 1, BRANCH_DIM), sgu_w.astype(MXU_DTYPE),
           sgu_b.reshape(DEPTH, SGU_GROUPS, SGU_CHUNK, 1))

    grp = math.gcd(b, max(1, l // lc))
    stack = lambda t: t.reshape(b // grp, grp * lc, t.shape[-1])
    unstack = lambda t: t.reshape(b, lc, t.shape[-1])

    xc = ctx
    for i in range(DEPTH):
        update_ctx = i < DEPTH - 1
        lat_row = lambda bb, i=i: i * n_rows + bb
        ctx_row = lambda bb, i=i: i * n_rows + b
        filt_args = (hy_filt_w1[i], hy_filt_b1[i], hy_filt_w2[i], hy_filt_b2[i], hy_filt_w3[i], hy_sin_freq[i])
        bias = _bias_tables(na_rpb[i], patterns)

        if update_ctx:
            a_c, q_c, k_c, v_c, hy_c = map(unstack, _in_proj(stack(xc), mod, ctx_row, n1g, w_in_b, i, FULL_SEGS, None,
                                                             sgu, "context_in_proj"))
        else:
            k_c, v_c = map(unstack, _in_proj(stack(xc), mod, ctx_row, n1g, w_in_b, i, KV_SEGS, None, None,
                                             "context_kv_proj"))

        a, q, k, v, hy = _in_proj(x, mod, lat_row, n1g, w_in_b, i, FULL_SEGS, rope_tabs, sgu, "latent_in_proj")
        attn = _neighborhood_attention(q, k, v, k_c, v_c, bias, type_of, key_start)
        filt = _hyena_filters(l, dft_l, fc_l, *filt_args)
        hyo = _hyena(hy, hy_conv_w[i], hy_conv_b[i], hy_skip[i], filt, dft_l)
        x = _merge(x, a, attn, hyo, mod, lat_row, n1g, w_in_b, wb_b, wo_b, i)
        x = _ffn(x, mod, lat_row, n2g, wup_b, ffn_conv_w, ffn_cb, wdn_b, i, final_norm_g,
                 final=not update_ctx, seq_len=l)

        if update_ctx:
            attn_c = _context_attention(q_c, k_c, v_c)
            filt_c = _hyena_filters(lc, dft_c, fc_c, *filt_args)
            hyo_c = _hyena(hy_c, hy_conv_w[i], hy_conv_b[i], hy_skip[i], filt_c, dft_c)
            xs = _merge(stack(xc), stack(a_c), stack(attn_c), stack(hyo_c), mod, ctx_row, n1g, w_in_b, wb_b, wo_b, i)
            xc = unstack(_ffn(xs, mod, ctx_row, n2g, wup_b, ffn_conv_w, ffn_cb, wdn_b, i, final_norm_g,
                              final=False, seq_len=lc))
    return x
```

```python
import functools
import math

import jax
import jax.numpy as jnp
import numpy as np
from jax import lax
from jax.experimental import pallas as pl
from jax.experimental.pallas import tpu as pltpu

D_MODEL = 1024
DEPTH = 2
GRID_W = 64
NORM_EPS = 1e-6
N_MOD = 6
BRANCH_DIM = D_MODEL // 2
SGU_GROUP_DIM = 128
SGU_GROUPS = BRANCH_DIM // SGU_GROUP_DIM
SGU_CHUNK = 128
NA_HEAD_DIM = 64
NA_HEADS = BRANCH_DIM // NA_HEAD_DIM
NA_WIN_ROWS = 8
NA_WIN_COLS = 16
ROPE_THETA = 10000.0
NEG_INF = -1e30
HY_DIM = BRANCH_DIM
HY_ORDER = 2
HY_EMB = 33
HY_FILTER_HIDDEN = 64
HY_FAST_DECAY_PCT = 0.3
HY_SLOW_DECAY_PCT = 1.5
HY_DECAY_TARGET = 1e-2
D_FF = 128 * ((8 * D_MODEL + 3 * 128 - 1) // (3 * 128))
OFF_SGU = 0
OFF_QKV = OFF_SGU + 2 * BRANCH_DIM
OFF_HY = OFF_QKV + 3 * BRANCH_DIM
OFF_GATE = OFF_HY + (HY_ORDER + 1) * HY_DIM
IN_COLS = OFF_GATE + 3 * D_MODEL

LANES = 128
MXU_WIDTH = 256
VMEM_LIMIT_BYTES = 56 * 1024 * 1024

MXU_DTYPE = jnp.bfloat16
ACT_DTYPE = jnp.bfloat16
F32 = jnp.float32
HIGHEST = lax.Precision.HIGHEST

Q_ROWS_PER_BLOCK = 4
K_ROWS_PER_BLOCK = 12
ATT_TOK = Q_ROWS_PER_BLOCK * GRID_W
HY_CW = MXU_WIDTH


def _cparams(sem):
    return pltpu.CompilerParams(dimension_semantics=sem, vmem_limit_bytes=VMEM_LIMIT_BYTES)


def _const_spec(shape):
    nd = len(shape)
    return pl.BlockSpec(shape, lambda *_: (0,) * nd, pipeline_mode=pl.Buffered(1))


def _layer_spec(arr, layer):
    nd = arr.ndim - 1
    return pl.BlockSpec((None,) + arr.shape[1:], lambda *_: (layer,) + (0,) * nd, pipeline_mode=pl.Buffered(1))


def _dot(a, b, **kw):
    return jnp.dot(a, b, preferred_element_type=F32, **kw)


def _dot_nt(a, b):
    return lax.dot_general(a, b, (((1,), (1,)), ((), ())), preferred_element_type=F32)


def _gelu(x):
    return 0.5 * x * (1.0 + lax.erf(x * (1.0 / math.sqrt(2.0))))


def _mod_body(c_ref, w_ref, b_ref, o_ref):
    c = c_ref[...]
    s = c * jax.nn.sigmoid(c)
    o_ref[...] = _dot(s, w_ref[...], precision=HIGHEST) + b_ref[...]


def _modulation(cc, ada_w, ada_b):
    depth, d, n = ada_w.shape
    r = cc.shape[0]
    tn = 1024
    return pl.pallas_call(
        _mod_body,
        grid=(depth, n // tn),
        in_specs=[
            pl.BlockSpec((r, d), lambda i, j: (0, 0)),
            pl.BlockSpec((None, d, tn), lambda i, j: (i, 0, j)),
            pl.BlockSpec((None, 1, tn), lambda i, j: (i, 0, j)),
        ],
        out_specs=pl.BlockSpec((None, r, tn), lambda i, j: (i, 0, j)),
        out_shape=jax.ShapeDtypeStruct((depth, r, n), F32),
        compiler_params=_cparams(("arbitrary", "arbitrary")),
        name="adaln_modulation",
    )(cc, ada_w, ada_b.reshape(depth, 1, n))


def _mod_spec(row_fn, chunk):
    return pl.BlockSpec((None, 1, D_MODEL), lambda *g: (row_fn(*g), 0, chunk))


def _rope(acc, cos, sin):
    lane = lax.broadcasted_iota(jnp.int32, cos.shape, 1)
    first_half = (lane % 32) < 16
    outs = []
    for c0 in range(0, acc.shape[1], LANES):
        xc = acc[:, c0:c0 + LANES]
        partner = jnp.where(first_half, pltpu.roll(xc, LANES - 16, 1), pltpu.roll(xc, 16, 1))
        outs.append(xc * cos + partner * sin)
    return outs


def _spatial_gating(u, v, g_ref, w_ref, b_ref, o_ref):
    ms = jnp.mean(v * v, axis=-1, keepdims=True)
    vb = (v * lax.rsqrt(ms + NORM_EPS) * g_ref[...]).astype(MXU_DTYPE)
    for n in range(u.shape[0] // SGU_CHUNK):
        rs = slice(n * SGU_CHUNK, (n + 1) * SGU_CHUNK)
        for g in range(SGU_GROUPS):
            cs = slice(g * SGU_GROUP_DIM, (g + 1) * SGU_GROUP_DIM)
            mixed = _dot(w_ref[g], vb[rs, cs]) + b_ref[g]
            o_ref[rs, cs] = (u[rs, cs] * mixed).astype(ACT_DTYPE)


def _in_proj_body(*refs, segs, rope, sgu, n_out):
    x_ref, sh_ref, sc_ref, g_ref, w_ref = refs[:5]
    pos = 5
    if rope:
        cos_ref, sin_ref = refs[pos:pos + 2]
        pos += 2
    if sgu:
        sg_ref, sw_ref, sb_ref = refs[pos:pos + 3]
        pos += 3
    out_refs = refs[pos:pos + n_out]
    x = x_ref[...]
    ms = jnp.mean(x * x, axis=-1, keepdims=True)
    y = x * lax.rsqrt(ms + NORM_EPS) * g_ref[...]
    h = (y * (1.0 + sc_ref[...]) + sh_ref[...]).astype(MXU_DTYPE)
    for (c0, width, kind, oi) in segs:
        if kind == "sgu":
            u = _gelu(_dot(h, w_ref[:, c0:c0 + BRANCH_DIM]))
            v = _gelu(_dot(h, w_ref[:, c0 + BRANCH_DIM:c0 + 2 * BRANCH_DIM]))
            _spatial_gating(u, v, sg_ref, sw_ref, sb_ref, out_refs[oi])
            continue
        step = min(width, 512)
        for cc in range(0, width, step):
            acc = _dot(h, w_ref[:, c0 + cc:c0 + cc + step])
            if kind == "rope" and rope:
                for k, piece in enumerate(_rope(acc, cos_ref[...], sin_ref[...])):
                    out_refs[oi][:, cc + k * LANES:cc + (k + 1) * LANES] = piece.astype(ACT_DTYPE)
            else:
                out_refs[oi][:, cc:cc + step] = acc.astype(ACT_DTYPE)


def _in_proj(x, mod, mod_row, norm_g, w, layer, segs, rope_tabs, sgu, name):
    bx, l, d = x.shape
    tm = min(IN_PROJ_TM, l)
    assert all(s[0] + s[1] <= OFF_GATE for s in segs)
    n_out = max(s[3] for s in segs) + 1
    out_w = lambda s: BRANCH_DIM if s[2] == "sgu" else s[1]
    widths = [sum(out_w(s) for s in segs if s[3] == oi) for oi in range(n_out)]
    rope = rope_tabs is not None
    use_sgu = any(s[2] == "sgu" for s in segs)
    in_specs = [
        pl.BlockSpec((None, tm, d), lambda b, i: (b, i, 0)),
        _mod_spec(lambda b, i: mod_row(b), 0),
        _mod_spec(lambda b, i: mod_row(b), 1),
        _layer_spec(norm_g, layer),
        pl.BlockSpec((None, d, OFF_GATE), lambda *_: (layer, 0, 0), pipeline_mode=pl.Buffered(1)),
    ]
    args = [x, mod, mod, norm_g, w]
    if rope:
        in_specs += [pl.BlockSpec((tm, LANES), lambda b, i: (i, 0))] * 2
        args += list(rope_tabs)
    if use_sgu:
        in_specs += [_layer_spec(p, layer) for p in sgu]
        args += list(sgu)
    return pl.pallas_call(
        functools.partial(_in_proj_body, segs=segs, rope=rope, sgu=use_sgu, n_out=n_out),
        grid=(bx, l // tm),
        in_specs=in_specs,
        out_specs=[pl.BlockSpec((None, tm, wd), lambda b, i: (b, i, 0)) for wd in widths],
        out_shape=[jax.ShapeDtypeStruct((bx, l, wd), ACT_DTYPE) for wd in widths],
        compiler_params=_cparams(("arbitrary", "arbitrary")),
        name=name,
    )(*args)


IN_PROJ_TM = 1024
FULL_SEGS = (
    (OFF_SGU, 2 * BRANCH_DIM, "sgu", 0),
    (OFF_QKV, BRANCH_DIM, "rope", 1),
    (OFF_QKV + BRANCH_DIM, BRANCH_DIM, "rope", 2),
    (OFF_QKV + 2 * BRANCH_DIM, BRANCH_DIM, "plain", 3),
) + tuple((OFF_HY + j * HY_CW, HY_CW, "plain", 4 + j) for j in range(3 * HY_DIM // HY_CW))
KV_SEGS = ((OFF_QKV + BRANCH_DIM, BRANCH_DIM, "plain", 0), (OFF_QKV + 2 * BRANCH_DIM, BRANCH_DIM, "plain", 1))


def _att_plan(rows):
    kr = min(NA_WIN_ROWS, rows)
    assert rows % Q_ROWS_PER_BLOCK == 0 and rows >= K_ROWS_PER_BLOCK and kr == NA_WIN_ROWS
    patterns, type_of, key_start = [], [], []
    for rb in range(0, rows, Q_ROWS_PER_BLOCK):
        ks = min(max(rb - NA_WIN_ROWS // 2, 0), rows - K_ROWS_PER_BLOCK)
        assert ks % Q_ROWS_PER_BLOCK == 0
        pat = []
        for qi in range(Q_ROWS_PER_BLOCK):
            rq = rb + qi
            r0 = min(max(rq - kr // 2, 0), rows - kr)
            for j in range(K_ROWS_PER_BLOCK):
                rk = ks + j
                pat.append(rk - rq + NA_WIN_ROWS - 1 if r0 <= rk < r0 + kr else None)
        pat = tuple(pat)
        if pat not in patterns:
            patterns.append(pat)
        type_of.append(patterns.index(pat))
        key_start.append(ks // Q_ROWS_PER_BLOCK)
    return patterns, type_of, key_start


def _bias_body(rpb_ref, o_ref, *, patterns):
    n_dr, n_dc = 2 * NA_WIN_ROWS - 1, 2 * NA_WIN_COLS - 1
    h = pl.program_id(0)
    qc = lax.broadcasted_iota(jnp.int32, (GRID_W, GRID_W), 0)
    kc = lax.broadcasted_iota(jnp.int32, (GRID_W, GRID_W), 1)
    cstart = jnp.clip(qc - NA_WIN_COLS // 2, 0, GRID_W - NA_WIN_COLS)
    col_ok = (kc >= cstart) & (kc < cstart + NA_WIN_COLS)
    dc = jnp.clip(kc - qc + NA_WIN_COLS - 1, 0, n_dc - 1)
    neg = jnp.full((GRID_W, GRID_W), NEG_INF, F32)
    tiles = []
    for dr in range(n_dr):
        t = jnp.zeros((GRID_W, GRID_W), F32)
        for d in range(n_dc):
            t = jnp.where(dc == d, rpb_ref[(h * n_dr + dr) * n_dc + d], t)
        tiles.append(jnp.where(col_ok, t * LOG2E, neg))
    for ty, pat in enumerate(patterns):
        for qi in range(Q_ROWS_PER_BLOCK):
            for j in range(K_ROWS_PER_BLOCK):
                dr = pat[qi * K_ROWS_PER_BLOCK + j]
                o_ref[ty, qi * GRID_W:(qi + 1) * GRID_W, j * GRID_W:(j + 1) * GRID_W] = neg if dr is None else tiles[dr]


def _bias_tables(rpb, patterns):
    nt = len(patterns)
    return pl.pallas_call(
        functools.partial(_bias_body, patterns=patterns),
        grid=(NA_HEADS,),
        in_specs=[pl.BlockSpec(memory_space=pltpu.SMEM)],
        out_specs=pl.BlockSpec((nt, None, ATT_TOK, K_ROWS_PER_BLOCK * GRID_W), lambda h: (0, h, 0, 0)),
        out_shape=jax.ShapeDtypeStruct((nt, NA_HEADS, ATT_TOK, K_ROWS_PER_BLOCK * GRID_W), F32),
        compiler_params=_cparams(("arbitrary",)),
        name="attention_bias_tables",
    )(rpb.reshape(-1))


def _head_mask(shape, hh):
    lane = lax.broadcasted_iota(jnp.int32, shape, 1)
    return (lane >= hh * NA_HEAD_DIM) & (lane < (hh + 1) * NA_HEAD_DIM)


LOG2E = math.log2(math.e)
QK_SCALE = NA_HEAD_DIM ** -0.5 * LOG2E


def _pair_attention(qp, keys, biases, values):
    m_rows = qp.shape[0]
    q2 = jnp.concatenate([jnp.where(_head_mask(qp.shape, hh), qp, jnp.zeros_like(qp)) for hh in range(2)], axis=0)
    s_all = _dot_nt(q2, jnp.concatenate(keys, axis=0))
    scores, c0 = [], 0
    for j, kj in enumerate(keys):
        s = s_all[:, c0:c0 + kj.shape[0]]
        c0 += kj.shape[0]
        if biases[0][j] is not None:
            s = s + jnp.concatenate([biases[0][j], biases[1][j]], axis=0)
        scores.append(s)
    m = functools.reduce(jnp.maximum, [jnp.max(s, axis=-1, keepdims=True) for s in scores])
    e = jnp.concatenate([jnp.exp2(s - m).astype(MXU_DTYPE) for s in scores], axis=1)
    vcat = jnp.concatenate(values, axis=0)
    acc = _dot(e, jnp.concatenate([vcat, jnp.ones_like(vcat)], axis=1))
    out = acc[:, :LANES] / acc[:, LANES:]
    return jnp.where(_head_mask((m_rows, LANES), 0), out[:m_rows], out[m_rows:])


ATT_BATCH = 8


def _attn_body(ty_ref, kb_ref, q_ref, k0, k1, k2, v0, v1, v2, kc_ref, vc_ref, bias_ref, o_ref):
    del ty_ref, kb_ref
    for p in range(NA_HEADS // 2):
        cs = slice(p * LANES, (p + 1) * LANES)
        biases = [[bias_ref[2 * p + hh, :, j * ATT_TOK:(j + 1) * ATT_TOK] for j in range(3)] + [None]
                  for hh in range(2)]
        for e in range(q_ref.shape[0]):
            qp = q_ref[e, :, cs] * QK_SCALE
            keys = [k0[e, :, cs], k1[e, :, cs], k2[e, :, cs], kc_ref[e, :, cs]]
            values = [v0[e, :, cs], v1[e, :, cs], v2[e, :, cs], vc_ref[e, :, cs]]
            o_ref[e, :, cs] = _pair_attention(qp, keys, biases, values).astype(ACT_DTYPE)


def _neighborhood_attention(q, k, v, kc, vc, bias, type_of, key_start):
    b, l, _ = q.shape
    lc = kc.shape[1]
    n_blk = l // ATT_TOK
    nb = math.gcd(b, ATT_BATCH)
    tok = lambda off: pl.BlockSpec((nb, ATT_TOK, BRANCH_DIM), lambda r, bb, ty, kb: (bb, kb[r] + off, 0))
    ctx = pl.BlockSpec((nb, lc, BRANCH_DIM), lambda r, bb, ty, kb: (bb, 0, 0))
    grid_spec = pltpu.PrefetchScalarGridSpec(
        num_scalar_prefetch=2,
        grid=(n_blk, b // nb),
        in_specs=[
            pl.BlockSpec((nb, ATT_TOK, BRANCH_DIM), lambda r, bb, ty, kb: (bb, r, 0)),
            tok(0), tok(1), tok(2), tok(0), tok(1), tok(2), ctx, ctx,
            pl.BlockSpec((None, NA_HEADS, ATT_TOK, K_ROWS_PER_BLOCK * GRID_W), lambda r, bb, ty, kb: (ty[r], 0, 0, 0)),
        ],
        out_specs=pl.BlockSpec((nb, ATT_TOK, BRANCH_DIM), lambda r, bb, ty, kb: (bb, r, 0)),
    )
    return pl.pallas_call(
        _attn_body,
        grid_spec=grid_spec,
        out_shape=jax.ShapeDtypeStruct((b, l, BRANCH_DIM), ACT_DTYPE),
        compiler_params=_cparams(("arbitrary", "arbitrary")),
        name="neighborhood_attention",
    )(jnp.asarray(type_of, jnp.int32), jnp.asarray(key_start, jnp.int32), q, k, k, k, v, v, v, kc, vc, bias)


def _ctx_attn_body(q_ref, k_ref, v_ref, o_ref):
    for p in range(NA_HEADS // 2):
        cs = slice(p * LANES, (p + 1) * LANES)
        qp = q_ref[:, cs] * QK_SCALE
        o_ref[:, cs] = _pair_attention(qp, [k_ref[:, cs]], [[None], [None]], [v_ref[:, cs]]).astype(ACT_DTYPE)


def _context_attention(q, k, v):
    b, lc, _ = q.shape
    spec = pl.BlockSpec((None, lc, BRANCH_DIM), lambda bb: (bb, 0, 0))
    return pl.pallas_call(
        _ctx_attn_body,
        grid=(b,),
        in_specs=[spec, spec, spec],
        out_specs=spec,
        out_shape=jax.ShapeDtypeStruct((b, lc, BRANCH_DIM), ACT_DTYPE),
        compiler_params=_cparams(("arbitrary",)),
        name="context_attention",
    )(q, k, v)


HY_TM = 512
HY_BLOCKS = 4
HY_MIN_BLOCK = 256
CONV_HALO = 16


def _hy_blocks(l):
    return HY_BLOCKS if l % (HY_BLOCKS * HY_MIN_BLOCK) == 0 else 1


@functools.lru_cache(maxsize=None)
def _dft_consts_np(m):
    n = 2 * m
    idx = np.arange(m, dtype=np.int64)
    ang = ((idx[:, None] * idx[None, :]) % n).astype(np.float64) * (2.0 * math.pi / n)
    cs = np.concatenate([np.cos(ang), -np.sin(ang)], axis=1).astype(np.float32)
    sign = np.where(idx % 2 == 0, 1.0, -1.0).astype(np.float32).reshape(m, 1)
    wf = np.where(idx == 0, 1.0 / n, 2.0 / n).astype(np.float32).reshape(m, 1)
    return cs, sign, wf


def _dft_consts(m):
    cs, sign, wf = _dft_consts_np(m)
    return jnp.asarray(cs.astype(MXU_DTYPE)), jnp.asarray(sign), jnp.asarray(wf)


def _filter_consts(l):
    t = np.linspace(0.0, 1.0, l, dtype=np.float32)[:, None]
    n_bands = (HY_EMB - 1) // 2
    bands = np.linspace(1e-4, n_bands - 1, n_bands, dtype=np.float32)[None, :]
    ang = np.float32(2.0 * math.pi) * np.arange(l, dtype=np.float32)[:, None] / np.float32(l) * bands
    z = np.concatenate([t, np.cos(ang), -np.sin(ang)], axis=-1).astype(np.float32)
    z = np.pad(z, ((0, 0), (0, LANES - HY_EMB)))
    max_decay = math.log(HY_DECAY_TARGET) / HY_FAST_DECAY_PCT
    min_decay = math.log(HY_DECAY_TARGET) / HY_SLOW_DECAY_PCT
    deltas = np.abs(np.linspace(min_decay, max_decay, HY_DIM, dtype=np.float32)).reshape(1, HY_DIM)
    return jnp.asarray(z), jnp.asarray(t), jnp.asarray(deltas)


def _filter_body(z_ref, w1_ref, b1_ref, w2_ref, b2_ref, fr_ref, w3a_ref, w3b_ref, t_ref, dl_ref,
                 sg_ref, wf_ref, cs_ref, gre_ref, gim_ref, gny_ref, h_ref, *, p):
    l = z_ref.shape[0]
    m = l // p

    @pl.when((pl.program_id(0) == 0) & (pl.program_id(1) == 0))
    def _():
        fr = fr_ref[...]
        h1 = jnp.sin(fr * (_dot(z_ref[...], w1_ref[...], precision=HIGHEST) + b1_ref[...]))
        h_ref[...] = jnp.sin(fr * (_dot(h1, w2_ref[...], precision=HIGHEST) + b2_ref[...]))

    h = h_ref[...]
    decay = jnp.exp(-t_ref[...] * dl_ref[...])
    h0 = _dot(h, w3a_ref[...], precision=HIGHEST) * decay
    h1 = _dot(h, w3b_ref[...], precision=HIGHEST) * decay
    row = lax.broadcasted_iota(jnp.int32, h1.shape, 0)
    den = (jnp.sum(jnp.abs(h0), axis=0, keepdims=True)
           + jnp.sum(jnp.where(row == 0, 0.0, jnp.abs(h1)), axis=0, keepdims=True))
    h0, h1 = h0 / den, h1 / den
    sg, wf = sg_ref[...], wf_ref[...]
    u, v, pc, q, un, pn, r0, r1 = [], [], [], [], [], [], [], []
    for b in range(p):
        x0, x1 = h0[b * m:(b + 1) * m], h1[b * m:(b + 1) * m]
        x0b, x1b = x0.astype(MXU_DTYPE), x1.astype(MXU_DTYPE)
        u.append(_dot(cs_ref[:, :m], x0b))
        v.append(_dot(cs_ref[:, m:], x0b))
        pc.append(_dot(cs_ref[:, :m], x1b))
        q.append(_dot(cs_ref[:, m:], x1b))
        un.append(jnp.sum(sg * x0, axis=0, keepdims=True))
        pn.append(jnp.sum(sg * x1, axis=0, keepdims=True))
        r0.append(x0[0:1])
        r1.append(x1[0:1])
    for li, d in enumerate(range(-(p - 1), p)):
        if d > 0:
            gre = u[d] + sg * (u[d - 1] - r0[d - 1])
            gim = v[d] + sg * v[d - 1]
            gny = un[d] + un[d - 1] - r0[d - 1]
        elif d == 0:
            gre = u[0] + pc[0] - r1[0]
            gim = v[0] - q[0]
            gny = un[0] + pn[0] - r1[0]
        else:
            gre = pc[-d] + sg * (pc[-d - 1] - r1[-d - 1])
            gim = -q[-d] - sg * q[-d - 1]
            gny = pn[-d] + pn[-d - 1] - r1[-d - 1]
        gre_ref[li] = (wf * gre).astype(gre_ref.dtype)
        gim_ref[li] = (wf * gim).astype(gim_ref.dtype)
        gny_ref[li] = gny * (1.0 / (2 * m))


def _hyena_filters(l, dft, fconsts, w1, b1, w2, b2, w3, freq):
    cs, sign, wf = dft
    z, t, deltas = fconsts
    p = _hy_blocks(l)
    m, nl = l // p, 2 * p - 1
    cw = HY_CW
    nb = HY_DIM // cw
    w1p = jnp.pad(w1, ((0, LANES - HY_EMB), (0, 0)))
    hid = HY_FILTER_HIDDEN
    small = lambda shape: pl.BlockSpec(shape, lambda n, cb: (0,) * len(shape))
    spec_out = pl.BlockSpec((None, nl, m, cw), lambda n, cb: (n, 0, 0, cb))
    return pl.pallas_call(
        functools.partial(_filter_body, p=p),
        grid=(HY_ORDER, nb),
        in_specs=[
            small((l, LANES)), small((LANES, hid)), small((1, hid)), small((hid, hid)), small((1, hid)), small((1, hid)),
            pl.BlockSpec((hid, cw), lambda n, cb: (0, n * 2 * nb + cb)),
            pl.BlockSpec((hid, cw), lambda n, cb: (0, n * 2 * nb + nb + cb)),
            small((l, 1)),
            pl.BlockSpec((1, cw), lambda n, cb: (0, cb)),
            small((m, 1)), small((m, 1)),
            _const_spec((m, 2 * m)),
        ],
        out_specs=[spec_out, spec_out, pl.BlockSpec((None, nl, 1, cw), lambda n, cb: (n, 0, 0, cb))],
        out_shape=[jax.ShapeDtypeStruct((HY_ORDER, nl, m, HY_DIM), ACT_DTYPE)] * 2
        + [jax.ShapeDtypeStruct((HY_ORDER, nl, 1, HY_DIM), F32)],
        scratch_shapes=[pltpu.VMEM((l, hid), F32)],
        compiler_params=_cparams(("arbitrary", "arbitrary")),
        name="hyena_filter_spectrum",
    )(z, w1p, b1.reshape(1, hid), w2, b2.reshape(1, hid), freq.reshape(1, hid), w3, w3, t, deltas, sign, wf, cs)


def _conv3_rows(p_ref, w_ref, b_ref, r0, rows):
    l = p_ref.shape[0]
    lo, hi = max(r0 - CONV_HALO, 0), min(r0 + rows + CONV_HALO, l)
    x = p_ref[lo:hi, :].astype(F32)
    n = hi - lo
    prev, nxt = pltpu.roll(x, 1, 0), pltpu.roll(x, n - 1, 0)
    row = lax.broadcasted_iota(jnp.int32, x.shape, 0)
    if lo == 0:
        prev = jnp.where(row == 0, 0.0, prev)
    if hi == l:
        nxt = jnp.where(row == n - 1, 0.0, nxt)
    y = prev * w_ref[0:1, :] + x * w_ref[1:2, :] + nxt * w_ref[2:3, :] + b_ref[...]
    return y[r0 - lo:r0 - lo + rows]


def _hyena_body(pv_ref, p1_ref, p2_ref, wv_ref, w1_ref, w2_ref, bv_ref, b1_ref, b2_ref, skip_ref,
                gre_ref, gim_ref, gny_ref, sg_ref, cs_ref, o_ref, zf_ref, zb_ref, zz_ref, y_ref, *, p):
    l, cw = pv_ref.shape
    m = l // p
    tm = min(HY_TM, m)
    offs = range(0, m, tm)
    zn = [jnp.zeros((1, cw), F32) for _ in range(p)]

    def put_z(a, c0, z):
        rs = slice(a * m + c0, a * m + c0 + tm)
        zf_ref[rs] = z
        zb_ref[rs] = z.astype(MXU_DTYPE)
        zn[a] = zn[a] + jnp.sum(sg_ref[c0:c0 + tm] * z, axis=0, keepdims=True)

    for a in range(p):
        for c0 in offs:
            put_z(a, c0, _conv3_rows(pv_ref, wv_ref, bv_ref, a * m + c0, tm))
    gates = ((p1_ref, w1_ref, b1_ref), (p2_ref, w2_ref, b2_ref))
    for n, (p_ref, w_ref, b_ref) in enumerate(gates):
        for a in range(p):
            zblk = zb_ref[a * m:(a + 1) * m]
            for c0 in offs:
                rs = slice(a * m + c0, a * m + c0 + tm)
                zz_ref[0, rs] = _dot(cs_ref[c0:c0 + tm, :m], zblk).astype(zz_ref.dtype)
                zz_ref[1, rs] = _dot(cs_ref[c0:c0 + tm, m:], zblk).astype(zz_ref.dtype)
        for a2 in range(p):
            for c0 in offs:
                yc = ys = None
                for a in range(p):
                    li = a2 - a + p - 1
                    gre, gim = gre_ref[n, li, c0:c0 + tm], gim_ref[n, li, c0:c0 + tm]
                    zc, zs = zz_ref[0, a * m + c0:a * m + c0 + tm], zz_ref[1, a * m + c0:a * m + c0 + tm]
                    tc, ts = zc * gre - zs * gim, zc * gim + zs * gre
                    yc, ys = (tc, ts) if yc is None else (yc + tc, ys + ts)
                y_ref[a2, c0:c0 + tm] = yc.astype(MXU_DTYPE)
                y_ref[a2, m + c0:m + c0 + tm] = ys.astype(MXU_DTYPE)
        nyq = [functools.reduce(lambda s, t: s + t, [zn[a] * gny_ref[n, a2 - a + p - 1] for a in range(p)])
               for a2 in range(p)]
        zn = [jnp.zeros((1, cw), F32) for _ in range(p)]
        for a2 in range(p):
            for c0 in offs:
                rs = slice(a2 * m + c0, a2 * m + c0 + tm)
                y = _dot(cs_ref[c0:c0 + tm, :], y_ref[a2]) + sg_ref[c0:c0 + tm] * nyq[a2]
                gate = _conv3_rows(p_ref, w_ref, b_ref, rs.start, tm)
                z = gate * (y + zf_ref[rs] * skip_ref[n:n + 1, :])
                if n + 1 < HY_ORDER:
                    put_z(a2, c0, z)
                else:
                    o_ref[rs] = z.astype(ACT_DTYPE)


def _hyena(parts, conv_w, conv_b, skip, filters, dft):
    bx, l, cw = parts[0].shape
    cs, sign, _ = dft
    gre, gim, gny = filters
    p = _hy_blocks(l)
    m, nl = l // p, 2 * p - 1
    nb = HY_DIM // cw
    conv_b = conv_b.reshape(1, -1)
    act = pl.BlockSpec((None, l, cw), lambda b: (b, 0, 0))
    outs = []
    for cb in range(nb):
        col = lambda shape, j: pl.BlockSpec(shape, lambda b: (0,) * (len(shape) - 1) + (j,), pipeline_mode=pl.Buffered(1))
        outs.append(pl.pallas_call(
            functools.partial(_hyena_body, p=p),
            grid=(bx,),
            in_specs=[
                act, act, act,
                col((3, cw), cb), col((3, cw), nb + cb), col((3, cw), 2 * nb + cb),
                col((1, cw), cb), col((1, cw), nb + cb), col((1, cw), 2 * nb + cb),
                col((HY_ORDER, cw), cb),
                col((HY_ORDER, nl, m, cw), cb), col((HY_ORDER, nl, m, cw), cb), col((HY_ORDER, nl, 1, cw), cb),
                _const_spec((m, 1)), _const_spec((m, 2 * m)),
            ],
            out_specs=act,
            out_shape=jax.ShapeDtypeStruct((bx, l, cw), ACT_DTYPE),
            scratch_shapes=[pltpu.VMEM((l, cw), F32), pltpu.VMEM((l, cw), MXU_DTYPE),
                            pltpu.VMEM((2, l, cw), ACT_DTYPE), pltpu.VMEM((p, 2 * m, cw), MXU_DTYPE)],
            compiler_params=_cparams(("arbitrary",)),
            name="hyena_long_conv",
        )(parts[cb], parts[nb + cb], parts[2 * nb + cb], conv_w, conv_w, conv_w, conv_b, conv_b, conv_b,
          skip, gre, gim, gny, sign, cs))
    return outs


MERGE_TM = 1024


def _merge_body(x_ref, a_ref, at_ref, *refs):
    hy_refs = refs[:-10]
    sh_ref, sc_ref, g1_ref, ng_ref, wg0_ref, wg1_ref, wg2_ref, wb_ref, wo_ref, o_ref = refs[-10:]
    x = x_ref[...]
    ms = jnp.mean(x * x, axis=-1, keepdims=True)
    y = x * lax.rsqrt(ms + NORM_EPS) * ng_ref[...]
    h = (y * (1.0 + sc_ref[...]) + sh_ref[...]).astype(MXU_DTYPE)
    m = None
    cw = hy_refs[0].shape[1]
    hy_proj = functools.reduce(lambda s, t: s + t, [_dot(r[...], wb_ref[2, k * cw:(k + 1) * cw])
                                                   for k, r in enumerate(hy_refs)])
    branch = (_dot(a_ref[...], wb_ref[0]), _dot(at_ref[...], wb_ref[1]), hy_proj)
    for j, wg_ref in enumerate((wg0_ref, wg1_ref, wg2_ref)):
        term = jax.nn.sigmoid(_dot(h, wg_ref[...])) * branch[j]
        m = term if m is None else m + term
    o_ref[...] = x + g1_ref[...] * _dot(m.astype(MXU_DTYPE), wo_ref[...])


def _merge(x, a, attn, hy, mod, mod_row, norm_g, w_in, wb, wo, layer):
    bx, l, d = x.shape
    tm = min(MERGE_TM, l)
    tile = lambda w: pl.BlockSpec((None, tm, w), lambda b, i: (b, i, 0))
    gate_w = lambda j: pl.BlockSpec((None, d, d), lambda *_: (layer, 0, OFF_GATE // d + j), pipeline_mode=pl.Buffered(1))
    return pl.pallas_call(
        _merge_body,
        grid=(bx, l // tm),
        in_specs=[tile(d), tile(BRANCH_DIM), tile(BRANCH_DIM)] + [tile(t.shape[-1]) for t in hy] + [
                  _mod_spec(lambda b, i: mod_row(b), 0), _mod_spec(lambda b, i: mod_row(b), 1),
                  _mod_spec(lambda b, i: mod_row(b), 2), _layer_spec(norm_g, layer),
                  gate_w(0), gate_w(1), gate_w(2), _layer_spec(wb, layer), _layer_spec(wo, layer)],
        out_specs=tile(d),
        out_shape=jax.ShapeDtypeStruct((bx, l, d), F32),
        compiler_params=_cparams(("arbitrary", "arbitrary")),
        name="branch_merge",
    )(x, a, attn, *hy, mod, mod, mod, norm_g, w_in, w_in, w_in, wb, wo)


FFN_TM = 1024
FFN_TN = 256


def _ffn_body(xp_ref, x_ref, xn_ref, sh_ref, sc_ref, g2_ref, ng_ref, wu_ref, cw_ref, cb_ref, wd_ref, fg_ref,
              o_ref, h_ref, gt_ref, *, final, seq_len):
    tm = x_ref.shape[0]
    halo = CONV_HALO
    n_ext = tm + 2 * halo

    def norm_mod(x):
        ms = jnp.mean(x * x, axis=-1, keepdims=True)
        y = x * lax.rsqrt(ms + NORM_EPS) * ng_ref[...]
        return (y * (1.0 + sc_ref[...]) + sh_ref[...]).astype(MXU_DTYPE)

    h_ref[0:halo] = norm_mod(xp_ref[...])
    h_ref[halo:halo + tm] = norm_mod(x_ref[...])
    h_ref[halo + tm:n_ext] = norm_mod(xn_ref[...])
    row = lax.broadcasted_iota(jnp.int32, (n_ext, 1), 0)
    pos = (pl.program_id(1) * tm + row + (seq_len - halo)) % seq_len
    first, last = pos == 0, pos == seq_len - 1
    for j in range(D_FF // FFN_TN):
        cs = slice(j * FFN_TN, (j + 1) * FFN_TN)
        a = _dot(h_ref[...], wu_ref[:, cs])
        prev = jnp.where(first, 0.0, pltpu.roll(a, 1, 0))
        nxt = jnp.where(last, 0.0, pltpu.roll(a, n_ext - 1, 0))
        a = prev * cw_ref[0:1, cs] + a * cw_ref[1:2, cs] + nxt * cw_ref[2:3, cs] + cb_ref[:, cs]
        gate = _dot(h_ref[halo:halo + tm], wu_ref[:, D_FF + j * FFN_TN:D_FF + (j + 1) * FFN_TN])
        gt_ref[:, cs] = (_gelu(a[halo:halo + tm]) * gate).astype(MXU_DTYPE)
    xn = x_ref[...] + g2_ref[...] * _dot(gt_ref[...], wd_ref[...])
    if final:
        ms = jnp.mean(xn * xn, axis=-1, keepdims=True)
        xn = xn * lax.rsqrt(ms + NORM_EPS) * fg_ref[...]
    o_ref[...] = xn


def _ffn(x, mod, mod_row, norm_g, w_up, conv_w, conv_b, w_down, layer, final_g, final, seq_len):
    bx, l, d = x.shape
    tm = min(FFN_TM, l)
    halo = CONV_HALO
    per, last_blk = tm // halo, l // halo - 1
    tile = pl.BlockSpec((None, tm, d), lambda b, i: (b, i, 0))
    return pl.pallas_call(
        functools.partial(_ffn_body, final=final, seq_len=seq_len),
        grid=(bx, l // tm),
        in_specs=[
            pl.BlockSpec((None, halo, d), lambda b, i: (b, jnp.maximum(i * per - 1, 0), 0)),
            tile,
            pl.BlockSpec((None, halo, d), lambda b, i: (b, jnp.minimum((i + 1) * per, last_blk), 0)),
            _mod_spec(lambda b, i: mod_row(b), 3), _mod_spec(lambda b, i: mod_row(b), 4),
            _mod_spec(lambda b, i: mod_row(b), 5),
            _layer_spec(norm_g, layer), _layer_spec(w_up, layer), _layer_spec(conv_w, layer),
            _layer_spec(conv_b, layer), _layer_spec(w_down, layer), _const_spec((1, d)),
        ],
        out_specs=tile,
        out_shape=jax.ShapeDtypeStruct((bx, l, d), F32),
        scratch_shapes=[pltpu.VMEM((tm + 2 * halo, d), MXU_DTYPE), pltpu.VMEM((tm, D_FF), MXU_DTYPE)],
        compiler_params=_cparams(("arbitrary", "arbitrary")),
        name="conv_ffn",
    )(x, x, x, mod, mod, mod, norm_g, w_up, conv_w, conv_b, w_down, final_g.reshape(1, d))


def _rope_tables(l):
    half = NA_HEAD_DIM // 2
    quarter = half // 2
    inv_freq = np.float32(ROPE_THETA) ** (-np.arange(quarter, dtype=np.float32) * np.float32(2.0) / np.float32(half))
    pos = np.arange(l)
    ang_r = (pos // GRID_W).astype(np.float32)[:, None] * inv_freq[None, :]
    ang_c = (pos % GRID_W).astype(np.float32)[:, None] * inv_freq[None, :]
    cos_h = np.concatenate([np.cos(ang_r), np.cos(ang_r), np.cos(ang_c), np.cos(ang_c)], axis=-1)
    sin_h = np.concatenate([-np.sin(ang_r), np.sin(ang_r), -np.sin(ang_c), np.sin(ang_c)], axis=-1)
    reps = LANES // NA_HEAD_DIM
    return (jnp.asarray(np.tile(cos_h, (1, reps)).astype(np.float32)),
            jnp.asarray(np.tile(sin_h, (1, reps)).astype(np.float32)))


def kernel(x, c, ctx, c_ctx, ada_w, ada_b, norm1_g, norm2_g, w_in, sgu_norm_g, sgu_w, sgu_b, na_rpb, hy_conv_w, hy_conv_b, hy_filt_w1, hy_filt_b1, hy_filt_w2, hy_filt_b2, hy_filt_w3, hy_sin_freq, hy_skip, w_branch, w_out, ffn_w_up, ffn_conv_w, ffn_conv_b, ffn_w_down, final_norm_g):
    b, l, d = x.shape
    lc = ctx.shape[1]
    assert d == D_MODEL and l % ATT_TOK == 0 and l % GRID_W == 0

    n_rows = -(-(b + 1) // 8) * 8
    cc = jnp.concatenate([c, c_ctx[None, :], jnp.zeros((n_rows - b - 1, d), F32)], axis=0)
    mod = _modulation(cc, ada_w, ada_b).reshape(DEPTH * n_rows, 1, N_MOD * d)

    rope_tabs = _rope_tables(l)
    dft_l, fc_l = _dft_consts(l // _hy_blocks(l)), _filter_consts(l)
    dft_c, fc_c = _dft_consts(lc // _hy_blocks(lc)), _filter_consts(lc)
    patterns, type_of, key_start = _att_plan(l // GRID_W)

    w_in_b, wb_b, wo_b = w_in.astype(MXU_DTYPE), w_branch.astype(MXU_DTYPE), w_out.astype(MXU_DTYPE)
    wup_b, wdn_b = ffn_w_up.astype(MXU_DTYPE), ffn_w_down.astype(MXU_DTYPE)
    n1g, n2g = norm1_g.reshape(DEPTH, 1, d), norm2_g.reshape(DEPTH, 1, d)
    ffn_cb = ffn_conv_b.reshape(DEPTH, 1, D_FF)
    sgu = (sgu_norm_g.reshape(DEPTH, 1, BRANCH_DIM), sgu_w.astype(MXU_DTYPE),
           sgu_b.reshape(DEPTH, SGU_GROUPS, SGU_CHUNK, 1))

    grp = math.gcd(b, max(1, l // lc))
    stack = lambda t: t.reshape(b // grp, grp * lc, t.shape[-1])
    unstack = lambda t: t.reshape(b, lc, t.shape[-1])

    xc = ctx
    for i in range(DEPTH):
        update_ctx = i < DEPTH - 1
        lat_row = lambda bb, i=i: i * n_rows + bb
        ctx_row = lambda bb, i=i: i * n_rows + b
        filt_args = (hy_filt_w1[i], hy_filt_b1[i], hy_filt_w2[i], hy_filt_b2[i], hy_filt_w3[i], hy_sin_freq[i])
        bias = _bias_tables(na_rpb[i], patterns)

        if update_ctx:
            a_c, q_c, k_c, v_c, *hy_c = map(unstack, _in_proj(stack(xc), mod, ctx_row, n1g, w_in_b, i, FULL_SEGS, None,
                                                             sgu, "context_in_proj"))
        else:
            k_c, v_c = map(unstack, _in_proj(stack(xc), mod, ctx_row, n1g, w_in_b, i, KV_SEGS, None, None,
                                             "context_kv_proj"))

        a, q, k, v, *hy = _in_proj(x, mod, lat_row, n1g, w_in_b, i, FULL_SEGS, rope_tabs, sgu, "latent_in_proj")
        attn = _neighborhood_attention(q, k, v, k_c, v_c, bias, type_of, key_start)
        filt = _hyena_filters(l, dft_l, fc_l, *filt_args)
        hyo = _hyena(hy, hy_conv_w[i], hy_conv_b[i], hy_skip[i], filt, dft_l)
        x = _merge(x, a, attn, hyo, mod, lat_row, n1g, w_in_b, wb_b, wo_b, i)
        x = _ffn(x, mod, lat_row, n2g, wup_b, ffn_conv_w, ffn_cb, wdn_b, i, final_norm_g,
                 final=not update_ctx, seq_len=l)

        if update_ctx:
            attn_c = _context_attention(q_c, k_c, v_c)
            filt_c = _hyena_filters(lc, dft_c, fc_c, *filt_args)
            hyo_c = _hyena(hy_c, hy_conv_w[i], hy_conv_b[i], hy_skip[i], filt_c, dft_c)
            xs = _merge(stack(xc), stack(a_c), stack(attn_c), [stack(t) for t in hyo_c], mod, ctx_row, n1g, w_in_b, wb_b, wo_b, i)
            xc = unstack(_ffn(xs, mod, ctx_row, n2g, wup_b, ffn_conv_w, ffn_cb, wdn_b, i, final_norm_g,
                              final=False, seq_len=lc))
    return x
```

```python
import functools
import math

import jax
import jax.numpy as jnp
import numpy as np
from jax import lax
from jax.experimental import pallas as pl
from jax.experimental.pallas import tpu as pltpu

D_MODEL = 1024
DEPTH = 2
GRID_W = 64
NORM_EPS = 1e-6
N_MOD = 6
BRANCH_DIM = D_MODEL // 2
SGU_GROUP_DIM = 128
SGU_GROUPS = BRANCH_DIM // SGU_GROUP_DIM
SGU_CHUNK = 128
NA_HEAD_DIM = 64
NA_HEADS = BRANCH_DIM // NA_HEAD_DIM
NA_WIN_ROWS = 8
NA_WIN_COLS = 16
ROPE_THETA = 10000.0
NEG_INF = -1e30
HY_DIM = BRANCH_DIM
HY_ORDER = 2
HY_EMB = 33
HY_FILTER_HIDDEN = 64
HY_FAST_DECAY_PCT = 0.3
HY_SLOW_DECAY_PCT = 1.5
HY_DECAY_TARGET = 1e-2
D_FF = 128 * ((8 * D_MODEL + 3 * 128 - 1) // (3 * 128))
OFF_SGU = 0
OFF_QKV = OFF_SGU + 2 * BRANCH_DIM
OFF_HY = OFF_QKV + 3 * BRANCH_DIM
OFF_GATE = OFF_HY + (HY_ORDER + 1) * HY_DIM
IN_COLS = OFF_GATE + 3 * D_MODEL

LANES = 128
MXU_WIDTH = 256
VMEM_LIMIT_BYTES = 56 * 1024 * 1024

MXU_DTYPE = jnp.bfloat16
ACT_DTYPE = jnp.bfloat16
F32 = jnp.float32
HIGHEST = lax.Precision.HIGHEST

Q_ROWS_PER_BLOCK = 4
K_ROWS_PER_BLOCK = 12
ATT_TOK = Q_ROWS_PER_BLOCK * GRID_W


def _cparams(sem):
    return pltpu.CompilerParams(dimension_semantics=sem, vmem_limit_bytes=VMEM_LIMIT_BYTES)


def _const_spec(shape):
    nd = len(shape)
    return pl.BlockSpec(shape, lambda *_: (0,) * nd, pipeline_mode=pl.Buffered(1))


def _layer_spec(arr, layer):
    nd = arr.ndim - 1
    return pl.BlockSpec((None,) + arr.shape[1:], lambda *_: (layer,) + (0,) * nd, pipeline_mode=pl.Buffered(1))


def _dot(a, b, **kw):
    return jnp.dot(a, b, preferred_element_type=F32, **kw)


def _dot_nt(a, b):
    return lax.dot_general(a, b, (((1,), (1,)), ((), ())), preferred_element_type=F32)


def _gelu(x):
    return 0.5 * x * (1.0 + lax.erf(x * (1.0 / math.sqrt(2.0))))


def _mod_body(c_ref, w_ref, b_ref, o_ref):
    c = c_ref[...]
    s = c * jax.nn.sigmoid(c)
    o_ref[...] = _dot(s, w_ref[...], precision=HIGHEST) + b_ref[...]


def _modulation(cc, ada_w, ada_b):
    depth, d, n = ada_w.shape
    r = cc.shape[0]
    tn = 1024
    return pl.pallas_call(
        _mod_body,
        grid=(depth, n // tn),
        in_specs=[
            pl.BlockSpec((r, d), lambda i, j: (0, 0)),
            pl.BlockSpec((None, d, tn), lambda i, j: (i, 0, j)),
            pl.BlockSpec((None, 1, tn), lambda i, j: (i, 0, j)),
        ],
        out_specs=pl.BlockSpec((None, r, tn), lambda i, j: (i, 0, j)),
        out_shape=jax.ShapeDtypeStruct((depth, r, n), F32),
        compiler_params=_cparams(("arbitrary", "arbitrary")),
        name="adaln_modulation",
    )(cc, ada_w, ada_b.reshape(depth, 1, n))


def _mod_spec(row_fn, chunk):
    return pl.BlockSpec((None, 1, D_MODEL), lambda *g: (row_fn(*g), 0, chunk))


def _rope(acc, cos, sin):
    lane = lax.broadcasted_iota(jnp.int32, cos.shape, 1)
    first_half = (lane % 32) < 16
    outs = []
    for c0 in range(0, acc.shape[1], LANES):
        xc = acc[:, c0:c0 + LANES]
        partner = jnp.where(first_half, pltpu.roll(xc, LANES - 16, 1), pltpu.roll(xc, 16, 1))
        outs.append(xc * cos + partner * sin)
    return outs


def _spatial_gating(u, v, g_ref, w_ref, b_ref, o_ref):
    ms = jnp.mean(v * v, axis=-1, keepdims=True)
    vb = (v * lax.rsqrt(ms + NORM_EPS) * g_ref[...]).astype(MXU_DTYPE)
    for n in range(u.shape[0] // SGU_CHUNK):
        rs = slice(n * SGU_CHUNK, (n + 1) * SGU_CHUNK)
        for g in range(SGU_GROUPS):
            cs = slice(g * SGU_GROUP_DIM, (g + 1) * SGU_GROUP_DIM)
            mixed = _dot(w_ref[g], vb[rs, cs]) + b_ref[g]
            o_ref[rs, cs] = (u[rs, cs] * mixed).astype(ACT_DTYPE)


def _in_proj_body(*refs, segs, rope, sgu, n_out):
    x_ref, sh_ref, sc_ref, g_ref, w_ref = refs[:5]
    pos = 5
    if rope:
        cos_ref, sin_ref = refs[pos:pos + 2]
        pos += 2
    if sgu:
        sg_ref, sw_ref, sb_ref = refs[pos:pos + 3]
        pos += 3
    out_refs = refs[pos:pos + n_out]
    x = x_ref[...]
    ms = jnp.mean(x * x, axis=-1, keepdims=True)
    y = x * lax.rsqrt(ms + NORM_EPS) * g_ref[...]
    h = (y * (1.0 + sc_ref[...]) + sh_ref[...]).astype(MXU_DTYPE)
    for (c0, width, kind, oi) in segs:
        if kind == "sgu":
            u = _gelu(_dot(h, w_ref[:, c0:c0 + BRANCH_DIM]))
            v = _gelu(_dot(h, w_ref[:, c0 + BRANCH_DIM:c0 + 2 * BRANCH_DIM]))
            _spatial_gating(u, v, sg_ref, sw_ref, sb_ref, out_refs[oi])
            continue
        step = min(width, 512)
        for cc in range(0, width, step):
            acc = _dot(h, w_ref[:, c0 + cc:c0 + cc + step])
            if kind == "rope" and rope:
                for k, piece in enumerate(_rope(acc, cos_ref[...], sin_ref[...])):
                    out_refs[oi][:, cc + k * LANES:cc + (k + 1) * LANES] = piece.astype(ACT_DTYPE)
            else:
                out_refs[oi][:, cc:cc + step] = acc.astype(ACT_DTYPE)


def _in_proj(x, mod, mod_row, norm_g, w, layer, segs, rope_tabs, sgu, name):
    bx, l, d = x.shape
    tm = min(IN_PROJ_TM, l)
    assert all(s[0] + s[1] <= OFF_GATE for s in segs)
    n_out = max(s[3] for s in segs) + 1
    out_w = lambda s: BRANCH_DIM if s[2] == "sgu" else s[1]
    widths = [sum(out_w(s) for s in segs if s[3] == oi) for oi in range(n_out)]
    rope = rope_tabs is not None
    use_sgu = any(s[2] == "sgu" for s in segs)
    in_specs = [
        pl.BlockSpec((None, tm, d), lambda b, i: (b, i, 0)),
        _mod_spec(lambda b, i: mod_row(b), 0),
        _mod_spec(lambda b, i: mod_row(b), 1),
        _layer_spec(norm_g, layer),
        pl.BlockSpec((None, d, OFF_GATE), lambda *_: (layer, 0, 0), pipeline_mode=pl.Buffered(1)),
    ]
    args = [x, mod, mod, norm_g, w]
    if rope:
        in_specs += [pl.BlockSpec((tm, LANES), lambda b, i: (i, 0))] * 2
        args += list(rope_tabs)
    if use_sgu:
        in_specs += [_layer_spec(p, layer) for p in sgu]
        args += list(sgu)
    return pl.pallas_call(
        functools.partial(_in_proj_body, segs=segs, rope=rope, sgu=use_sgu, n_out=n_out),
        grid=(bx, l // tm),
        in_specs=in_specs,
        out_specs=[pl.BlockSpec((None, tm, wd), lambda b, i: (b, i, 0)) for wd in widths],
        out_shape=[jax.ShapeDtypeStruct((bx, l, wd), ACT_DTYPE) for wd in widths],
        compiler_params=_cparams(("arbitrary", "arbitrary")),
        name=name,
    )(*args)


IN_PROJ_TM = 1024
FULL_SEGS = (
    (OFF_SGU, 2 * BRANCH_DIM, "sgu", 0),
    (OFF_QKV, BRANCH_DIM, "rope", 1),
    (OFF_QKV + BRANCH_DIM, BRANCH_DIM, "rope", 2),
    (OFF_QKV + 2 * BRANCH_DIM, BRANCH_DIM, "plain", 3),
    (OFF_HY, 3 * HY_DIM, "plain", 4),
)
KV_SEGS = ((OFF_QKV + BRANCH_DIM, BRANCH_DIM, "plain", 0), (OFF_QKV + 2 * BRANCH_DIM, BRANCH_DIM, "plain", 1))


def _att_plan(rows):
    kr = min(NA_WIN_ROWS, rows)
    assert rows % Q_ROWS_PER_BLOCK == 0 and rows >= K_ROWS_PER_BLOCK and kr == NA_WIN_ROWS
    patterns, type_of, key_start = [], [], []
    for rb in range(0, rows, Q_ROWS_PER_BLOCK):
        ks = min(max(rb - NA_WIN_ROWS // 2, 0), rows - K_ROWS_PER_BLOCK)
        assert ks % Q_ROWS_PER_BLOCK == 0
        pat = []
        for qi in range(Q_ROWS_PER_BLOCK):
            rq = rb + qi
            r0 = min(max(rq - kr // 2, 0), rows - kr)
            for j in range(K_ROWS_PER_BLOCK):
                rk = ks + j
                pat.append(rk - rq + NA_WIN_ROWS - 1 if r0 <= rk < r0 + kr else None)
        pat = tuple(pat)
        if pat not in patterns:
            patterns.append(pat)
        type_of.append(patterns.index(pat))
        key_start.append(ks // Q_ROWS_PER_BLOCK)
    return patterns, type_of, key_start


def _bias_body(rpb_ref, o_ref, *, patterns):
    n_dr, n_dc = 2 * NA_WIN_ROWS - 1, 2 * NA_WIN_COLS - 1
    h = pl.program_id(0)
    qc = lax.broadcasted_iota(jnp.int32, (GRID_W, GRID_W), 0)
    kc = lax.broadcasted_iota(jnp.int32, (GRID_W, GRID_W), 1)
    cstart = jnp.clip(qc - NA_WIN_COLS // 2, 0, GRID_W - NA_WIN_COLS)
    col_ok = (kc >= cstart) & (kc < cstart + NA_WIN_COLS)
    dc = jnp.clip(kc - qc + NA_WIN_COLS - 1, 0, n_dc - 1)
    neg = jnp.full((GRID_W, GRID_W), NEG_INF, F32)
    tiles = []
    for dr in range(n_dr):
        t = jnp.zeros((GRID_W, GRID_W), F32)
        for d in range(n_dc):
            t = jnp.where(dc == d, rpb_ref[(h * n_dr + dr) * n_dc + d], t)
        tiles.append(jnp.where(col_ok, t * LOG2E, neg))
    for ty, pat in enumerate(patterns):
        for qi in range(Q_ROWS_PER_BLOCK):
            for j in range(K_ROWS_PER_BLOCK):
                dr = pat[qi * K_ROWS_PER_BLOCK + j]
                o_ref[ty, qi * GRID_W:(qi + 1) * GRID_W, j * GRID_W:(j + 1) * GRID_W] = neg if dr is None else tiles[dr]


def _bias_tables(rpb, patterns):
    nt = len(patterns)
    return pl.pallas_call(
        functools.partial(_bias_body, patterns=patterns),
        grid=(NA_HEADS,),
        in_specs=[pl.BlockSpec(memory_space=pltpu.SMEM)],
        out_specs=pl.BlockSpec((nt, None, ATT_TOK, K_ROWS_PER_BLOCK * GRID_W), lambda h: (0, h, 0, 0)),
        out_shape=jax.ShapeDtypeStruct((nt, NA_HEADS, ATT_TOK, K_ROWS_PER_BLOCK * GRID_W), F32),
        compiler_params=_cparams(("arbitrary",)),
        name="attention_bias_tables",
    )(rpb.reshape(-1))


def _head_mask(shape, hh):
    lane = lax.broadcasted_iota(jnp.int32, shape, 1)
    return (lane >= hh * NA_HEAD_DIM) & (lane < (hh + 1) * NA_HEAD_DIM)


LOG2E = math.log2(math.e)
QK_SCALE = NA_HEAD_DIM ** -0.5 * LOG2E


def _pair_attention(qp, keys, biases, values):
    m_rows = qp.shape[0]
    q2 = jnp.concatenate([jnp.where(_head_mask(qp.shape, hh), qp, jnp.zeros_like(qp)) for hh in range(2)], axis=0)
    s_all = _dot_nt(q2, jnp.concatenate(keys, axis=0))
    scores, c0 = [], 0
    for j, kj in enumerate(keys):
        s = s_all[:, c0:c0 + kj.shape[0]]
        c0 += kj.shape[0]
        if biases[0][j] is not None:
            s = s + jnp.concatenate([biases[0][j], biases[1][j]], axis=0)
        scores.append(s)
    m = functools.reduce(jnp.maximum, [jnp.max(s, axis=-1, keepdims=True) for s in scores])
    e = jnp.concatenate([jnp.exp2(s - m).astype(MXU_DTYPE) for s in scores], axis=1)
    vcat = jnp.concatenate(values, axis=0)
    acc = _dot(e, jnp.concatenate([vcat, jnp.ones_like(vcat)], axis=1))
    out = acc[:, :LANES] / acc[:, LANES:]
    return jnp.where(_head_mask((m_rows, LANES), 0), out[:m_rows], out[m_rows:])


ATT_BATCH = 8


def _attn_body(ty_ref, kb_ref, q_ref, k0, k1, k2, v0, v1, v2, kc_ref, vc_ref, bias_ref, o_ref):
    del ty_ref, kb_ref
    for p in range(NA_HEADS // 2):
        cs = slice(p * LANES, (p + 1) * LANES)
        biases = [[bias_ref[2 * p + hh, :, j * ATT_TOK:(j + 1) * ATT_TOK] for j in range(3)] + [None]
                  for hh in range(2)]
        for e in range(q_ref.shape[0]):
            qp = q_ref[e, :, cs] * QK_SCALE
            keys = [k0[e, :, cs], k1[e, :, cs], k2[e, :, cs], kc_ref[e, :, cs]]
            values = [v0[e, :, cs], v1[e, :, cs], v2[e, :, cs], vc_ref[e, :, cs]]
            o_ref[e, :, cs] = _pair_attention(qp, keys, biases, values).astype(ACT_DTYPE)


def _neighborhood_attention(q, k, v, kc, vc, bias, type_of, key_start):
    b, l, _ = q.shape
    lc = kc.shape[1]
    n_blk = l // ATT_TOK
    nb = math.gcd(b, ATT_BATCH)
    tok = lambda off: pl.BlockSpec((nb, ATT_TOK, BRANCH_DIM), lambda r, bb, ty, kb: (bb, kb[r] + off, 0))
    ctx = pl.BlockSpec((nb, lc, BRANCH_DIM), lambda r, bb, ty, kb: (bb, 0, 0))
    grid_spec = pltpu.PrefetchScalarGridSpec(
        num_scalar_prefetch=2,
        grid=(n_blk, b // nb),
        in_specs=[
            pl.BlockSpec((nb, ATT_TOK, BRANCH_DIM), lambda r, bb, ty, kb: (bb, r, 0)),
            tok(0), tok(1), tok(2), tok(0), tok(1), tok(2), ctx, ctx,
            pl.BlockSpec((None, NA_HEADS, ATT_TOK, K_ROWS_PER_BLOCK * GRID_W), lambda r, bb, ty, kb: (ty[r], 0, 0, 0)),
        ],
        out_specs=pl.BlockSpec((nb, ATT_TOK, BRANCH_DIM), lambda r, bb, ty, kb: (bb, r, 0)),
    )
    return pl.pallas_call(
        _attn_body,
        grid_spec=grid_spec,
        out_shape=jax.ShapeDtypeStruct((b, l, BRANCH_DIM), ACT_DTYPE),
        compiler_params=_cparams(("arbitrary", "arbitrary")),
        name="neighborhood_attention",
    )(jnp.asarray(type_of, jnp.int32), jnp.asarray(key_start, jnp.int32), q, k, k, k, v, v, v, kc, vc, bias)


def _ctx_attn_body(q_ref, k_ref, v_ref, o_ref):
    for p in range(NA_HEADS // 2):
        cs = slice(p * LANES, (p + 1) * LANES)
        qp = q_ref[:, cs] * QK_SCALE
        o_ref[:, cs] = _pair_attention(qp, [k_ref[:, cs]], [[None], [None]], [v_ref[:, cs]]).astype(ACT_DTYPE)


def _context_attention(q, k, v):
    b, lc, _ = q.shape
    spec = pl.BlockSpec((None, lc, BRANCH_DIM), lambda bb: (bb, 0, 0))
    return pl.pallas_call(
        _ctx_attn_body,
        grid=(b,),
        in_specs=[spec, spec, spec],
        out_specs=spec,
        out_shape=jax.ShapeDtypeStruct((b, lc, BRANCH_DIM), ACT_DTYPE),
        compiler_params=_cparams(("arbitrary",)),
        name="context_attention",
    )(q, k, v)


HY_CW = 256
HY_TM = 512
HY_BLOCKS = 4
HY_MIN_BLOCK = 256
CONV_HALO = 16


def _hy_blocks(l):
    return HY_BLOCKS if l % (HY_BLOCKS * HY_MIN_BLOCK) == 0 else 1


@functools.lru_cache(maxsize=None)
def _dft_consts_np(m):
    n = 2 * m
    idx = np.arange(m, dtype=np.int64)
    ang = ((idx[:, None] * idx[None, :]) % n).astype(np.float64) * (2.0 * math.pi / n)
    cs = np.concatenate([np.cos(ang), -np.sin(ang)], axis=1).astype(np.float32)
    sign = np.where(idx % 2 == 0, 1.0, -1.0).astype(np.float32).reshape(m, 1)
    wf = np.where(idx == 0, 1.0 / n, 2.0 / n).astype(np.float32).reshape(m, 1)
    return cs, sign, wf


def _dft_consts(m):
    cs, sign, wf = _dft_consts_np(m)
    return jnp.asarray(cs.astype(MXU_DTYPE)), jnp.asarray(sign), jnp.asarray(wf)


def _filter_consts(l):
    t = np.linspace(0.0, 1.0, l, dtype=np.float32)[:, None]
    n_bands = (HY_EMB - 1) // 2
    bands = np.linspace(1e-4, n_bands - 1, n_bands, dtype=np.float32)[None, :]
    ang = np.float32(2.0 * math.pi) * np.arange(l, dtype=np.float32)[:, None] / np.float32(l) * bands
    z = np.concatenate([t, np.cos(ang), -np.sin(ang)], axis=-1).astype(np.float32)
    z = np.pad(z, ((0, 0), (0, LANES - HY_EMB)))
    max_decay = math.log(HY_DECAY_TARGET) / HY_FAST_DECAY_PCT
    min_decay = math.log(HY_DECAY_TARGET) / HY_SLOW_DECAY_PCT
    deltas = np.abs(np.linspace(min_decay, max_decay, HY_DIM, dtype=np.float32)).reshape(1, HY_DIM)
    return jnp.asarray(z), jnp.asarray(t), jnp.asarray(deltas)


def _filter_body(z_ref, w1_ref, b1_ref, w2_ref, b2_ref, fr_ref, w3a_ref, w3b_ref, t_ref, dl_ref,
                 sg_ref, wf_ref, cs_ref, gre_ref, gim_ref, gny_ref, h_ref, *, p):
    l = z_ref.shape[0]
    m = l // p

    @pl.when((pl.program_id(0) == 0) & (pl.program_id(1) == 0))
    def _():
        fr = fr_ref[...]
        h1 = jnp.sin(fr * (_dot(z_ref[...], w1_ref[...], precision=HIGHEST) + b1_ref[...]))
        h_ref[...] = jnp.sin(fr * (_dot(h1, w2_ref[...], precision=HIGHEST) + b2_ref[...]))

    h = h_ref[...]
    decay = jnp.exp(-t_ref[...] * dl_ref[...])
    h0 = _dot(h, w3a_ref[...], precision=HIGHEST) * decay
    h1 = _dot(h, w3b_ref[...], precision=HIGHEST) * decay
    row = lax.broadcasted_iota(jnp.int32, h1.shape, 0)
    den = (jnp.sum(jnp.abs(h0), axis=0, keepdims=True)
           + jnp.sum(jnp.where(row == 0, 0.0, jnp.abs(h1)), axis=0, keepdims=True))
    h0, h1 = h0 / den, h1 / den
    sg, wf = sg_ref[...], wf_ref[...]
    u, v, pc, q, un, pn, r0, r1 = [], [], [], [], [], [], [], []
    for b in range(p):
        x0, x1 = h0[b * m:(b + 1) * m], h1[b * m:(b + 1) * m]
        x0b, x1b = x0.astype(MXU_DTYPE), x1.astype(MXU_DTYPE)
        u.append(_dot(cs_ref[:, :m], x0b))
        v.append(_dot(cs_ref[:, m:], x0b))
        pc.append(_dot(cs_ref[:, :m], x1b))
        q.append(_dot(cs_ref[:, m:], x1b))
        un.append(jnp.sum(sg * x0, axis=0, keepdims=True))
        pn.append(jnp.sum(sg * x1, axis=0, keepdims=True))
        r0.append(x0[0:1])
        r1.append(x1[0:1])
    for li, d in enumerate(range(-(p - 1), p)):
        if d > 0:
            gre = u[d] + sg * (u[d - 1] - r0[d - 1])
            gim = v[d] + sg * v[d - 1]
            gny = un[d] + un[d - 1] - r0[d - 1]
        elif d == 0:
            gre = u[0] + pc[0] - r1[0]
            gim = v[0] - q[0]
            gny = un[0] + pn[0] - r1[0]
        else:
            gre = pc[-d] + sg * (pc[-d - 1] - r1[-d - 1])
            gim = -q[-d] - sg * q[-d - 1]
            gny = pn[-d] + pn[-d - 1] - r1[-d - 1]
        gre_ref[li] = (wf * gre).astype(gre_ref.dtype)
        gim_ref[li] = (wf * gim).astype(gim_ref.dtype)
        gny_ref[li] = gny * (1.0 / (2 * m))


def _hyena_filters(l, dft, fconsts, w1, b1, w2, b2, w3, freq):
    cs, sign, wf = dft
    z, t, deltas = fconsts
    p = _hy_blocks(l)
    m, nl = l // p, 2 * p - 1
    cw = HY_CW
    nb = HY_DIM // cw
    w1p = jnp.pad(w1, ((0, LANES - HY_EMB), (0, 0)))
    hid = HY_FILTER_HIDDEN
    small = lambda shape: pl.BlockSpec(shape, lambda n, cb: (0,) * len(shape))
    spec_out = pl.BlockSpec((None, nl, m, cw), lambda n, cb: (n, 0, 0, cb))
    return pl.pallas_call(
        functools.partial(_filter_body, p=p),
        grid=(HY_ORDER, nb),
        in_specs=[
            small((l, LANES)), small((LANES, hid)), small((1, hid)), small((hid, hid)), small((1, hid)), small((1, hid)),
            pl.BlockSpec((hid, cw), lambda n, cb: (0, n * 2 * nb + cb)),
            pl.BlockSpec((hid, cw), lambda n, cb: (0, n * 2 * nb + nb + cb)),
            small((l, 1)),
            pl.BlockSpec((1, cw), lambda n, cb: (0, cb)),
            small((m, 1)), small((m, 1)),
            _const_spec((m, 2 * m)),
        ],
        out_specs=[spec_out, spec_out, pl.BlockSpec((None, nl, 1, cw), lambda n, cb: (n, 0, 0, cb))],
        out_shape=[jax.ShapeDtypeStruct((HY_ORDER, nl, m, HY_DIM), ACT_DTYPE)] * 2
        + [jax.ShapeDtypeStruct((HY_ORDER, nl, 1, HY_DIM), F32)],
        scratch_shapes=[pltpu.VMEM((l, hid), F32)],
        compiler_params=_cparams(("arbitrary", "arbitrary")),
        name="hyena_filter_spectrum",
    )(z, w1p, b1.reshape(1, hid), w2, b2.reshape(1, hid), freq.reshape(1, hid), w3, w3, t, deltas, sign, wf, cs)


def _conv3_rows(p_ref, w_ref, b_ref, r0, rows):
    l = p_ref.shape[0]
    lo, hi = max(r0 - CONV_HALO, 0), min(r0 + rows + CONV_HALO, l)
    x = p_ref[lo:hi, :].astype(F32)
    n = hi - lo
    prev, nxt = pltpu.roll(x, 1, 0), pltpu.roll(x, n - 1, 0)
    row = lax.broadcasted_iota(jnp.int32, x.shape, 0)
    if lo == 0:
        prev = jnp.where(row == 0, 0.0, prev)
    if hi == l:
        nxt = jnp.where(row == n - 1, 0.0, nxt)
    y = prev * w_ref[0:1, :] + x * w_ref[1:2, :] + nxt * w_ref[2:3, :] + b_ref[...]
    return y[r0 - lo:r0 - lo + rows]


def _hyena_body(pv_ref, p1_ref, p2_ref, wv_ref, w1_ref, w2_ref, bv_ref, b1_ref, b2_ref, skip_ref,
                gre_ref, gim_ref, gny_ref, sg_ref, cs_ref, o_ref, zf_ref, zb_ref, zz_ref, y_ref, *, p):
    l, cw = pv_ref.shape
    m = l // p
    tm = min(HY_TM, m)
    offs = range(0, m, tm)
    zn = [jnp.zeros((1, cw), F32) for _ in range(p)]

    def put_z(a, c0, z):
        rs = slice(a * m + c0, a * m + c0 + tm)
        zf_ref[rs] = z
        zb_ref[rs] = z.astype(MXU_DTYPE)
        zn[a] = zn[a] + jnp.sum(sg_ref[c0:c0 + tm] * z, axis=0, keepdims=True)

    for a in range(p):
        for c0 in offs:
            put_z(a, c0, _conv3_rows(pv_ref, wv_ref, bv_ref, a * m + c0, tm))
    gates = ((p1_ref, w1_ref, b1_ref), (p2_ref, w2_ref, b2_ref))
    for n, (p_ref, w_ref, b_ref) in enumerate(gates):
        for a in range(p):
            zblk = zb_ref[a * m:(a + 1) * m]
            for c0 in offs:
                rs = slice(a * m + c0, a * m + c0 + tm)
                zz_ref[0, rs] = _dot(cs_ref[c0:c0 + tm, :m], zblk).astype(zz_ref.dtype)
                zz_ref[1, rs] = _dot(cs_ref[c0:c0 + tm, m:], zblk).astype(zz_ref.dtype)
        for a2 in range(p):
            for c0 in offs:
                yc = ys = None
                for a in range(p):
                    li = a2 - a + p - 1
                    gre, gim = gre_ref[n, li, c0:c0 + tm], gim_ref[n, li, c0:c0 + tm]
                    zc, zs = zz_ref[0, a * m + c0:a * m + c0 + tm], zz_ref[1, a * m + c0:a * m + c0 + tm]
                    tc, ts = zc * gre - zs * gim, zc * gim + zs * gre
                    yc, ys = (tc, ts) if yc is None else (yc + tc, ys + ts)
                y_ref[a2, c0:c0 + tm] = yc.astype(MXU_DTYPE)
                y_ref[a2, m + c0:m + c0 + tm] = ys.astype(MXU_DTYPE)
        nyq = [functools.reduce(lambda s, t: s + t, [zn[a] * gny_ref[n, a2 - a + p - 1] for a in range(p)])
               for a2 in range(p)]
        zn = [jnp.zeros((1, cw), F32) for _ in range(p)]
        for a2 in range(p):
            for c0 in offs:
                rs = slice(a2 * m + c0, a2 * m + c0 + tm)
                y = _dot(cs_ref[c0:c0 + tm, :], y_ref[a2]) + sg_ref[c0:c0 + tm] * nyq[a2]
                gate = _conv3_rows(p_ref, w_ref, b_ref, rs.start, tm)
                z = gate * (y + zf_ref[rs] * skip_ref[n:n + 1, :])
                if n + 1 < HY_ORDER:
                    put_z(a2, c0, z)
                else:
                    o_ref[rs] = z.astype(ACT_DTYPE)


def _hyena(x, conv_w, conv_b, skip, filters, dft):
    bx, l, _ = x.shape
    cs, sign, _ = dft
    gre, gim, gny = filters
    p = _hy_blocks(l)
    m, nl = l // p, 2 * p - 1
    cw = HY_CW
    nb = HY_DIM // cw
    pspec = lambda part: pl.BlockSpec((None, l, cw), lambda cb, b: (b, 0, part * nb + cb))
    wspec = lambda part: pl.BlockSpec((3, cw), lambda cb, b: (0, part * nb + cb))
    bspec = lambda part: pl.BlockSpec((1, cw), lambda cb, b: (0, part * nb + cb))
    fspec = pl.BlockSpec((HY_ORDER, nl, m, cw), lambda cb, b: (0, 0, 0, cb), pipeline_mode=pl.Buffered(1))
    return pl.pallas_call(
        functools.partial(_hyena_body, p=p),
        grid=(nb, bx),
        in_specs=[
            pspec(0), pspec(1), pspec(2), wspec(0), wspec(1), wspec(2), bspec(0), bspec(1), bspec(2),
            pl.BlockSpec((HY_ORDER, cw), lambda cb, b: (0, cb)),
            fspec, fspec,
            pl.BlockSpec((HY_ORDER, nl, 1, cw), lambda cb, b: (0, 0, 0, cb)),
            _const_spec((m, 1)), _const_spec((m, 2 * m)),
        ],
        out_specs=pl.BlockSpec((None, l, cw), lambda cb, b: (b, 0, cb)),
        out_shape=jax.ShapeDtypeStruct((bx, l, HY_DIM), ACT_DTYPE),
        scratch_shapes=[pltpu.VMEM((l, cw), F32), pltpu.VMEM((l, cw), MXU_DTYPE), pltpu.VMEM((2, l, cw), ACT_DTYPE),
                        pltpu.VMEM((p, 2 * m, cw), MXU_DTYPE)],
        compiler_params=_cparams(("arbitrary", "arbitrary")),
        name="hyena_long_conv",
    )(x, x, x, conv_w, conv_w, conv_w, conv_b.reshape(1, -1), conv_b.reshape(1, -1), conv_b.reshape(1, -1),
      skip, gre, gim, gny, sign, cs)


MERGE_TM = 1024


MERGE_TN = 512


def _merge_body(x_ref, a_ref, at_ref, hy_ref, sh_ref, sc_ref, g1_ref, ng_ref, wg0_ref, wg1_ref, wg2_ref,
                wb_ref, wo_ref, o_ref, m_ref):
    x = x_ref[...]
    ms = jnp.mean(x * x, axis=-1, keepdims=True)
    y = x * lax.rsqrt(ms + NORM_EPS) * ng_ref[...]
    h = (y * (1.0 + sc_ref[...]) + sh_ref[...]).astype(MXU_DTYPE)
    for c0 in range(0, D_MODEL, MERGE_TN):
        cs = slice(c0, c0 + MERGE_TN)
        m = None
        for j, (br_ref, wg_ref) in enumerate(((a_ref, wg0_ref), (at_ref, wg1_ref), (hy_ref, wg2_ref))):
            term = jax.nn.sigmoid(_dot(h, wg_ref[:, cs])) * _dot(br_ref[...], wb_ref[j, :, cs])
            m = term if m is None else m + term
        m_ref[:, cs] = m.astype(MXU_DTYPE)
    o_ref[...] = x + g1_ref[...] * _dot(m_ref[...], wo_ref[...])


def _merge(x, a, attn, hy, mod, mod_row, norm_g, w_in, wb, wo, layer):
    bx, l, d = x.shape
    tm = min(MERGE_TM, l)
    tile = lambda w: pl.BlockSpec((None, tm, w), lambda b, i: (b, i, 0))
    gate_w = lambda j: pl.BlockSpec((None, d, d), lambda *_: (layer, 0, OFF_GATE // d + j), pipeline_mode=pl.Buffered(1))
    return pl.pallas_call(
        _merge_body,
        grid=(bx, l // tm),
        in_specs=[tile(d), tile(BRANCH_DIM), tile(BRANCH_DIM), tile(BRANCH_DIM),
                  _mod_spec(lambda b, i: mod_row(b), 0), _mod_spec(lambda b, i: mod_row(b), 1),
                  _mod_spec(lambda b, i: mod_row(b), 2), _layer_spec(norm_g, layer),
                  gate_w(0), gate_w(1), gate_w(2), _layer_spec(wb, layer), _layer_spec(wo, layer)],
        out_specs=tile(d),
        out_shape=jax.ShapeDtypeStruct((bx, l, d), F32),
        scratch_shapes=[pltpu.VMEM((tm, d), MXU_DTYPE)],
        compiler_params=_cparams(("arbitrary", "arbitrary")),
        name="branch_merge",
    )(x, a, attn, hy, mod, mod, mod, norm_g, w_in, w_in, w_in, wb, wo)


FFN_TM = 1024
FFN_TN = 256


def _ffn_body(xp_ref, x_ref, xn_ref, sh_ref, sc_ref, g2_ref, ng_ref, wu_ref, cw_ref, cb_ref, wd_ref, fg_ref,
              o_ref, h_ref, gt_ref, *, final, seq_len):
    tm = x_ref.shape[0]
    halo = CONV_HALO
    n_ext = tm + 2 * halo

    def norm_mod(x):
        ms = jnp.mean(x * x, axis=-1, keepdims=True)
        y = x * lax.rsqrt(ms + NORM_EPS) * ng_ref[...]
        return (y * (1.0 + sc_ref[...]) + sh_ref[...]).astype(MXU_DTYPE)

    h_ref[0:halo] = norm_mod(xp_ref[...])
    h_ref[halo:halo + tm] = norm_mod(x_ref[...])
    h_ref[halo + tm:n_ext] = norm_mod(xn_ref[...])
    row = lax.broadcasted_iota(jnp.int32, (n_ext, 1), 0)
    pos = (pl.program_id(1) * tm + row + (seq_len - halo)) % seq_len
    first, last = pos == 0, pos == seq_len - 1
    for j in range(D_FF // FFN_TN):
        cs = slice(j * FFN_TN, (j + 1) * FFN_TN)
        a = _dot(h_ref[...], wu_ref[:, cs])
        prev = jnp.where(first, 0.0, pltpu.roll(a, 1, 0))
        nxt = jnp.where(last, 0.0, pltpu.roll(a, n_ext - 1, 0))
        a = prev * cw_ref[0:1, cs] + a * cw_ref[1:2, cs] + nxt * cw_ref[2:3, cs] + cb_ref[:, cs]
        gate = _dot(h_ref[halo:halo + tm], wu_ref[:, D_FF + j * FFN_TN:D_FF + (j + 1) * FFN_TN])
        gt_ref[:, cs] = (_gelu(a[halo:halo + tm]) * gate).astype(MXU_DTYPE)
    xn = x_ref[...] + g2_ref[...] * _dot(gt_ref[...], wd_ref[...])
    if final:
        ms = jnp.mean(xn * xn, axis=-1, keepdims=True)
        xn = xn * lax.rsqrt(ms + NORM_EPS) * fg_ref[...]
    o_ref[...] = xn


def _ffn(x, mod, mod_row, norm_g, w_up, conv_w, conv_b, w_down, layer, final_g, final, seq_len):
    bx, l, d = x.shape
    tm = min(FFN_TM, l)
    halo = CONV_HALO
    per, last_blk = tm // halo, l // halo - 1
    tile = pl.BlockSpec((None, tm, d), lambda b, i: (b, i, 0))
    return pl.pallas_call(
        functools.partial(_ffn_body, final=final, seq_len=seq_len),
        grid=(bx, l // tm),
        in_specs=[
            pl.BlockSpec((None, halo, d), lambda b, i: (b, jnp.maximum(i * per - 1, 0), 0)),
            tile,
            pl.BlockSpec((None, halo, d), lambda b, i: (b, jnp.minimum((i + 1) * per, last_blk), 0)),
            _mod_spec(lambda b, i: mod_row(b), 3), _mod_spec(lambda b, i: mod_row(b), 4),
            _mod_spec(lambda b, i: mod_row(b), 5),
            _layer_spec(norm_g, layer), _layer_spec(w_up, layer), _layer_spec(conv_w, layer),
            _layer_spec(conv_b, layer), _layer_spec(w_down, layer), _const_spec((1, d)),
        ],
        out_specs=tile,
        out_shape=jax.ShapeDtypeStruct((bx, l, d), F32),
        scratch_shapes=[pltpu.VMEM((tm + 2 * halo, d), MXU_DTYPE), pltpu.VMEM((tm, D_FF), MXU_DTYPE)],
        compiler_params=_cparams(("arbitrary", "arbitrary")),
        name="conv_ffn",
    )(x, x, x, mod, mod, mod, norm_g, w_up, conv_w, conv_b, w_down, final_g.reshape(1, d))


def _rope_tables(l):
    half = NA_HEAD_DIM // 2
    quarter = half // 2
    inv_freq = np.float32(ROPE_THETA) ** (-np.arange(quarter, dtype=np.float32) * np.float32(2.0) / np.float32(half))
    pos = np.arange(l)
    ang_r = (pos // GRID_W).astype(np.float32)[:, None] * inv_freq[None, :]
    ang_c = (pos % GRID_W).astype(np.float32)[:, None] * inv_freq[None, :]
    cos_h = np.concatenate([np.cos(ang_r), np.cos(ang_r), np.cos(ang_c), np.cos(ang_c)], axis=-1)
    sin_h = np.concatenate([-np.sin(ang_r), np.sin(ang_r), -np.sin(ang_c), np.sin(ang_c)], axis=-1)
    reps = LANES // NA_HEAD_DIM
    return (jnp.asarray(np.tile(cos_h, (1, reps)).astype(np.float32)),
            jnp.asarray(np.tile(sin_h, (1, reps)).astype(np.float32)))


def kernel(x, c, ctx, c_ctx, ada_w, ada_b, norm1_g, norm2_g, w_in, sgu_norm_g, sgu_w, sgu_b, na_rpb, hy_conv_w, hy_conv_b, hy_filt_w1, hy_filt_b1, hy_filt_w2, hy_filt_b2, hy_filt_w3, hy_sin_freq, hy_skip, w_branch, w_out, ffn_w_up, ffn_conv_w, ffn_conv_b, ffn_w_down, final_norm_g):
    b, l, d = x.shape
    lc = ctx.shape[1]
    assert d == D_MODEL and l % ATT_TOK == 0 and l % GRID_W == 0

    n_rows = -(-(b + 1) // 8) * 8
    cc = jnp.concatenate([c, c_ctx[None, :], jnp.zeros((n_rows - b - 1, d), F32)], axis=0)
    mod = _modulation(cc, ada_w, ada_b).reshape(DEPTH * n_rows, 1, N_MOD * d)

    rope_tabs = _rope_tables(l)
    dft_l, fc_l = _dft_consts(l // _hy_blocks(l)), _filter_consts(l)
    dft_c, fc_c = _dft_consts(lc // _hy_blocks(lc)), _filter_consts(lc)
    patterns, type_of, key_start = _att_plan(l // GRID_W)

    w_in_b, wb_b, wo_b = w_in.astype(MXU_DTYPE), w_branch.astype(MXU_DTYPE), w_out.astype(MXU_DTYPE)
    wup_b, wdn_b = ffn_w_up.astype(MXU_DTYPE), ffn_w_down.astype(MXU_DTYPE)
    n1g, n2g = norm1_g.reshape(DEPTH, 1, d), norm2_g.reshape(DEPTH, 1, d)
    ffn_cb = ffn_conv_b.reshape(DEPTH, 1, D_FF)
    sgu = (sgu_norm_g.reshape(DEPTH, 1, BRANCH_DIM), sgu_w.astype(MXU_DTYPE),
           sgu_b.reshape(DEPTH, SGU_GROUPS, SGU_CHUNK, 1))

    grp = math.gcd(b, max(1, l // lc))
    stack = lambda t: t.reshape(b // grp, grp * lc, t.shape[-1])
    unstack = lambda t: t.reshape(b, lc, t.shape[-1])

    xc = ctx
    for i in range(DEPTH):
        update_ctx = i < DEPTH - 1
        lat_row = lambda bb, i=i: i * n_rows + bb
        ctx_row = lambda bb, i=i: i * n_rows + b
        filt_args = (hy_filt_w1[i], hy_filt_b1[i], hy_filt_w2[i], hy_filt_b2[i], hy_filt_w3[i], hy_sin_freq[i])
        bias = _bias_tables(na_rpb[i], patterns)

        if update_ctx:
            a_c, q_c, k_c, v_c, hy_c = map(unstack, _in_proj(stack(xc), mod, ctx_row, n1g, w_in_b, i, FULL_SEGS, None,
                                                             sgu, "context_in_proj"))
        else:
            k_c, v_c = map(unstack, _in_proj(stack(xc), mod, ctx_row, n1g, w_in_b, i, KV_SEGS, None, None,
                                             "context_kv_proj"))

        a, q, k, v, hy = _in_proj(x, mod, lat_row, n1g, w_in_b, i, FULL_SEGS, rope_tabs, sgu, "latent_in_proj")
        attn = _neighborhood_attention(q, k, v, k_c, v_c, bias, type_of, key_start)
        filt = _hyena_filters(l, dft_l, fc_l, *filt_args)
        hyo = _hyena(hy, hy_conv_w[i], hy_conv_b[i], hy_skip[i], filt, dft_l)
        x = _merge(x, a, attn, hyo, mod, lat_row, n1g, w_in_b, wb_b, wo_b, i)
        x = _ffn(x, mod, lat_row, n2g, wup_b, ffn_conv_w, ffn_cb, wdn_b, i, final_norm_g,
                 final=not update_ctx, seq_len=l)

        if update_ctx:
            attn_c = _context_attention(q_c, k_c, v_c)
            filt_c = _hyena_filters(lc, dft_c, fc_c, *filt_args)
            hyo_c = _hyena(hy_c, hy_conv_w[i], hy_conv_b[i], hy_skip[i], filt_c, dft_c)
            xs = _merge(stack(xc), stack(a_c), stack(attn_c), stack(hyo_c), mod, ctx_row, n1g, w_in_b, wb_b, wo_b, i)
            xc = unstack(_ffn(xs, mod, ctx_row, n2g, wup_b, ffn_conv_w, ffn_cb, wdn_b, i, final_norm_g,
                              final=False, seq_len=lc))
    return x
```

```python
import functools
import math

import jax
import jax.numpy as jnp
import numpy as np
from jax import lax
from jax.experimental import pallas as pl
from jax.experimental.pallas import tpu as pltpu

D_MODEL = 1024
DEPTH = 2
GRID_W = 64
NORM_EPS = 1e-6
N_MOD = 6
BRANCH_DIM = D_MODEL // 2
SGU_GROUP_DIM = 128
SGU_GROUPS = BRANCH_DIM // SGU_GROUP_DIM
SGU_CHUNK = 128
NA_HEAD_DIM = 64
NA_HEADS = BRANCH_DIM // NA_HEAD_DIM
NA_WIN_ROWS = 8
NA_WIN_COLS = 16
ROPE_THETA = 10000.0
NEG_INF = -1e30
HY_DIM = BRANCH_DIM
HY_ORDER = 2
HY_EMB = 33
HY_FILTER_HIDDEN = 64
HY_FAST_DECAY_PCT = 0.3
HY_SLOW_DECAY_PCT = 1.5
HY_DECAY_TARGET = 1e-2
D_FF = 128 * ((8 * D_MODEL + 3 * 128 - 1) // (3 * 128))
OFF_SGU = 0
OFF_QKV = OFF_SGU + 2 * BRANCH_DIM
OFF_HY = OFF_QKV + 3 * BRANCH_DIM
OFF_GATE = OFF_HY + (HY_ORDER + 1) * HY_DIM
IN_COLS = OFF_GATE + 3 * D_MODEL

LANES = 128
MXU_WIDTH = 256
VMEM_LIMIT_BYTES = 56 * 1024 * 1024

MXU_DTYPE = jnp.bfloat16
ACT_DTYPE = jnp.bfloat16
F32 = jnp.float32
HIGHEST = lax.Precision.HIGHEST

Q_ROWS_PER_BLOCK = 4
K_ROWS_PER_BLOCK = 12
ATT_TOK = Q_ROWS_PER_BLOCK * GRID_W


def _cparams(sem):
    return pltpu.CompilerParams(dimension_semantics=sem, vmem_limit_bytes=VMEM_LIMIT_BYTES)


def _const_spec(shape):
    nd = len(shape)
    return pl.BlockSpec(shape, lambda *_: (0,) * nd, pipeline_mode=pl.Buffered(1))


def _layer_spec(arr, layer):
    nd = arr.ndim - 1
    return pl.BlockSpec((None,) + arr.shape[1:], lambda *_: (layer,) + (0,) * nd, pipeline_mode=pl.Buffered(1))


def _dot(a, b, **kw):
    return jnp.dot(a, b, preferred_element_type=F32, **kw)


def _dot_nt(a, b):
    return lax.dot_general(a, b, (((1,), (1,)), ((), ())), preferred_element_type=F32)


def _gelu(x):
    return 0.5 * x * (1.0 + lax.erf(x * (1.0 / math.sqrt(2.0))))


def _mod_body(c_ref, w_ref, b_ref, o_ref):
    c = c_ref[...]
    s = c * jax.nn.sigmoid(c)
    o_ref[...] = _dot(s, w_ref[...], precision=HIGHEST) + b_ref[...]


def _modulation(cc, ada_w, ada_b):
    depth, d, n = ada_w.shape
    r = cc.shape[0]
    tn = 1024
    return pl.pallas_call(
        _mod_body,
        grid=(depth, n // tn),
        in_specs=[
            pl.BlockSpec((r, d), lambda i, j: (0, 0)),
            pl.BlockSpec((None, d, tn), lambda i, j: (i, 0, j)),
            pl.BlockSpec((None, 1, tn), lambda i, j: (i, 0, j)),
        ],
        out_specs=pl.BlockSpec((None, r, tn), lambda i, j: (i, 0, j)),
        out_shape=jax.ShapeDtypeStruct((depth, r, n), F32),
        compiler_params=_cparams(("arbitrary", "arbitrary")),
        name="adaln_modulation",
    )(cc, ada_w, ada_b.reshape(depth, 1, n))


def _mod_spec(row_fn, chunk):
    return pl.BlockSpec((None, 1, D_MODEL), lambda *g: (row_fn(*g), 0, chunk))


def _rope(acc, cos, sin):
    lane = lax.broadcasted_iota(jnp.int32, cos.shape, 1)
    first_half = (lane % 32) < 16
    outs = []
    for c0 in range(0, acc.shape[1], LANES):
        xc = acc[:, c0:c0 + LANES]
        partner = jnp.where(first_half, pltpu.roll(xc, LANES - 16, 1), pltpu.roll(xc, 16, 1))
        outs.append(xc * cos + partner * sin)
    return outs


def _spatial_gating(u, v, g_ref, w_ref, b_ref, o_ref):
    ms = jnp.mean(v * v, axis=-1, keepdims=True)
    vb = (v * lax.rsqrt(ms + NORM_EPS) * g_ref[...]).astype(MXU_DTYPE)
    for n in range(u.shape[0] // SGU_CHUNK):
        rs = slice(n * SGU_CHUNK, (n + 1) * SGU_CHUNK)
        for g in range(SGU_GROUPS):
            cs = slice(g * SGU_GROUP_DIM, (g + 1) * SGU_GROUP_DIM)
            mixed = _dot(w_ref[g], vb[rs, cs]) + b_ref[g]
            o_ref[rs, cs] = (u[rs, cs] * mixed).astype(ACT_DTYPE)


def _in_proj_body(*refs, segs, rope, sgu, n_out):
    x_ref, sh_ref, sc_ref, g_ref, w_ref = refs[:5]
    pos = 5
    if rope:
        cos_ref, sin_ref = refs[pos:pos + 2]
        pos += 2
    if sgu:
        sg_ref, sw_ref, sb_ref = refs[pos:pos + 3]
        pos += 3
    out_refs = refs[pos:pos + n_out]
    x = x_ref[...]
    ms = jnp.mean(x * x, axis=-1, keepdims=True)
    y = x * lax.rsqrt(ms + NORM_EPS) * g_ref[...]
    h = (y * (1.0 + sc_ref[...]) + sh_ref[...]).astype(MXU_DTYPE)
    for (c0, width, kind, oi) in segs:
        if kind == "sgu":
            u = _gelu(_dot(h, w_ref[:, c0:c0 + BRANCH_DIM]))
            v = _gelu(_dot(h, w_ref[:, c0 + BRANCH_DIM:c0 + 2 * BRANCH_DIM]))
            _spatial_gating(u, v, sg_ref, sw_ref, sb_ref, out_refs[oi])
            continue
        step = min(width, 512)
        for cc in range(0, width, step):
            acc = _dot(h, w_ref[:, c0 + cc:c0 + cc + step])
            if kind == "rope" and rope:
                for k, piece in enumerate(_rope(acc, cos_ref[...], sin_ref[...])):
                    out_refs[oi][:, cc + k * LANES:cc + (k + 1) * LANES] = piece.astype(ACT_DTYPE)
            else:
                out_refs[oi][:, cc:cc + step] = acc.astype(ACT_DTYPE)


def _in_proj(x, mod, mod_row, norm_g, w, layer, segs, rope_tabs, sgu, name):
    bx, l, d = x.shape
    tm = min(IN_PROJ_TM, l)
    assert all(s[0] + s[1] <= OFF_GATE for s in segs)
    n_out = max(s[3] for s in segs) + 1
    out_w = lambda s: BRANCH_DIM if s[2] == "sgu" else s[1]
    widths = [sum(out_w(s) for s in segs if s[3] == oi) for oi in range(n_out)]
    rope = rope_tabs is not None
    use_sgu = any(s[2] == "sgu" for s in segs)
    in_specs = [
        pl.BlockSpec((None, tm, d), lambda b, i: (b, i, 0)),
        _mod_spec(lambda b, i: mod_row(b), 0),
        _mod_spec(lambda b, i: mod_row(b), 1),
        _layer_spec(norm_g, layer),
        pl.BlockSpec((None, d, OFF_GATE), lambda *_: (layer, 0, 0), pipeline_mode=pl.Buffered(1)),
    ]
    args = [x, mod, mod, norm_g, w]
    if rope:
        in_specs += [pl.BlockSpec((tm, LANES), lambda b, i: (i, 0))] * 2
        args += list(rope_tabs)
    if use_sgu:
        in_specs += [_layer_spec(p, layer) for p in sgu]
        args += list(sgu)
    return pl.pallas_call(
        functools.partial(_in_proj_body, segs=segs, rope=rope, sgu=use_sgu, n_out=n_out),
        grid=(bx, l // tm),
        in_specs=in_specs,
        out_specs=[pl.BlockSpec((None, tm, wd), lambda b, i: (b, i, 0)) for wd in widths],
        out_shape=[jax.ShapeDtypeStruct((bx, l, wd), ACT_DTYPE) for wd in widths],
        compiler_params=_cparams(("arbitrary", "arbitrary")),
        name=name,
    )(*args)


IN_PROJ_TM = 1024
FULL_SEGS = (
    (OFF_SGU, 2 * BRANCH_DIM, "sgu", 0),
    (OFF_QKV, BRANCH_DIM, "rope", 1),
    (OFF_QKV + BRANCH_DIM, BRANCH_DIM, "rope", 2),
    (OFF_QKV + 2 * BRANCH_DIM, BRANCH_DIM, "plain", 3),
    (OFF_HY, 3 * HY_DIM, "plain", 4),
)
KV_SEGS = ((OFF_QKV + BRANCH_DIM, BRANCH_DIM, "plain", 0), (OFF_QKV + 2 * BRANCH_DIM, BRANCH_DIM, "plain", 1))


def _att_plan(rows):
    kr = min(NA_WIN_ROWS, rows)
    assert rows % Q_ROWS_PER_BLOCK == 0 and rows >= K_ROWS_PER_BLOCK and kr == NA_WIN_ROWS
    patterns, type_of, key_start = [], [], []
    for rb in range(0, rows, Q_ROWS_PER_BLOCK):
        ks = min(max(rb - NA_WIN_ROWS // 2, 0), rows - K_ROWS_PER_BLOCK)
        assert ks % Q_ROWS_PER_BLOCK == 0
        pat = []
        for qi in range(Q_ROWS_PER_BLOCK):
            rq = rb + qi
            r0 = min(max(rq - kr // 2, 0), rows - kr)
            for j in range(K_ROWS_PER_BLOCK):
                rk = ks + j
                pat.append(rk - rq + NA_WIN_ROWS - 1 if r0 <= rk < r0 + kr else None)
        pat = tuple(pat)
        if pat not in patterns:
            patterns.append(pat)
        type_of.append(patterns.index(pat))
        key_start.append(ks // Q_ROWS_PER_BLOCK)
    return patterns, type_of, key_start


def _bias_body(rpb_ref, o_ref, *, patterns):
    n_dr, n_dc = 2 * NA_WIN_ROWS - 1, 2 * NA_WIN_COLS - 1
    h = pl.program_id(0)
    qc = lax.broadcasted_iota(jnp.int32, (GRID_W, GRID_W), 0)
    kc = lax.broadcasted_iota(jnp.int32, (GRID_W, GRID_W), 1)
    cstart = jnp.clip(qc - NA_WIN_COLS // 2, 0, GRID_W - NA_WIN_COLS)
    col_ok = (kc >= cstart) & (kc < cstart + NA_WIN_COLS)
    dc = jnp.clip(kc - qc + NA_WIN_COLS - 1, 0, n_dc - 1)
    neg = jnp.full((GRID_W, GRID_W), NEG_INF, F32)
    tiles = []
    for dr in range(n_dr):
        t = jnp.zeros((GRID_W, GRID_W), F32)
        for d in range(n_dc):
            t = jnp.where(dc == d, rpb_ref[(h * n_dr + dr) * n_dc + d], t)
        tiles.append(jnp.where(col_ok, t * LOG2E, neg))
    for ty, pat in enumerate(patterns):
        for qi in range(Q_ROWS_PER_BLOCK):
            for j in range(K_ROWS_PER_BLOCK):
                dr = pat[qi * K_ROWS_PER_BLOCK + j]
                o_ref[ty, qi * GRID_W:(qi + 1) * GRID_W, j * GRID_W:(j + 1) * GRID_W] = neg if dr is None else tiles[dr]


def _bias_tables(rpb, patterns):
    nt = len(patterns)
    return pl.pallas_call(
        functools.partial(_bias_body, patterns=patterns),
        grid=(NA_HEADS,),
        in_specs=[pl.BlockSpec(memory_space=pltpu.SMEM)],
        out_specs=pl.BlockSpec((nt, None, ATT_TOK, K_ROWS_PER_BLOCK * GRID_W), lambda h: (0, h, 0, 0)),
        out_shape=jax.ShapeDtypeStruct((nt, NA_HEADS, ATT_TOK, K_ROWS_PER_BLOCK * GRID_W), F32),
        compiler_params=_cparams(("arbitrary",)),
        name="attention_bias_tables",
    )(rpb.reshape(-1))


def _head_mask(shape, hh):
    lane = lax.broadcasted_iota(jnp.int32, shape, 1)
    return (lane >= hh * NA_HEAD_DIM) & (lane < (hh + 1) * NA_HEAD_DIM)


LOG2E = math.log2(math.e)
QK_SCALE = NA_HEAD_DIM ** -0.5 * LOG2E


def _pair_attention(qp, keys, biases, values):
    m_rows = qp.shape[0]
    q2 = jnp.concatenate([jnp.where(_head_mask(qp.shape, hh), qp, jnp.zeros_like(qp)) for hh in range(2)], axis=0)
    s_all = _dot_nt(q2, jnp.concatenate(keys, axis=0))
    scores, c0 = [], 0
    for j, kj in enumerate(keys):
        s = s_all[:, c0:c0 + kj.shape[0]]
        c0 += kj.shape[0]
        if biases[0][j] is not None:
            s = s + jnp.concatenate([biases[0][j], biases[1][j]], axis=0)
        scores.append(s)
    m = functools.reduce(jnp.maximum, [jnp.max(s, axis=-1, keepdims=True) for s in scores])
    e = jnp.concatenate([jnp.exp2(s - m).astype(MXU_DTYPE) for s in scores], axis=1)
    vcat = jnp.concatenate(values, axis=0)
    acc = _dot(e, jnp.concatenate([vcat, jnp.ones_like(vcat)], axis=1))
    out = acc[:, :LANES] / acc[:, LANES:]
    return jnp.where(_head_mask((m_rows, LANES), 0), out[:m_rows], out[m_rows:])


ATT_BATCH = 8


def _attn_body(ty_ref, kb_ref, q_ref, k0, k1, k2, v0, v1, v2, kc_ref, vc_ref, bias_ref, o_ref):
    del ty_ref, kb_ref
    for p in range(NA_HEADS // 2):
        cs = slice(p * LANES, (p + 1) * LANES)
        biases = [[bias_ref[2 * p + hh, :, j * ATT_TOK:(j + 1) * ATT_TOK] for j in range(3)] + [None]
                  for hh in range(2)]
        for e in range(q_ref.shape[0]):
            qp = q_ref[e, :, cs] * QK_SCALE
            keys = [k0[e, :, cs], k1[e, :, cs], k2[e, :, cs], kc_ref[e, :, cs]]
            values = [v0[e, :, cs], v1[e, :, cs], v2[e, :, cs], vc_ref[e, :, cs]]
            o_ref[e, :, cs] = _pair_attention(qp, keys, biases, values).astype(ACT_DTYPE)


def _neighborhood_attention(q, k, v, kc, vc, bias, type_of, key_start):
    b, l, _ = q.shape
    lc = kc.shape[1]
    n_blk = l // ATT_TOK
    nb = math.gcd(b, ATT_BATCH)
    tok = lambda off: pl.BlockSpec((nb, ATT_TOK, BRANCH_DIM), lambda r, bb, ty, kb: (bb, kb[r] + off, 0))
    ctx = pl.BlockSpec((nb, lc, BRANCH_DIM), lambda r, bb, ty, kb: (bb, 0, 0))
    grid_spec = pltpu.PrefetchScalarGridSpec(
        num_scalar_prefetch=2,
        grid=(n_blk, b // nb),
        in_specs=[
            pl.BlockSpec((nb, ATT_TOK, BRANCH_DIM), lambda r, bb, ty, kb: (bb, r, 0)),
            tok(0), tok(1), tok(2), tok(0), tok(1), tok(2), ctx, ctx,
            pl.BlockSpec((None, NA_HEADS, ATT_TOK, K_ROWS_PER_BLOCK * GRID_W), lambda r, bb, ty, kb: (ty[r], 0, 0, 0)),
        ],
        out_specs=pl.BlockSpec((nb, ATT_TOK, BRANCH_DIM), lambda r, bb, ty, kb: (bb, r, 0)),
    )
    return pl.pallas_call(
        _attn_body,
        grid_spec=grid_spec,
        out_shape=jax.ShapeDtypeStruct((b, l, BRANCH_DIM), ACT_DTYPE),
        compiler_params=_cparams(("arbitrary", "arbitrary")),
        name="neighborhood_attention",
    )(jnp.asarray(type_of, jnp.int32), jnp.asarray(key_start, jnp.int32), q, k, k, k, v, v, v, kc, vc, bias)


def _ctx_attn_body(q_ref, k_ref, v_ref, o_ref):
    for p in range(NA_HEADS // 2):
        cs = slice(p * LANES, (p + 1) * LANES)
        qp = q_ref[:, cs] * QK_SCALE
        o_ref[:, cs] = _pair_attention(qp, [k_ref[:, cs]], [[None], [None]], [v_ref[:, cs]]).astype(ACT_DTYPE)


def _context_attention(q, k, v):
    b, lc, _ = q.shape
    spec = pl.BlockSpec((None, lc, BRANCH_DIM), lambda bb: (bb, 0, 0))
    return pl.pallas_call(
        _ctx_attn_body,
        grid=(b,),
        in_specs=[spec, spec, spec],
        out_specs=spec,
        out_shape=jax.ShapeDtypeStruct((b, lc, BRANCH_DIM), ACT_DTYPE),
        compiler_params=_cparams(("arbitrary",)),
        name="context_attention",
    )(q, k, v)


HY_CW = 256
HY_TM = 512
HY_BLOCKS = 4
HY_MIN_BLOCK = 256
CONV_HALO = 16


def _hy_blocks(l):
    return HY_BLOCKS if l % (HY_BLOCKS * HY_MIN_BLOCK) == 0 else 1


@functools.lru_cache(maxsize=None)
def _dft_consts_np(m):
    n = 2 * m
    idx = np.arange(m, dtype=np.int64)
    ang = ((idx[:, None] * idx[None, :]) % n).astype(np.float64) * (2.0 * math.pi / n)
    cs = np.concatenate([np.cos(ang), -np.sin(ang)], axis=1).astype(np.float32)
    sign = np.where(idx % 2 == 0, 1.0, -1.0).astype(np.float32).reshape(m, 1)
    wf = np.where(idx == 0, 1.0 / n, 2.0 / n).astype(np.float32).reshape(m, 1)
    return cs, sign, wf


def _dft_consts(m):
    cs, sign, wf = _dft_consts_np(m)
    return jnp.asarray(cs.astype(MXU_DTYPE)), jnp.asarray(sign), jnp.asarray(wf)


def _filter_consts(l):
    t = np.linspace(0.0, 1.0, l, dtype=np.float32)[:, None]
    n_bands = (HY_EMB - 1) // 2
    bands = np.linspace(1e-4, n_bands - 1, n_bands, dtype=np.float32)[None, :]
    ang = np.float32(2.0 * math.pi) * np.arange(l, dtype=np.float32)[:, None] / np.float32(l) * bands
    z = np.concatenate([t, np.cos(ang), -np.sin(ang)], axis=-1).astype(np.float32)
    z = np.pad(z, ((0, 0), (0, LANES - HY_EMB)))
    max_decay = math.log(HY_DECAY_TARGET) / HY_FAST_DECAY_PCT
    min_decay = math.log(HY_DECAY_TARGET) / HY_SLOW_DECAY_PCT
    deltas = np.abs(np.linspace(min_decay, max_decay, HY_DIM, dtype=np.float32)).reshape(1, HY_DIM)
    return jnp.asarray(z), jnp.asarray(t), jnp.asarray(deltas)


def _filter_body(z_ref, w1_ref, b1_ref, w2_ref, b2_ref, fr_ref, w3a_ref, w3b_ref, t_ref, dl_ref,
                 sg_ref, wf_ref, cs_ref, skip_ref, gre_ref, gim_ref, gny_ref, h_ref, *, p):
    l = z_ref.shape[0]
    m = l // p

    @pl.when((pl.program_id(0) == 0) & (pl.program_id(1) == 0))
    def _():
        fr = fr_ref[...]
        h1 = jnp.sin(fr * (_dot(z_ref[...], w1_ref[...], precision=HIGHEST) + b1_ref[...]))
        h_ref[...] = jnp.sin(fr * (_dot(h1, w2_ref[...], precision=HIGHEST) + b2_ref[...]))

    h = h_ref[...]
    decay = jnp.exp(-t_ref[...] * dl_ref[...])
    h0 = _dot(h, w3a_ref[...], precision=HIGHEST) * decay
    h1 = _dot(h, w3b_ref[...], precision=HIGHEST) * decay
    row = lax.broadcasted_iota(jnp.int32, h1.shape, 0)
    den = (jnp.sum(jnp.abs(h0), axis=0, keepdims=True)
           + jnp.sum(jnp.where(row == 0, 0.0, jnp.abs(h1)), axis=0, keepdims=True))
    h0, h1 = h0 / den, h1 / den
    sg, wf = sg_ref[...], wf_ref[...]
    u, v, pc, q, un, pn, r0, r1 = [], [], [], [], [], [], [], []
    for b in range(p):
        x0, x1 = h0[b * m:(b + 1) * m], h1[b * m:(b + 1) * m]
        x0b, x1b = x0.astype(MXU_DTYPE), x1.astype(MXU_DTYPE)
        u.append(_dot(cs_ref[:, :m], x0b))
        v.append(_dot(cs_ref[:, m:], x0b))
        pc.append(_dot(cs_ref[:, :m], x1b))
        q.append(_dot(cs_ref[:, m:], x1b))
        un.append(jnp.sum(sg * x0, axis=0, keepdims=True))
        pn.append(jnp.sum(sg * x1, axis=0, keepdims=True))
        r0.append(x0[0:1])
        r1.append(x1[0:1])
    for li, d in enumerate(range(-(p - 1), p)):
        if d > 0:
            gre = u[d] + sg * (u[d - 1] - r0[d - 1])
            gim = v[d] + sg * v[d - 1]
            gny = un[d] + un[d - 1] - r0[d - 1]
        elif d == 0:
            gre = u[0] + pc[0] - r1[0] + skip_ref[...]
            gim = v[0] - q[0]
            gny = un[0] + pn[0] - r1[0] + skip_ref[...]
        else:
            gre = pc[-d] + sg * (pc[-d - 1] - r1[-d - 1])
            gim = -q[-d] - sg * q[-d - 1]
            gny = pn[-d] + pn[-d - 1] - r1[-d - 1]
        gre_ref[li] = (wf * gre).astype(gre_ref.dtype)
        gim_ref[li] = (wf * gim).astype(gim_ref.dtype)
        gny_ref[li] = gny * (1.0 / (2 * m))


def _hyena_filters(l, dft, fconsts, skip, w1, b1, w2, b2, w3, freq):
    cs, sign, wf = dft
    z, t, deltas = fconsts
    p = _hy_blocks(l)
    m, nl = l // p, 2 * p - 1
    cw = HY_CW
    nb = HY_DIM // cw
    w1p = jnp.pad(w1, ((0, LANES - HY_EMB), (0, 0)))
    hid = HY_FILTER_HIDDEN
    small = lambda shape: pl.BlockSpec(shape, lambda n, cb: (0,) * len(shape))
    spec_out = pl.BlockSpec((None, nl, m, cw), lambda n, cb: (n, 0, 0, cb))
    return pl.pallas_call(
        functools.partial(_filter_body, p=p),
        grid=(HY_ORDER, nb),
        in_specs=[
            small((l, LANES)), small((LANES, hid)), small((1, hid)), small((hid, hid)), small((1, hid)), small((1, hid)),
            pl.BlockSpec((hid, cw), lambda n, cb: (0, n * 2 * nb + cb)),
            pl.BlockSpec((hid, cw), lambda n, cb: (0, n * 2 * nb + nb + cb)),
            small((l, 1)),
            pl.BlockSpec((1, cw), lambda n, cb: (0, cb)),
            small((m, 1)), small((m, 1)),
            _const_spec((m, 2 * m)),
            pl.BlockSpec((None, 1, cw), lambda n, cb: (n, 0, cb)),
        ],
        out_specs=[spec_out, spec_out, pl.BlockSpec((None, nl, 1, cw), lambda n, cb: (n, 0, 0, cb))],
        out_shape=[jax.ShapeDtypeStruct((HY_ORDER, nl, m, HY_DIM), ACT_DTYPE)] * 2
        + [jax.ShapeDtypeStruct((HY_ORDER, nl, 1, HY_DIM), F32)],
        scratch_shapes=[pltpu.VMEM((l, hid), F32)],
        compiler_params=_cparams(("arbitrary", "arbitrary")),
        name="hyena_filter_spectrum",
    )(z, w1p, b1.reshape(1, hid), w2, b2.reshape(1, hid), freq.reshape(1, hid), w3, w3, t, deltas, sign, wf, cs,
      skip.reshape(HY_ORDER, 1, HY_DIM))


def _conv3_rows(p_ref, w_ref, b_ref, r0, rows):
    l = p_ref.shape[0]
    lo, hi = max(r0 - CONV_HALO, 0), min(r0 + rows + CONV_HALO, l)
    x = p_ref[lo:hi, :].astype(F32)
    n = hi - lo
    prev, nxt = pltpu.roll(x, 1, 0), pltpu.roll(x, n - 1, 0)
    row = lax.broadcasted_iota(jnp.int32, x.shape, 0)
    if lo == 0:
        prev = jnp.where(row == 0, 0.0, prev)
    if hi == l:
        nxt = jnp.where(row == n - 1, 0.0, nxt)
    y = prev * w_ref[0:1, :] + x * w_ref[1:2, :] + nxt * w_ref[2:3, :] + b_ref[...]
    return y[r0 - lo:r0 - lo + rows]


def _hyena_body(pv_ref, p1_ref, p2_ref, wv_ref, w1_ref, w2_ref, bv_ref, b1_ref, b2_ref,
                gre_ref, gim_ref, gny_ref, sg_ref, cs_ref, o_ref, zb_ref, zz_ref, y_ref, *, p):
    l, cw = pv_ref.shape
    m = l // p
    tm = min(HY_TM, m)
    offs = range(0, m, tm)
    zn = [jnp.zeros((1, cw), F32) for _ in range(p)]

    def put_z(a, c0, z):
        rs = slice(a * m + c0, a * m + c0 + tm)
        zb_ref[rs] = z.astype(MXU_DTYPE)
        zn[a] = zn[a] + jnp.sum(sg_ref[c0:c0 + tm] * z, axis=0, keepdims=True)

    for a in range(p):
        for c0 in offs:
            put_z(a, c0, _conv3_rows(pv_ref, wv_ref, bv_ref, a * m + c0, tm))
    gates = ((p1_ref, w1_ref, b1_ref), (p2_ref, w2_ref, b2_ref))
    for n, (p_ref, w_ref, b_ref) in enumerate(gates):
        for a in range(p):
            zblk = zb_ref[a * m:(a + 1) * m]
            for c0 in offs:
                rs = slice(a * m + c0, a * m + c0 + tm)
                zz_ref[0, rs] = _dot(cs_ref[c0:c0 + tm, :m], zblk).astype(zz_ref.dtype)
                zz_ref[1, rs] = _dot(cs_ref[c0:c0 + tm, m:], zblk).astype(zz_ref.dtype)
        for a2 in range(p):
            for c0 in offs:
                yc = ys = None
                for a in range(p):
                    li = a2 - a + p - 1
                    gre, gim = gre_ref[n, li, c0:c0 + tm], gim_ref[n, li, c0:c0 + tm]
                    zc, zs = zz_ref[0, a * m + c0:a * m + c0 + tm], zz_ref[1, a * m + c0:a * m + c0 + tm]
                    tc, ts = zc * gre - zs * gim, zc * gim + zs * gre
                    yc, ys = (tc, ts) if yc is None else (yc + tc, ys + ts)
                y_ref[a2, c0:c0 + tm] = yc.astype(MXU_DTYPE)
                y_ref[a2, m + c0:m + c0 + tm] = ys.astype(MXU_DTYPE)
        nyq = [functools.reduce(lambda s, t: s + t, [zn[a] * gny_ref[n, a2 - a + p - 1] for a in range(p)])
               for a2 in range(p)]
        zn = [jnp.zeros((1, cw), F32) for _ in range(p)]
        for a2 in range(p):
            for c0 in offs:
                rs = slice(a2 * m + c0, a2 * m + c0 + tm)
                y = _dot(cs_ref[c0:c0 + tm, :], y_ref[a2]) + sg_ref[c0:c0 + tm] * nyq[a2]
                gate = _conv3_rows(p_ref, w_ref, b_ref, rs.start, tm)
                z = gate * y
                if n + 1 < HY_ORDER:
                    put_z(a2, c0, z)
                else:
                    o_ref[rs] = z.astype(ACT_DTYPE)


def _hyena(x, conv_w, conv_b, filters, dft):
    bx, l, _ = x.shape
    cs, sign, _ = dft
    gre, gim, gny = filters
    p = _hy_blocks(l)
    m, nl = l // p, 2 * p - 1
    cw = HY_CW
    nb = HY_DIM // cw
    pspec = lambda part: pl.BlockSpec((None, l, cw), lambda cb, b: (b, 0, part * nb + cb))
    wspec = lambda part: pl.BlockSpec((3, cw), lambda cb, b: (0, part * nb + cb))
    bspec = lambda part: pl.BlockSpec((1, cw), lambda cb, b: (0, part * nb + cb))
    fspec = pl.BlockSpec((HY_ORDER, nl, m, cw), lambda cb, b: (0, 0, 0, cb), pipeline_mode=pl.Buffered(1))
    return pl.pallas_call(
        functools.partial(_hyena_body, p=p),
        grid=(nb, bx),
        in_specs=[
            pspec(0), pspec(1), pspec(2), wspec(0), wspec(1), wspec(2), bspec(0), bspec(1), bspec(2),
            fspec, fspec,
            pl.BlockSpec((HY_ORDER, nl, 1, cw), lambda cb, b: (0, 0, 0, cb)),
            _const_spec((m, 1)), _const_spec((m, 2 * m)),
        ],
        out_specs=pl.BlockSpec((None, l, cw), lambda cb, b: (b, 0, cb)),
        out_shape=jax.ShapeDtypeStruct((bx, l, HY_DIM), ACT_DTYPE),
        scratch_shapes=[pltpu.VMEM((l, cw), MXU_DTYPE), pltpu.VMEM((2, l, cw), ACT_DTYPE),
                        pltpu.VMEM((p, 2 * m, cw), MXU_DTYPE)],
        compiler_params=_cparams(("arbitrary", "arbitrary")),
        name="hyena_long_conv",
    )(x, x, x, conv_w, conv_w, conv_w, conv_b.reshape(1, -1), conv_b.reshape(1, -1), conv_b.reshape(1, -1),
      gre, gim, gny, sign, cs)


MERGE_TM = 1024


def _merge_body(x_ref, a_ref, at_ref, hy_ref, sh_ref, sc_ref, g1_ref, ng_ref, wg0_ref, wg1_ref, wg2_ref,
                wb_ref, wo_ref, o_ref):
    x = x_ref[...]
    ms = jnp.mean(x * x, axis=-1, keepdims=True)
    y = x * lax.rsqrt(ms + NORM_EPS) * ng_ref[...]
    h = (y * (1.0 + sc_ref[...]) + sh_ref[...]).astype(MXU_DTYPE)
    m = None
    for j, (br_ref, wg_ref) in enumerate(((a_ref, wg0_ref), (at_ref, wg1_ref), (hy_ref, wg2_ref))):
        term = jax.nn.sigmoid(_dot(h, wg_ref[...])) * _dot(br_ref[...], wb_ref[j])
        m = term if m is None else m + term
    o_ref[...] = x + g1_ref[...] * _dot(m.astype(MXU_DTYPE), wo_ref[...])


def _merge(x, a, attn, hy, mod, mod_row, norm_g, w_in, wb, wo, layer):
    bx, l, d = x.shape
    tm = min(MERGE_TM, l)
    tile = lambda w: pl.BlockSpec((None, tm, w), lambda b, i: (b, i, 0))
    gate_w = lambda j: pl.BlockSpec((None, d, d), lambda *_: (layer, 0, OFF_GATE // d + j), pipeline_mode=pl.Buffered(1))
    return pl.pallas_call(
        _merge_body,
        grid=(bx, l // tm),
        in_specs=[tile(d), tile(BRANCH_DIM), tile(BRANCH_DIM), tile(BRANCH_DIM),
                  _mod_spec(lambda b, i: mod_row(b), 0), _mod_spec(lambda b, i: mod_row(b), 1),
                  _mod_spec(lambda b, i: mod_row(b), 2), _layer_spec(norm_g, layer),
                  gate_w(0), gate_w(1), gate_w(2), _layer_spec(wb, layer), _layer_spec(wo, layer)],
        out_specs=tile(d),
        out_shape=jax.ShapeDtypeStruct((bx, l, d), F32),
        compiler_params=_cparams(("arbitrary", "arbitrary")),
        name="branch_merge",
    )(x, a, attn, hy, mod, mod, mod, norm_g, w_in, w_in, w_in, wb, wo)


FFN_TM = 1024
FFN_TN = 256


def _ffn_body(xp_ref, x_ref, xn_ref, sh_ref, sc_ref, g2_ref, ng_ref, wu_ref, cw_ref, cb_ref, wd_ref, fg_ref,
              o_ref, h_ref, gt_ref, *, final, seq_len):
    tm = x_ref.shape[0]
    halo = CONV_HALO
    n_ext = tm + 2 * halo

    def norm_mod(x):
        ms = jnp.mean(x * x, axis=-1, keepdims=True)
        y = x * lax.rsqrt(ms + NORM_EPS) * ng_ref[...]
        return (y * (1.0 + sc_ref[...]) + sh_ref[...]).astype(MXU_DTYPE)

    h_ref[0:halo] = norm_mod(xp_ref[...])
    h_ref[halo:halo + tm] = norm_mod(x_ref[...])
    h_ref[halo + tm:n_ext] = norm_mod(xn_ref[...])
    row = lax.broadcasted_iota(jnp.int32, (n_ext, 1), 0)
    pos = (pl.program_id(1) * tm + row + (seq_len - halo)) % seq_len
    first, last = pos == 0, pos == seq_len - 1
    for j in range(D_FF // FFN_TN):
        cs = slice(j * FFN_TN, (j + 1) * FFN_TN)
        a = _dot(h_ref[...], wu_ref[:, cs])
        prev = jnp.where(first, 0.0, pltpu.roll(a, 1, 0))
        nxt = jnp.where(last, 0.0, pltpu.roll(a, n_ext - 1, 0))
        a = prev * cw_ref[0:1, cs] + a * cw_ref[1:2, cs] + nxt * cw_ref[2:3, cs] + cb_ref[:, cs]
        gate = _dot(h_ref[halo:halo + tm], wu_ref[:, D_FF + j * FFN_TN:D_FF + (j + 1) * FFN_TN])
        gt_ref[:, cs] = (_gelu(a[halo:halo + tm]) * gate).astype(MXU_DTYPE)
    xn = x_ref[...] + g2_ref[...] * _dot(gt_ref[...], wd_ref[...])
    if final:
        ms = jnp.mean(xn * xn, axis=-1, keepdims=True)
        xn = xn * lax.rsqrt(ms + NORM_EPS) * fg_ref[...]
    o_ref[...] = xn


def _ffn(x, mod, mod_row, norm_g, w_up, conv_w, conv_b, w_down, layer, final_g, final, seq_len):
    bx, l, d = x.shape
    tm = min(FFN_TM, l)
    halo = CONV_HALO
    per, last_blk = tm // halo, l // halo - 1
    tile = pl.BlockSpec((None, tm, d), lambda b, i: (b, i, 0))
    return pl.pallas_call(
        functools.partial(_ffn_body, final=final, seq_len=seq_len),
        grid=(bx, l // tm),
        in_specs=[
            pl.BlockSpec((None, halo, d), lambda b, i: (b, jnp.maximum(i * per - 1, 0), 0)),
            tile,
            pl.BlockSpec((None, halo, d), lambda b, i: (b, jnp.minimum((i + 1) * per, last_blk), 0)),
            _mod_spec(lambda b, i: mod_row(b), 3), _mod_spec(lambda b, i: mod_row(b), 4),
            _mod_spec(lambda b, i: mod_row(b), 5),
            _layer_spec(norm_g, layer), _layer_spec(w_up, layer), _layer_spec(conv_w, layer),
            _layer_spec(conv_b, layer), _layer_spec(w_down, layer), _const_spec((1, d)),
        ],
        out_specs=tile,
        out_shape=jax.ShapeDtypeStruct((bx, l, d), F32),
        scratch_shapes=[pltpu.VMEM((tm + 2 * halo, d), MXU_DTYPE), pltpu.VMEM((tm, D_FF), MXU_DTYPE)],
        compiler_params=_cparams(("arbitrary", "arbitrary")),
        name="conv_ffn",
    )(x, x, x, mod, mod, mod, norm_g, w_up, conv_w, conv_b, w_down, final_g.reshape(1, d))


def _rope_tables(l):
    half = NA_HEAD_DIM // 2
    quarter = half // 2
    inv_freq = np.float32(ROPE_THETA) ** (-np.arange(quarter, dtype=np.float32) * np.float32(2.0) / np.float32(half))
    pos = np.arange(l)
    ang_r = (pos // GRID_W).astype(np.float32)[:, None] * inv_freq[None, :]
    ang_c = (pos % GRID_W).astype(np.float32)[:, None] * inv_freq[None, :]
    cos_h = np.concatenate([np.cos(ang_r), np.cos(ang_r), np.cos(ang_c), np.cos(ang_c)], axis=-1)
    sin_h = np.concatenate([-np.sin(ang_r), np.sin(ang_r), -np.sin(ang_c), np.sin(ang_c)], axis=-1)
    reps = LANES // NA_HEAD_DIM
    return (jnp.asarray(np.tile(cos_h, (1, reps)).astype(np.float32)),
            jnp.asarray(np.tile(sin_h, (1, reps)).astype(np.float32)))


def kernel(x, c, ctx, c_ctx, ada_w, ada_b, norm1_g, norm2_g, w_in, sgu_norm_g, sgu_w, sgu_b, na_rpb, hy_conv_w, hy_conv_b, hy_filt_w1, hy_filt_b1, hy_filt_w2, hy_filt_b2, hy_filt_w3, hy_sin_freq, hy_skip, w_branch, w_out, ffn_w_up, ffn_conv_w, ffn_conv_b, ffn_w_down, final_norm_g):
    b, l, d = x.shape
    lc = ctx.shape[1]
    assert d == D_MODEL and l % ATT_TOK == 0 and l % GRID_W == 0

    n_rows = -(-(b + 1) // 8) * 8
    cc = jnp.concatenate([c, c_ctx[None, :], jnp.zeros((n_rows - b - 1, d), F32)], axis=0)
    mod = _modulation(cc, ada_w, ada_b).reshape(DEPTH * n_rows, 1, N_MOD * d)

    rope_tabs = _rope_tables(l)
    dft_l, fc_l = _dft_consts(l // _hy_blocks(l)), _filter_consts(l)
    dft_c, fc_c = _dft_consts(lc // _hy_blocks(lc)), _filter_consts(lc)
    patterns, type_of, key_start = _att_plan(l // GRID_W)

    w_in_b, wb_b, wo_b = w_in.astype(MXU_DTYPE), w_branch.astype(MXU_DTYPE), w_out.astype(MXU_DTYPE)
    wup_b, wdn_b = ffn_w_up.astype(MXU_DTYPE), ffn_w_down.astype(MXU_DTYPE)
    n1g, n2g = norm1_g.reshape(DEPTH, 1, d), norm2_g.reshape(DEPTH, 1, d)
    ffn_cb = ffn_conv_b.reshape(DEPTH, 1, D_FF)
    sgu = (sgu_norm_g.reshape(DEPTH, 1, BRANCH_DIM), sgu_w.astype(MXU_DTYPE),
           sgu_b.reshape(DEPTH, SGU_GROUPS, SGU_CHUNK, 1))

    grp = math.gcd(b, max(1, l // lc))
    stack = lambda t: t.reshape(b // grp, grp * lc, t.shape[-1])
    unstack = lambda t: t.reshape(b, lc, t.shape[-1])

    xc = ctx
    for i in range(DEPTH):
        update_ctx = i < DEPTH - 1
        lat_row = lambda bb, i=i: i * n_rows + bb
        ctx_row = lambda bb, i=i: i * n_rows + b
        filt_args = (hy_skip[i], hy_filt_w1[i], hy_filt_b1[i], hy_filt_w2[i], hy_filt_b2[i], hy_filt_w3[i], hy_sin_freq[i])
        bias = _bias_tables(na_rpb[i], patterns)

        if update_ctx:
            a_c, q_c, k_c, v_c, hy_c = map(unstack, _in_proj(stack(xc), mod, ctx_row, n1g, w_in_b, i, FULL_SEGS, None,
                                                             sgu, "context_in_proj"))
        else:
            k_c, v_c = map(unstack, _in_proj(stack(xc), mod, ctx_row, n1g, w_in_b, i, KV_SEGS, None, None,
                                             "context_kv_proj"))

        a, q, k, v, hy = _in_proj(x, mod, lat_row, n1g, w_in_b, i, FULL_SEGS, rope_tabs, sgu, "latent_in_proj")
        attn = _neighborhood_attention(q, k, v, k_c, v_c, bias, type_of, key_start)
        filt = _hyena_filters(l, dft_l, fc_l, *filt_args)
        hyo = _hyena(hy, hy_conv_w[i], hy_conv_b[i], filt, dft_l)
        x = _merge(x, a, attn, hyo, mod, lat_row, n1g, w_in_b, wb_b, wo_b, i)
        x = _ffn(x, mod, lat_row, n2g, wup_b, ffn_conv_w, ffn_cb, wdn_b, i, final_norm_g,
                 final=not update_ctx, seq_len=l)

        if update_ctx:
            attn_c = _context_attention(q_c, k_c, v_c)
            filt_c = _hyena_filters(lc, dft_c, fc_c, *filt_args)
            hyo_c = _hyena(hy_c, hy_conv_w[i], hy_conv_b[i], filt_c, dft_c)
            xs = _merge(stack(xc), stack(a_c), stack(attn_c), stack(hyo_c), mod, ctx_row, n1g, w_in_b, wb_b, wo_b, i)
            xc = unstack(_ffn(xs, mod, ctx_row, n2g, wup_b, ffn_conv_w, ffn_cb, wdn_b, i, final_norm_g,
                              final=False, seq_len=lc))
    return x
```

```python
import functools
import math

import jax
import jax.numpy as jnp
import numpy as np
from jax import lax
from jax.experimental import pallas as pl
from jax.experimental.pallas import tpu as pltpu

D_MODEL = 1024
DEPTH = 2
GRID_W = 64
NORM_EPS = 1e-6
N_MOD = 6
BRANCH_DIM = D_MODEL // 2
SGU_GROUP_DIM = 128
SGU_GROUPS = BRANCH_DIM // SGU_GROUP_DIM
SGU_CHUNK = 128
NA_HEAD_DIM = 64
NA_HEADS = BRANCH_DIM // NA_HEAD_DIM
NA_WIN_ROWS = 8
NA_WIN_COLS = 16
ROPE_THETA = 10000.0
NEG_INF = -1e30
HY_DIM = BRANCH_DIM
HY_ORDER = 2
HY_EMB = 33
HY_FILTER_HIDDEN = 64
HY_FAST_DECAY_PCT = 0.3
HY_SLOW_DECAY_PCT = 1.5
HY_DECAY_TARGET = 1e-2
D_FF = 128 * ((8 * D_MODEL + 3 * 128 - 1) // (3 * 128))
OFF_SGU = 0
OFF_QKV = OFF_SGU + 2 * BRANCH_DIM
OFF_HY = OFF_QKV + 3 * BRANCH_DIM
OFF_GATE = OFF_HY + (HY_ORDER + 1) * HY_DIM
IN_COLS = OFF_GATE + 3 * D_MODEL

LANES = 128
MXU_WIDTH = 256
VMEM_LIMIT_BYTES = 56 * 1024 * 1024

MXU_DTYPE = jnp.bfloat16
ACT_DTYPE = jnp.bfloat16
F32 = jnp.float32
HIGHEST = lax.Precision.HIGHEST

Q_ROWS_PER_BLOCK = 4
K_ROWS_PER_BLOCK = 12
ATT_TOK = Q_ROWS_PER_BLOCK * GRID_W


def _cparams(sem):
    return pltpu.CompilerParams(dimension_semantics=sem, vmem_limit_bytes=VMEM_LIMIT_BYTES)


def _const_spec(shape):
    nd = len(shape)
    return pl.BlockSpec(shape, lambda *_: (0,) * nd, pipeline_mode=pl.Buffered(1))


def _layer_spec(arr, layer):
    nd = arr.ndim - 1
    return pl.BlockSpec((None,) + arr.shape[1:], lambda *_: (layer,) + (0,) * nd, pipeline_mode=pl.Buffered(1))


def _dot(a, b, **kw):
    return jnp.dot(a, b, preferred_element_type=F32, **kw)


def _dot_nt(a, b):
    return lax.dot_general(a, b, (((1,), (1,)), ((), ())), preferred_element_type=F32)


def _gelu(x):
    return 0.5 * x * (1.0 + lax.erf(x * (1.0 / math.sqrt(2.0))))


def _mod_body(c_ref, w_ref, b_ref, o_ref):
    c = c_ref[...]
    s = c * jax.nn.sigmoid(c)
    o_ref[...] = _dot(s, w_ref[...], precision=HIGHEST) + b_ref[...]


def _modulation(cc, ada_w, ada_b):
    depth, d, n = ada_w.shape
    r = cc.shape[0]
    tn = 1024
    return pl.pallas_call(
        _mod_body,
        grid=(depth, n // tn),
        in_specs=[
            pl.BlockSpec((r, d), lambda i, j: (0, 0)),
            pl.BlockSpec((None, d, tn), lambda i, j: (i, 0, j)),
            pl.BlockSpec((None, 1, tn), lambda i, j: (i, 0, j)),
        ],
        out_specs=pl.BlockSpec((None, r, tn), lambda i, j: (i, 0, j)),
        out_shape=jax.ShapeDtypeStruct((depth, r, n), F32),
        compiler_params=_cparams(("arbitrary", "arbitrary")),
        name="adaln_modulation",
    )(cc, ada_w, ada_b.reshape(depth, 1, n))


def _mod_spec(row_fn, chunk):
    return pl.BlockSpec((None, 1, D_MODEL), lambda *g: (row_fn(*g), 0, chunk))


def _rope(acc, cos, sin):
    lane = lax.broadcasted_iota(jnp.int32, cos.shape, 1)
    first_half = (lane % 32) < 16
    outs = []
    for c0 in range(0, acc.shape[1], LANES):
        xc = acc[:, c0:c0 + LANES]
        partner = jnp.where(first_half, pltpu.roll(xc, LANES - 16, 1), pltpu.roll(xc, 16, 1))
        outs.append(xc * cos + partner * sin)
    return outs


def _spatial_gating(u, v, g_ref, w_ref, b_ref, o_ref):
    ms = jnp.mean(v * v, axis=-1, keepdims=True)
    vb = (v * lax.rsqrt(ms + NORM_EPS) * g_ref[...]).astype(MXU_DTYPE)
    for n in range(u.shape[0] // SGU_CHUNK):
        rs = slice(n * SGU_CHUNK, (n + 1) * SGU_CHUNK)
        for g in range(SGU_GROUPS):
            cs = slice(g * SGU_GROUP_DIM, (g + 1) * SGU_GROUP_DIM)
            mixed = _dot(w_ref[g], vb[rs, cs]) + b_ref[g]
            o_ref[rs, cs] = (u[rs, cs] * mixed).astype(ACT_DTYPE)


def _in_proj_body(*refs, segs, rope, sgu, n_out):
    x_ref, sh_ref, sc_ref, g_ref, w_ref = refs[:5]
    pos = 5
    if rope:
        cos_ref, sin_ref = refs[pos:pos + 2]
        pos += 2
    if sgu:
        sg_ref, sw_ref, sb_ref = refs[pos:pos + 3]
        pos += 3
    out_refs = refs[pos:pos + n_out]
    x = x_ref[...]
    ms = jnp.mean(x * x, axis=-1, keepdims=True)
    y = x * lax.rsqrt(ms + NORM_EPS) * g_ref[...]
    h = (y * (1.0 + sc_ref[...]) + sh_ref[...]).astype(MXU_DTYPE)
    for (c0, width, kind, oi) in segs:
        if kind == "sgu":
            u = _gelu(_dot(h, w_ref[:, c0:c0 + BRANCH_DIM]))
            v = _gelu(_dot(h, w_ref[:, c0 + BRANCH_DIM:c0 + 2 * BRANCH_DIM]))
            _spatial_gating(u, v, sg_ref, sw_ref, sb_ref, out_refs[oi])
            continue
        step = min(width, 512)
        for cc in range(0, width, step):
            acc = _dot(h, w_ref[:, c0 + cc:c0 + cc + step])
            if kind == "rope" and rope:
                for k, piece in enumerate(_rope(acc, cos_ref[...], sin_ref[...])):
                    out_refs[oi][:, cc + k * LANES:cc + (k + 1) * LANES] = piece.astype(ACT_DTYPE)
            else:
                out_refs[oi][:, cc:cc + step] = acc.astype(ACT_DTYPE)


def _in_proj(x, mod, mod_row, norm_g, w, layer, segs, rope_tabs, sgu, name):
    bx, l, d = x.shape
    tm = min(IN_PROJ_TM, l)
    assert all(s[0] + s[1] <= OFF_GATE for s in segs)
    n_out = max(s[3] for s in segs) + 1
    out_w = lambda s: BRANCH_DIM if s[2] == "sgu" else s[1]
    widths = [sum(out_w(s) for s in segs if s[3] == oi) for oi in range(n_out)]
    rope = rope_tabs is not None
    use_sgu = any(s[2] == "sgu" for s in segs)
    in_specs = [
        pl.BlockSpec((None, tm, d), lambda b, i: (b, i, 0)),
        _mod_spec(lambda b, i: mod_row(b), 0),
        _mod_spec(lambda b, i: mod_row(b), 1),
        _layer_spec(norm_g, layer),
        pl.BlockSpec((None, d, OFF_GATE), lambda *_: (layer, 0, 0), pipeline_mode=pl.Buffered(1)),
    ]
    args = [x, mod, mod, norm_g, w]
    if rope:
        in_specs += [pl.BlockSpec((tm, LANES), lambda b, i: (i, 0))] * 2
        args += list(rope_tabs)
    if use_sgu:
        in_specs += [_layer_spec(p, layer) for p in sgu]
        args += list(sgu)
    return pl.pallas_call(
        functools.partial(_in_proj_body, segs=segs, rope=rope, sgu=use_sgu, n_out=n_out),
        grid=(bx, l // tm),
        in_specs=in_specs,
        out_specs=[pl.BlockSpec((None, tm, wd), lambda b, i: (b, i, 0)) for wd in widths],
        out_shape=[jax.ShapeDtypeStruct((bx, l, wd), ACT_DTYPE) for wd in widths],
        compiler_params=_cparams(("arbitrary", "arbitrary")),
        name=name,
    )(*args)


IN_PROJ_TM = 1024
FULL_SEGS = (
    (OFF_SGU, 2 * BRANCH_DIM, "sgu", 0),
    (OFF_QKV, BRANCH_DIM, "rope", 1),
    (OFF_QKV + BRANCH_DIM, BRANCH_DIM, "rope", 2),
    (OFF_QKV + 2 * BRANCH_DIM, BRANCH_DIM, "plain", 3),
    (OFF_HY, 3 * HY_DIM, "plain", 4),
)
KV_SEGS = ((OFF_QKV + BRANCH_DIM, BRANCH_DIM, "plain", 0), (OFF_QKV + 2 * BRANCH_DIM, BRANCH_DIM, "plain", 1))


def _att_plan(rows):
    kr = min(NA_WIN_ROWS, rows)
    assert rows % Q_ROWS_PER_BLOCK == 0 and rows >= K_ROWS_PER_BLOCK and kr == NA_WIN_ROWS
    patterns, type_of, key_start = [], [], []
    for rb in range(0, rows, Q_ROWS_PER_BLOCK):
        ks = min(max(rb - NA_WIN_ROWS // 2, 0), rows - K_ROWS_PER_BLOCK)
        assert ks % Q_ROWS_PER_BLOCK == 0
        pat = []
        for qi in range(Q_ROWS_PER_BLOCK):
            rq = rb + qi
            r0 = min(max(rq - kr // 2, 0), rows - kr)
            for j in range(K_ROWS_PER_BLOCK):
                rk = ks + j
                pat.append(rk - rq + NA_WIN_ROWS - 1 if r0 <= rk < r0 + kr else None)
        pat = tuple(pat)
        if pat not in patterns:
            patterns.append(pat)
        type_of.append(patterns.index(pat))
        key_start.append(ks // Q_ROWS_PER_BLOCK)
    return patterns, type_of, key_start


def _bias_body(rpb_ref, o_ref, *, patterns):
    n_dr, n_dc = 2 * NA_WIN_ROWS - 1, 2 * NA_WIN_COLS - 1
    h = pl.program_id(0)
    qc = lax.broadcasted_iota(jnp.int32, (GRID_W, GRID_W), 0)
    kc = lax.broadcasted_iota(jnp.int32, (GRID_W, GRID_W), 1)
    cstart = jnp.clip(qc - NA_WIN_COLS // 2, 0, GRID_W - NA_WIN_COLS)
    col_ok = (kc >= cstart) & (kc < cstart + NA_WIN_COLS)
    dc = jnp.clip(kc - qc + NA_WIN_COLS - 1, 0, n_dc - 1)
    neg = jnp.full((GRID_W, GRID_W), NEG_INF, F32)
    tiles = []
    for dr in range(n_dr):
        t = jnp.zeros((GRID_W, GRID_W), F32)
        for d in range(n_dc):
            t = jnp.where(dc == d, rpb_ref[(h * n_dr + dr) * n_dc + d], t)
        tiles.append(jnp.where(col_ok, t * LOG2E, neg))
    for ty, pat in enumerate(patterns):
        for qi in range(Q_ROWS_PER_BLOCK):
            for j in range(K_ROWS_PER_BLOCK):
                dr = pat[qi * K_ROWS_PER_BLOCK + j]
                o_ref[ty, qi * GRID_W:(qi + 1) * GRID_W, j * GRID_W:(j + 1) * GRID_W] = neg if dr is None else tiles[dr]


def _bias_tables(rpb, patterns):
    nt = len(patterns)
    return pl.pallas_call(
        functools.partial(_bias_body, patterns=patterns),
        grid=(NA_HEADS,),
        in_specs=[pl.BlockSpec(memory_space=pltpu.SMEM)],
        out_specs=pl.BlockSpec((nt, None, ATT_TOK, K_ROWS_PER_BLOCK * GRID_W), lambda h: (0, h, 0, 0)),
        out_shape=jax.ShapeDtypeStruct((nt, NA_HEADS, ATT_TOK, K_ROWS_PER_BLOCK * GRID_W), F32),
        compiler_params=_cparams(("arbitrary",)),
        name="attention_bias_tables",
    )(rpb.reshape(-1))


def _head_mask(shape, hh):
    lane = lax.broadcasted_iota(jnp.int32, shape, 1)
    return (lane >= hh * NA_HEAD_DIM) & (lane < (hh + 1) * NA_HEAD_DIM)


LOG2E = math.log2(math.e)
QK_SCALE = NA_HEAD_DIM ** -0.5 * LOG2E


def _pair_attention(qp, keys, biases, values):
    m_rows = qp.shape[0]
    q2 = jnp.concatenate([jnp.where(_head_mask(qp.shape, hh), qp, jnp.zeros_like(qp)) for hh in range(2)], axis=0)
    s_all = _dot_nt(q2, jnp.concatenate(keys, axis=0))
    scores, c0 = [], 0
    for j, kj in enumerate(keys):
        s = s_all[:, c0:c0 + kj.shape[0]]
        c0 += kj.shape[0]
        if biases[0][j] is not None:
            s = s + jnp.concatenate([biases[0][j], biases[1][j]], axis=0)
        scores.append(s)
    m = functools.reduce(jnp.maximum, [jnp.max(s, axis=-1, keepdims=True) for s in scores])
    e = jnp.concatenate([jnp.exp2(s - m).astype(MXU_DTYPE) for s in scores], axis=1)
    vcat = jnp.concatenate(values, axis=0)
    acc = _dot(e, jnp.concatenate([vcat, jnp.ones_like(vcat)], axis=1))
    out = acc[:, :LANES] / acc[:, LANES:]
    return jnp.where(_head_mask((m_rows, LANES), 0), out[:m_rows], out[m_rows:])


ATT_BATCH = 8


def _attn_body(ty_ref, kb_ref, q_ref, k0, k1, k2, v0, v1, v2, kc_ref, vc_ref, bias_ref, o_ref):
    del ty_ref, kb_ref
    for p in range(NA_HEADS // 2):
        cs = slice(p * LANES, (p + 1) * LANES)
        biases = [[bias_ref[2 * p + hh, :, j * ATT_TOK:(j + 1) * ATT_TOK] for j in range(3)] + [None]
                  for hh in range(2)]
        for e in range(q_ref.shape[0]):
            qp = q_ref[e, :, cs] * QK_SCALE
            keys = [k0[e, :, cs], k1[e, :, cs], k2[e, :, cs], kc_ref[e, :, cs]]
            values = [v0[e, :, cs], v1[e, :, cs], v2[e, :, cs], vc_ref[e, :, cs]]
            o_ref[e, :, cs] = _pair_attention(qp, keys, biases, values).astype(ACT_DTYPE)


def _neighborhood_attention(q, k, v, kc, vc, bias, type_of, key_start):
    b, l, _ = q.shape
    lc = kc.shape[1]
    n_blk = l // ATT_TOK
    nb = math.gcd(b, ATT_BATCH)
    tok = lambda off: pl.BlockSpec((nb, ATT_TOK, BRANCH_DIM), lambda r, bb, ty, kb: (bb, kb[r] + off, 0))
    ctx = pl.BlockSpec((nb, lc, BRANCH_DIM), lambda r, bb, ty, kb: (bb, 0, 0))
    grid_spec = pltpu.PrefetchScalarGridSpec(
        num_scalar_prefetch=2,
        grid=(n_blk, b // nb),
        in_specs=[
            pl.BlockSpec((nb, ATT_TOK, BRANCH_DIM), lambda r, bb, ty, kb: (bb, r, 0)),
            tok(0), tok(1), tok(2), tok(0), tok(1), tok(2), ctx, ctx,
            pl.BlockSpec((None, NA_HEADS, ATT_TOK, K_ROWS_PER_BLOCK * GRID_W), lambda r, bb, ty, kb: (ty[r], 0, 0, 0)),
        ],
        out_specs=pl.BlockSpec((nb, ATT_TOK, BRANCH_DIM), lambda r, bb, ty, kb: (bb, r, 0)),
    )
    return pl.pallas_call(
        _attn_body,
        grid_spec=grid_spec,
        out_shape=jax.ShapeDtypeStruct((b, l, BRANCH_DIM), ACT_DTYPE),
        compiler_params=_cparams(("arbitrary", "arbitrary")),
        name="neighborhood_attention",
    )(jnp.asarray(type_of, jnp.int32), jnp.asarray(key_start, jnp.int32), q, k, k, k, v, v, v, kc, vc, bias)


def _ctx_attn_body(q_ref, k_ref, v_ref, o_ref):
    for p in range(NA_HEADS // 2):
        cs = slice(p * LANES, (p + 1) * LANES)
        qp = q_ref[:, cs] * QK_SCALE
        o_ref[:, cs] = _pair_attention(qp, [k_ref[:, cs]], [[None], [None]], [v_ref[:, cs]]).astype(ACT_DTYPE)


def _context_attention(q, k, v):
    b, lc, _ = q.shape
    spec = pl.BlockSpec((None, lc, BRANCH_DIM), lambda bb: (bb, 0, 0))
    return pl.pallas_call(
        _ctx_attn_body,
        grid=(b,),
        in_specs=[spec, spec, spec],
        out_specs=spec,
        out_shape=jax.ShapeDtypeStruct((b, lc, BRANCH_DIM), ACT_DTYPE),
        compiler_params=_cparams(("arbitrary",)),
        name="context_attention",
    )(q, k, v)


HY_CW = 256
HY_TM = 512
HY_BLOCKS = 4
HY_MIN_BLOCK = 256
CONV_HALO = 16


def _hy_blocks(l):
    return HY_BLOCKS if l % (HY_BLOCKS * HY_MIN_BLOCK) == 0 else 1


@functools.lru_cache(maxsize=None)
def _dft_consts_np(m):
    n = 2 * m
    idx = np.arange(m, dtype=np.int64)
    ang = ((idx[:, None] * idx[None, :]) % n).astype(np.float64) * (2.0 * math.pi / n)
    cs = np.concatenate([np.cos(ang), -np.sin(ang)], axis=1).astype(np.float32)
    sign = np.where(idx % 2 == 0, 1.0, -1.0).astype(np.float32).reshape(m, 1)
    wf = np.where(idx == 0, 1.0 / n, 2.0 / n).astype(np.float32).reshape(m, 1)
    return cs, sign, wf


def _dft_consts(m):
    cs, sign, wf = _dft_consts_np(m)
    sign_rows = np.zeros((8, m), np.float32)
    sign_rows[0] = sign[:, 0]
    return (jnp.asarray(cs.astype(MXU_DTYPE)), jnp.asarray(sign), jnp.asarray(wf),
            jnp.asarray(sign_rows.astype(MXU_DTYPE)))


def _filter_consts(l):
    t = np.linspace(0.0, 1.0, l, dtype=np.float32)[:, None]
    n_bands = (HY_EMB - 1) // 2
    bands = np.linspace(1e-4, n_bands - 1, n_bands, dtype=np.float32)[None, :]
    ang = np.float32(2.0 * math.pi) * np.arange(l, dtype=np.float32)[:, None] / np.float32(l) * bands
    z = np.concatenate([t, np.cos(ang), -np.sin(ang)], axis=-1).astype(np.float32)
    z = np.pad(z, ((0, 0), (0, LANES - HY_EMB)))
    max_decay = math.log(HY_DECAY_TARGET) / HY_FAST_DECAY_PCT
    min_decay = math.log(HY_DECAY_TARGET) / HY_SLOW_DECAY_PCT
    deltas = np.abs(np.linspace(min_decay, max_decay, HY_DIM, dtype=np.float32)).reshape(1, HY_DIM)
    return jnp.asarray(z), jnp.asarray(t), jnp.asarray(deltas)


def _filter_body(z_ref, w1_ref, b1_ref, w2_ref, b2_ref, fr_ref, w3a_ref, w3b_ref, t_ref, dl_ref,
                 sg_ref, wf_ref, cs_ref, skip_ref, gre_ref, gim_ref, gny_ref, h_ref, *, p):
    l = z_ref.shape[0]
    m = l // p

    @pl.when((pl.program_id(0) == 0) & (pl.program_id(1) == 0))
    def _():
        fr = fr_ref[...]
        h1 = jnp.sin(fr * (_dot(z_ref[...], w1_ref[...], precision=HIGHEST) + b1_ref[...]))
        h_ref[...] = jnp.sin(fr * (_dot(h1, w2_ref[...], precision=HIGHEST) + b2_ref[...]))

    h = h_ref[...]
    decay = jnp.exp(-t_ref[...] * dl_ref[...])
    h0 = _dot(h, w3a_ref[...], precision=HIGHEST) * decay
    h1 = _dot(h, w3b_ref[...], precision=HIGHEST) * decay
    row = lax.broadcasted_iota(jnp.int32, h1.shape, 0)
    den = (jnp.sum(jnp.abs(h0), axis=0, keepdims=True)
           + jnp.sum(jnp.where(row == 0, 0.0, jnp.abs(h1)), axis=0, keepdims=True))
    h0, h1 = h0 / den, h1 / den
    sg, wf = sg_ref[...], wf_ref[...]
    u, v, pc, q, un, pn, r0, r1 = [], [], [], [], [], [], [], []
    for b in range(p):
        x0, x1 = h0[b * m:(b + 1) * m], h1[b * m:(b + 1) * m]
        x0b, x1b = x0.astype(MXU_DTYPE), x1.astype(MXU_DTYPE)
        u.append(_dot(cs_ref[:, :m], x0b))
        v.append(_dot(cs_ref[:, m:], x0b))
        pc.append(_dot(cs_ref[:, :m], x1b))
        q.append(_dot(cs_ref[:, m:], x1b))
        un.append(jnp.sum(sg * x0, axis=0, keepdims=True))
        pn.append(jnp.sum(sg * x1, axis=0, keepdims=True))
        r0.append(x0[0:1])
        r1.append(x1[0:1])
    for li, d in enumerate(range(-(p - 1), p)):
        if d > 0:
            gre = u[d] + sg * (u[d - 1] - r0[d - 1])
            gim = v[d] + sg * v[d - 1]
            gny = un[d] + un[d - 1] - r0[d - 1]
        elif d == 0:
            gre = u[0] + pc[0] - r1[0] + skip_ref[...]
            gim = v[0] - q[0]
            gny = un[0] + pn[0] - r1[0] + skip_ref[...]
        else:
            gre = pc[-d] + sg * (pc[-d - 1] - r1[-d - 1])
            gim = -q[-d] - sg * q[-d - 1]
            gny = pn[-d] + pn[-d - 1] - r1[-d - 1]
        gre_ref[li] = (wf * gre).astype(gre_ref.dtype)
        gim_ref[li] = (wf * gim).astype(gim_ref.dtype)
        gny_ref[li] = gny * (1.0 / (2 * m))


def _hyena_filters(l, dft, fconsts, skip, w1, b1, w2, b2, w3, freq):
    cs, sign, wf, _ = dft
    z, t, deltas = fconsts
    p = _hy_blocks(l)
    m, nl = l // p, 2 * p - 1
    cw = HY_CW
    nb = HY_DIM // cw
    w1p = jnp.pad(w1, ((0, LANES - HY_EMB), (0, 0)))
    hid = HY_FILTER_HIDDEN
    small = lambda shape: pl.BlockSpec(shape, lambda n, cb: (0,) * len(shape))
    spec_out = pl.BlockSpec((None, nl, m, cw), lambda n, cb: (n, 0, 0, cb))
    return pl.pallas_call(
        functools.partial(_filter_body, p=p),
        grid=(HY_ORDER, nb),
        in_specs=[
            small((l, LANES)), small((LANES, hid)), small((1, hid)), small((hid, hid)), small((1, hid)), small((1, hid)),
            pl.BlockSpec((hid, cw), lambda n, cb: (0, n * 2 * nb + cb)),
            pl.BlockSpec((hid, cw), lambda n, cb: (0, n * 2 * nb + nb + cb)),
            small((l, 1)),
            pl.BlockSpec((1, cw), lambda n, cb: (0, cb)),
            small((m, 1)), small((m, 1)),
            _const_spec((m, 2 * m)),
            pl.BlockSpec((None, 1, cw), lambda n, cb: (n, 0, cb)),
        ],
        out_specs=[spec_out, spec_out, pl.BlockSpec((None, nl, 1, cw), lambda n, cb: (n, 0, 0, cb))],
        out_shape=[jax.ShapeDtypeStruct((HY_ORDER, nl, m, HY_DIM), ACT_DTYPE)] * 2
        + [jax.ShapeDtypeStruct((HY_ORDER, nl, 1, HY_DIM), F32)],
        scratch_shapes=[pltpu.VMEM((l, hid), F32)],
        compiler_params=_cparams(("arbitrary", "arbitrary")),
        name="hyena_filter_spectrum",
    )(z, w1p, b1.reshape(1, hid), w2, b2.reshape(1, hid), freq.reshape(1, hid), w3, w3, t, deltas, sign, wf, cs,
      skip.reshape(HY_ORDER, 1, HY_DIM))


def _conv3_rows(p_ref, w_ref, b_ref, r0, rows):
    l = p_ref.shape[0]
    lo, hi = max(r0 - CONV_HALO, 0), min(r0 + rows + CONV_HALO, l)
    x = p_ref[lo:hi, :].astype(F32)
    n = hi - lo
    prev, nxt = pltpu.roll(x, 1, 0), pltpu.roll(x, n - 1, 0)
    row = lax.broadcasted_iota(jnp.int32, x.shape, 0)
    if lo == 0:
        prev = jnp.where(row == 0, 0.0, prev)
    if hi == l:
        nxt = jnp.where(row == n - 1, 0.0, nxt)
    y = prev * w_ref[0:1, :] + x * w_ref[1:2, :] + nxt * w_ref[2:3, :] + b_ref[...]
    return y[r0 - lo:r0 - lo + rows]


def _hyena_body(pv_ref, p1_ref, p2_ref, wv_ref, w1_ref, w2_ref, bv_ref, b1_ref, b2_ref,
                gre_ref, gim_ref, gny_ref, sg_ref, sr_ref, cs_ref, o_ref, zb_ref, zz_ref, y_ref, *, p):
    l, cw = pv_ref.shape
    m = l // p
    tm = min(HY_TM, m)
    offs = range(0, m, tm)
    zn = [jnp.zeros((1, cw), F32) for _ in range(p)]

    def put_z(a, c0, z):
        rs = slice(a * m + c0, a * m + c0 + tm)
        zb = z.astype(MXU_DTYPE)
        zb_ref[rs] = zb
        zn[a] = zn[a] + _dot(sr_ref[:, c0:c0 + tm], zb)[0:1]

    for a in range(p):
        for c0 in offs:
            put_z(a, c0, _conv3_rows(pv_ref, wv_ref, bv_ref, a * m + c0, tm))
    gates = ((p1_ref, w1_ref, b1_ref), (p2_ref, w2_ref, b2_ref))
    for n, (p_ref, w_ref, b_ref) in enumerate(gates):
        for a in range(p):
            zblk = zb_ref[a * m:(a + 1) * m]
            for c0 in offs:
                rs = slice(a * m + c0, a * m + c0 + tm)
                zz_ref[0, rs] = _dot(cs_ref[c0:c0 + tm, :m], zblk).astype(zz_ref.dtype)
                zz_ref[1, rs] = _dot(cs_ref[c0:c0 + tm, m:], zblk).astype(zz_ref.dtype)
        for a2 in range(p):
            for c0 in offs:
                yc = ys = None
                for a in range(p):
                    li = a2 - a + p - 1
                    gre, gim = gre_ref[n, li, c0:c0 + tm], gim_ref[n, li, c0:c0 + tm]
                    zc, zs = zz_ref[0, a * m + c0:a * m + c0 + tm], zz_ref[1, a * m + c0:a * m + c0 + tm]
                    tc, ts = zc * gre - zs * gim, zc * gim + zs * gre
                    yc, ys = (tc, ts) if yc is None else (yc + tc, ys + ts)
                y_ref[a2, c0:c0 + tm] = yc.astype(MXU_DTYPE)
                y_ref[a2, m + c0:m + c0 + tm] = ys.astype(MXU_DTYPE)
        nyq = [functools.reduce(lambda s, t: s + t, [zn[a] * gny_ref[n, a2 - a + p - 1] for a in range(p)])
               for a2 in range(p)]
        zn = [jnp.zeros((1, cw), F32) for _ in range(p)]
        for a2 in range(p):
            for c0 in offs:
                rs = slice(a2 * m + c0, a2 * m + c0 + tm)
                y = _dot(cs_ref[c0:c0 + tm, :], y_ref[a2]) + sg_ref[c0:c0 + tm] * nyq[a2]
                gate = _conv3_rows(p_ref, w_ref, b_ref, rs.start, tm)
                z = gate * y
                if n + 1 < HY_ORDER:
                    put_z(a2, c0, z)
                else:
                    o_ref[rs] = z.astype(ACT_DTYPE)


def _hyena(x, conv_w, conv_b, filters, dft):
    bx, l, _ = x.shape
    cs, sign, _, sign_rows = dft
    gre, gim, gny = filters
    p = _hy_blocks(l)
    m, nl = l // p, 2 * p - 1
    cw = HY_CW
    nb = HY_DIM // cw
    pspec = lambda part: pl.BlockSpec((None, l, cw), lambda cb, b: (b, 0, part * nb + cb))
    wspec = lambda part: pl.BlockSpec((3, cw), lambda cb, b: (0, part * nb + cb))
    bspec = lambda part: pl.BlockSpec((1, cw), lambda cb, b: (0, part * nb + cb))
    fspec = pl.BlockSpec((HY_ORDER, nl, m, cw), lambda cb, b: (0, 0, 0, cb), pipeline_mode=pl.Buffered(1))
    return pl.pallas_call(
        functools.partial(_hyena_body, p=p),
        grid=(nb, bx),
        in_specs=[
            pspec(0), pspec(1), pspec(2), wspec(0), wspec(1), wspec(2), bspec(0), bspec(1), bspec(2),
            fspec, fspec,
            pl.BlockSpec((HY_ORDER, nl, 1, cw), lambda cb, b: (0, 0, 0, cb)),
            _const_spec((m, 1)), _const_spec((8, m)), _const_spec((m, 2 * m)),
        ],
        out_specs=pl.BlockSpec((None, l, cw), lambda cb, b: (b, 0, cb)),
        out_shape=jax.ShapeDtypeStruct((bx, l, HY_DIM), ACT_DTYPE),
        scratch_shapes=[pltpu.VMEM((l, cw), MXU_DTYPE), pltpu.VMEM((2, l, cw), ACT_DTYPE),
                        pltpu.VMEM((p, 2 * m, cw), MXU_DTYPE)],
        compiler_params=_cparams(("arbitrary", "arbitrary")),
        name="hyena_long_conv",
    )(x, x, x, conv_w, conv_w, conv_w, conv_b.reshape(1, -1), conv_b.reshape(1, -1), conv_b.reshape(1, -1),
      gre, gim, gny, sign, sign_rows, cs)


MERGE_TM = 1024


def _merge_body(x_ref, a_ref, at_ref, hy_ref, sh_ref, sc_ref, g1_ref, ng_ref, wg0_ref, wg1_ref, wg2_ref,
                wb_ref, wo_ref, o_ref):
    x = x_ref[...]
    ms = jnp.mean(x * x, axis=-1, keepdims=True)
    y = x * lax.rsqrt(ms + NORM_EPS) * ng_ref[...]
    h = (y * (1.0 + sc_ref[...]) + sh_ref[...]).astype(MXU_DTYPE)
    m = None
    for j, (br_ref, wg_ref) in enumerate(((a_ref, wg0_ref), (at_ref, wg1_ref), (hy_ref, wg2_ref))):
        term = jax.nn.sigmoid(_dot(h, wg_ref[...])) * _dot(br_ref[...], wb_ref[j])
        m = term if m is None else m + term
    o_ref[...] = x + g1_ref[...] * _dot(m.astype(MXU_DTYPE), wo_ref[...])


def _merge(x, a, attn, hy, mod, mod_row, norm_g, w_in, wb, wo, layer):
    bx, l, d = x.shape
    tm = min(MERGE_TM, l)
    tile = lambda w: pl.BlockSpec((None, tm, w), lambda b, i: (b, i, 0))
    gate_w = lambda j: pl.BlockSpec((None, d, d), lambda *_: (layer, 0, OFF_GATE // d + j), pipeline_mode=pl.Buffered(1))
    return pl.pallas_call(
        _merge_body,
        grid=(bx, l // tm),
        in_specs=[tile(d), tile(BRANCH_DIM), tile(BRANCH_DIM), tile(BRANCH_DIM),
                  _mod_spec(lambda b, i: mod_row(b), 0), _mod_spec(lambda b, i: mod_row(b), 1),
                  _mod_spec(lambda b, i: mod_row(b), 2), _layer_spec(norm_g, layer),
                  gate_w(0), gate_w(1), gate_w(2), _layer_spec(wb, layer), _layer_spec(wo, layer)],
        out_specs=tile(d),
        out_shape=jax.ShapeDtypeStruct((bx, l, d), F32),
        compiler_params=_cparams(("arbitrary", "arbitrary")),
        name="branch_merge",
    )(x, a, attn, hy, mod, mod, mod, norm_g, w_in, w_in, w_in, wb, wo)


FFN_TM = 1024
FFN_TN = 256


def _ffn_body(xp_ref, x_ref, xn_ref, sh_ref, sc_ref, g2_ref, ng_ref, wu_ref, cw_ref, cb_ref, wd_ref, fg_ref,
              o_ref, h_ref, gt_ref, *, final, seq_len):
    tm = x_ref.shape[0]
    halo = CONV_HALO
    n_ext = tm + 2 * halo

    def norm_mod(x):
        ms = jnp.mean(x * x, axis=-1, keepdims=True)
        y = x * lax.rsqrt(ms + NORM_EPS) * ng_ref[...]
        return (y * (1.0 + sc_ref[...]) + sh_ref[...]).astype(MXU_DTYPE)

    h_ref[0:halo] = norm_mod(xp_ref[...])
    h_ref[halo:halo + tm] = norm_mod(x_ref[...])
    h_ref[halo + tm:n_ext] = norm_mod(xn_ref[...])
    row = lax.broadcasted_iota(jnp.int32, (n_ext, 1), 0)
    pos = (pl.program_id(1) * tm + row + (seq_len - halo)) % seq_len
    first, last = pos == 0, pos == seq_len - 1
    for j in range(D_FF // FFN_TN):
        cs = slice(j * FFN_TN, (j + 1) * FFN_TN)
        a = _dot(h_ref[...], wu_ref[:, cs])
        prev = jnp.where(first, 0.0, pltpu.roll(a, 1, 0))
        nxt = jnp.where(last, 0.0, pltpu.roll(a, n_ext - 1, 0))
        a = prev * cw_ref[0:1, cs] + a * cw_ref[1:2, cs] + nxt * cw_ref[2:3, cs] + cb_ref[:, cs]
        gate = _dot(h_ref[halo:halo + tm], wu_ref[:, D_FF + j * FFN_TN:D_FF + (j + 1) * FFN_TN])
        gt_ref[:, cs] = (_gelu(a[halo:halo + tm]) * gate).astype(MXU_DTYPE)
    xn = x_ref[...] + g2_ref[...] * _dot(gt_ref[...], wd_ref[...])
    if final:
        ms = jnp.mean(xn * xn, axis=-1, keepdims=True)
        xn = xn * lax.rsqrt(ms + NORM_EPS) * fg_ref[...]
    o_ref[...] = xn


def _ffn(x, mod, mod_row, norm_g, w_up, conv_w, conv_b, w_down, layer, final_g, final, seq_len):
    bx, l, d = x.shape
    tm = min(FFN_TM, l)
    halo = CONV_HALO
    per, last_blk = tm // halo, l // halo - 1
    tile = pl.BlockSpec((None, tm, d), lambda b, i: (b, i, 0))
    return pl.pallas_call(
        functools.partial(_ffn_body, final=final, seq_len=seq_len),
        grid=(bx, l // tm),
        in_specs=[
            pl.BlockSpec((None, halo, d), lambda b, i: (b, jnp.maximum(i * per - 1, 0), 0)),
            tile,
            pl.BlockSpec((None, halo, d), lambda b, i: (b, jnp.minimum((i + 1) * per, last_blk), 0)),
            _mod_spec(lambda b, i: mod_row(b), 3), _mod_spec(lambda b, i: mod_row(b), 4),
            _mod_spec(lambda b, i: mod_row(b), 5),
            _layer_spec(norm_g, layer), _layer_spec(w_up, layer), _layer_spec(conv_w, layer),
            _layer_spec(conv_b, layer), _layer_spec(w_down, layer), _const_spec((1, d)),
        ],
        out_specs=tile,
        out_shape=jax.ShapeDtypeStruct((bx, l, d), F32),
        scratch_shapes=[pltpu.VMEM((tm + 2 * halo, d), MXU_DTYPE), pltpu.VMEM((tm, D_FF), MXU_DTYPE)],
        compiler_params=_cparams(("arbitrary", "arbitrary")),
        name="conv_ffn",
    )(x, x, x, mod, mod, mod, norm_g, w_up, conv_w, conv_b, w_down, final_g.reshape(1, d))


def _rope_tables(l):
    half = NA_HEAD_DIM // 2
    quarter = half // 2
    inv_freq = np.float32(ROPE_THETA) ** (-np.arange(quarter, dtype=np.float32) * np.float32(2.0) / np.float32(half))
    pos = np.arange(l)
    ang_r = (pos // GRID_W).astype(np.float32)[:, None] * inv_freq[None, :]
    ang_c = (pos % GRID_W).astype(np.float32)[:, None] * inv_freq[None, :]
    cos_h = np.concatenate([np.cos(ang_r), np.cos(ang_r), np.cos(ang_c), np.cos(ang_c)], axis=-1)
    sin_h = np.concatenate([-np.sin(ang_r), np.sin(ang_r), -np.sin(ang_c), np.sin(ang_c)], axis=-1)
    reps = LANES // NA_HEAD_DIM
    return (jnp.asarray(np.tile(cos_h, (1, reps)).astype(np.float32)),
            jnp.asarray(np.tile(sin_h, (1, reps)).astype(np.float32)))


def kernel(x, c, ctx, c_ctx, ada_w, ada_b, norm1_g, norm2_g, w_in, sgu_norm_g, sgu_w, sgu_b, na_rpb, hy_conv_w, hy_conv_b, hy_filt_w1, hy_filt_b1, hy_filt_w2, hy_filt_b2, hy_filt_w3, hy_sin_freq, hy_skip, w_branch, w_out, ffn_w_up, ffn_conv_w, ffn_conv_b, ffn_w_down, final_norm_g):
    b, l, d = x.shape
    lc = ctx.shape[1]
    assert d == D_MODEL and l % ATT_TOK == 0 and l % GRID_W == 0

    n_rows = -(-(b + 1) // 8) * 8
    cc = jnp.concatenate([c, c_ctx[None, :], jnp.zeros((n_rows - b - 1, d), F32)], axis=0)
    mod = _modulation(cc, ada_w, ada_b).reshape(DEPTH * n_rows, 1, N_MOD * d)

    rope_tabs = _rope_tables(l)
    dft_l, fc_l = _dft_consts(l // _hy_blocks(l)), _filter_consts(l)
    dft_c, fc_c = _dft_consts(lc // _hy_blocks(lc)), _filter_consts(lc)
    patterns, type_of, key_start = _att_plan(l // GRID_W)

    w_in_b, wb_b, wo_b = w_in.astype(MXU_DTYPE), w_branch.astype(MXU_DTYPE), w_out.astype(MXU_DTYPE)
    wup_b, wdn_b = ffn_w_up.astype(MXU_DTYPE), ffn_w_down.astype(MXU_DTYPE)
    n1g, n2g = norm1_g.reshape(DEPTH, 1, d), norm2_g.reshape(DEPTH, 1, d)
    ffn_cb = ffn_conv_b.reshape(DEPTH, 1, D_FF)
    sgu = (sgu_norm_g.reshape(DEPTH, 1, BRANCH_DIM), sgu_w.astype(MXU_DTYPE),
           sgu_b.reshape(DEPTH, SGU_GROUPS, SGU_CHUNK, 1))

    grp = math.gcd(b, max(1, l // lc))
    stack = lambda t: t.reshape(b // grp, grp * lc, t.shape[-1])
    unstack = lambda t: t.reshape(b, lc, t.shape[-1])

    xc = ctx
    for i in range(DEPTH):
        update_ctx = i < DEPTH - 1
        lat_row = lambda bb, i=i: i * n_rows + bb
        ctx_row = lambda bb, i=i: i * n_rows + b
        filt_args = (hy_skip[i], hy_filt_w1[i], hy_filt_b1[i], hy_filt_w2[i], hy_filt_b2[i], hy_filt_w3[i], hy_sin_freq[i])
        bias = _bias_tables(na_rpb[i], patterns)

        if update_ctx:
            a_c, q_c, k_c, v_c, hy_c = map(unstack, _in_proj(stack(xc), mod, ctx_row, n1g, w_in_b, i, FULL_SEGS, None,
                                                             sgu, "context_in_proj"))
        else:
            k_c, v_c = map(unstack, _in_proj(stack(xc), mod, ctx_row, n1g, w_in_b, i, KV_SEGS, None, None,
                                             "context_kv_proj"))

        a, q, k, v, hy = _in_proj(x, mod, lat_row, n1g, w_in_b, i, FULL_SEGS, rope_tabs, sgu, "latent_in_proj")
        attn = _neighborhood_attention(q, k, v, k_c, v_c, bias, type_of, key_start)
        filt = _hyena_filters(l, dft_l, fc_l, *filt_args)
        hyo = _hyena(hy, hy_conv_w[i], hy_conv_b[i], filt, dft_l)
        x = _merge(x, a, attn, hyo, mod, lat_row, n1g, w_in_b, wb_b, wo_b, i)
        x = _ffn(x, mod, lat_row, n2g, wup_b, ffn_conv_w, ffn_cb, wdn_b, i, final_norm_g,
                 final=not update_ctx, seq_len=l)

        if update_ctx:
            attn_c = _context_attention(q_c, k_c, v_c)
            filt_c = _hyena_filters(lc, dft_c, fc_c, *filt_args)
            hyo_c = _hyena(hy_c, hy_conv_w[i], hy_conv_b[i], filt_c, dft_c)
            xs = _merge(stack(xc), stack(a_c), stack(attn_c), stack(hyo_c), mod, ctx_row, n1g, w_in_b, wb_b, wo_b, i)
            xc = unstack(_ffn(xs, mod, ctx_row, n2g, wup_b, ffn_conv_w, ffn_cb, wdn_b, i, final_norm_g,
                              final=False, seq_len=lc))
    return x
```
